```python
import jax, jax.numpy as jnp
from jax import lax
import numpy as np

D_MODEL = 1024
BATCH = 8
SEQ = 4096
DEPTH = 1

GRID_W = 64
CTX_LEN = 256
D_MIX = D_MODEL
D_A = D_MIX // 2
D_B = D_MIX - D_A
CHUNK = 128
SGU_GROUPS = 4
SGU_CH = D_A // SGU_GROUPS
NA_HEAD_DIM = 64
NA_HEADS = D_B // NA_HEAD_DIM
WIN_R = 8
WIN_C = 16
D_IN = 3 * D_A + 4 * D_B
BRANCH_WIDTHS = (D_A, D_A, D_A, D_B, D_B, D_B, D_B)
SPLIT_POINTS = tuple(int(s) for s in np.cumsum(BRANCH_WIDTHS)[:-1])
KV_START = 3 * D_A + D_B
KV_END = 3 * D_A + 3 * D_B
EPS = 1e-6
NEG_INF = -1e30

kernel_name = "hybrid_sgu_natten_prefix_block"


def rms_norm(x, g):
    xf = x.astype(jnp.float32)
    y = xf * lax.rsqrt(jnp.mean(xf * xf, axis=-1, keepdims=True) + EPS)
    return (y * g.astype(jnp.float32)).astype(x.dtype)


def ada_params(cond, w_ada, b_ada):
    mod = jax.nn.silu(cond) @ w_ada + b_ada
    shift, scale, gate = jnp.split(mod, 3, axis=-1)
    return shift[..., None, :], scale[..., None, :], gate[..., None, :]


def chunk_sgu(u, v, g, sgu_g, w_s, b_s):
    B, L, _ = u.shape
    u = jax.nn.gelu(u, approximate=False)
    v = jax.nn.gelu(v, approximate=False).reshape(B, L // CHUNK, CHUNK, SGU_GROUPS, SGU_CH)
    v = rms_norm(v, sgu_g.reshape(SGU_GROUPS, SGU_CH))
    mixed = jnp.einsum('gpq,bnqgc->bnpgc', w_s, v) + b_s.T[:, :, None]
    return u * mixed.reshape(B, L, D_A) * jax.nn.silu(g)


def neighborhood_attention(q, k, v, k_ctx, v_ctx, rpb):
    B, L, H, Dh = q.shape
    rows = L // GRID_W
    wr = min(WIN_R, rows)
    scale = Dh ** -0.5
    qg = q.reshape(B, rows, GRID_W, H, Dh)
    kg = k.reshape(B, rows, GRID_W, H, Dh)
    vg = v.reshape(B, rows, GRID_W, H, Dh)
    r = jnp.arange(rows)
    r0 = jnp.clip(r - WIN_R // 2, 0, rows - wr)
    key_rows = r0[:, None] + jnp.arange(wr)[None, :]
    kw = jnp.take(kg, key_rows, axis=1)
    vw = jnp.take(vg, key_rows, axis=1)
    dr = key_rows - r[:, None] + (WIN_R - 1)
    cols = jnp.arange(GRID_W)
    c0 = jnp.clip(cols - WIN_C // 2, 0, GRID_W - WIN_C)
    in_win = (cols[None, :] >= c0[:, None]) & (cols[None, :] < c0[:, None] + WIN_C)
    dc = jnp.clip(cols[None, :] - cols[:, None] + (WIN_C - 1), 0, 2 * WIN_C - 2)
    bias = rpb.astype(jnp.float32)[:, dr[:, None, :, None], dc[None, :, None, :]]
    bias = bias.transpose(1, 0, 2, 3, 4)
    s_lat = jnp.einsum('brqhd,brjkhd->brhqjk', qg, kw).astype(jnp.float32) * scale + bias[None]
    s_lat = jnp.where(in_win[:, None, :], s_lat, NEG_INF)
    s_ctx = jnp.einsum('brqhd,bchd->brhqc', qg, k_ctx).astype(jnp.float32) * scale
    m = jnp.maximum(jnp.max(s_lat, axis=(-2, -1)), jnp.max(s_ctx, axis=-1))
    p_lat = jnp.exp(s_lat - m[..., None, None])
    p_ctx = jnp.exp(s_ctx - m[..., None])
    denom = jnp.sum(p_lat, axis=(-2, -1)) + jnp.sum(p_ctx, axis=-1)
    out = (jnp.einsum('brhqjk,brjkhd->brqhd', p_lat.astype(v.dtype), vw)
           + jnp.einsum('brhqc,bchd->brqhd', p_ctx.astype(v.dtype), v_ctx))
    out = out / denom.transpose(0, 1, 3, 2)[..., None].astype(out.dtype)
    return out.reshape(B, L, H, Dh)


def context_attention(q, k, v):
    s = jnp.einsum('bqhd,bkhd->bhqk', q, k).astype(jnp.float32) * (q.shape[-1] ** -0.5)
    p = jax.nn.softmax(s, axis=-1).astype(v.dtype)
    return jnp.einsum('bhqk,bkhd->bqhd', p, v)


def setup_inputs(seed: int = 0) -> dict:
    key = jax.random.key(seed)
    ks = jax.random.split(key, 16)
    f32 = jnp.float32
    nrm = lambda k, shape: jax.random.normal(k, shape, dtype=f32)
    return {
        "x": nrm(ks[0], (BATCH, SEQ, D_MODEL)),
        "c": nrm(ks[1], (BATCH, D_MODEL)),
        "ctx": nrm(ks[2], (BATCH, CTX_LEN, D_MODEL)),
        "c_ctx": nrm(ks[3], (D_MODEL,)),
        "w_ada": nrm(ks[4], (DEPTH, D_MODEL, 3 * D_MODEL)) * (0.5 * D_MODEL ** -0.5),
        "b_ada": nrm(ks[5], (DEPTH, 3 * D_MODEL)) * 0.02,
        "norm_g": 1.0 + 0.02 * nrm(ks[6], (DEPTH, D_MODEL)),
        "w_in": nrm(ks[7], (DEPTH, D_MODEL, D_IN)) * D_MODEL ** -0.5,
        "sgu_norm_g": 1.0 + 0.02 * nrm(ks[8], (DEPTH, D_A)),
        "w_spatial": nrm(ks[9], (DEPTH, SGU_GROUPS, CHUNK, CHUNK)) * CHUNK ** -0.5,
        "b_spatial": nrm(ks[10], (DEPTH, SGU_GROUPS, CHUNK)) * 0.02,
        "q_norm_g": 1.0 + 0.02 * nrm(ks[11], (DEPTH, NA_HEAD_DIM)),
        "k_norm_g": 1.0 + 0.02 * nrm(ks[12], (DEPTH, NA_HEAD_DIM)),
        "rpb": nrm(ks[13], (DEPTH, NA_HEADS, 2 * WIN_R - 1, 2 * WIN_C - 1)) * 0.02,
        "w_out": nrm(ks[14], (DEPTH, D_MIX, D_MODEL)) * D_MIX ** -0.5,
    }


def reference(x, c, ctx, c_ctx, w_ada, b_ada, norm_g, w_in, sgu_norm_g, w_spatial,
              b_spatial, q_norm_g, k_norm_g, rpb, w_out):
    B, L, _ = x.shape
    Bc, C, _ = ctx.shape
    for layer in range(DEPTH):
        last = layer == DEPTH - 1
        shift, scale, gate = ada_params(c, w_ada[layer], b_ada[layer])
        cshift, cscale, cgate = ada_params(c_ctx, w_ada[layer], b_ada[layer])
        h = rms_norm(x, norm_g[layer]) * (1.0 + scale) + shift
        hc = rms_norm(ctx, norm_g[layer]) * (1.0 + cscale) + cshift

        z = h @ w_in[layer]
        a_u, a_v, a_g, b_q, b_k, b_v, b_g = jnp.split(z, SPLIT_POINTS, axis=-1)
        q = rms_norm(b_q.reshape(B, L, NA_HEADS, NA_HEAD_DIM), q_norm_g[layer])
        k = rms_norm(b_k.reshape(B, L, NA_HEADS, NA_HEAD_DIM), k_norm_g[layer])
        v = b_v.reshape(B, L, NA_HEADS, NA_HEAD_DIM)

        if last:
            ck, cv = jnp.split(hc @ w_in[layer][:, KV_START:KV_END], 2, axis=-1)
        else:
            zc = hc @ w_in[layer]
            cu, cvv, cga, cq, ck, cv, cgb = jnp.split(zc, SPLIT_POINTS, axis=-1)
        ck = rms_norm(ck.reshape(Bc, C, NA_HEADS, NA_HEAD_DIM), k_norm_g[layer])
        cv = cv.reshape(Bc, C, NA_HEADS, NA_HEAD_DIM)

        out_a = chunk_sgu(a_u, a_v, a_g, sgu_norm_g[layer], w_spatial[layer], b_spatial[layer])
        out_b = neighborhood_attention(q, k, v, ck, cv, rpb[layer]).reshape(B, L, D_B) * jax.nn.silu(b_g)
        mix = jnp.concatenate([out_a, out_b], axis=-1) @ w_out[layer]
        new_x = x + gate * mix

        if not last:
            cq = rms_norm(cq.reshape(Bc, C, NA_HEADS, NA_HEAD_DIM), q_norm_g[layer])
            cout_a = chunk_sgu(cu, cvv, cga, sgu_norm_g[layer], w_spatial[layer], b_spatial[layer])
            cout_b = context_attention(cq, ck, cv).reshape(Bc, C, D_B) * jax.nn.silu(cgb)
            cmix = jnp.concatenate([cout_a, cout_b], axis=-1) @ w_out[layer]
            ctx = ctx + cgate * cmix
        x = new_x
    return x
```

```python
import functools

import numpy as np
import jax
import jax.numpy as jnp
from jax import lax
from jax.experimental import pallas as pl
from jax.experimental.pallas import tpu as pltpu

D_MODEL = 1024
GRID_W = 64
D_A = 512
D_B = 512
CHUNK = 128
SGU_GROUPS = 4
HEAD_DIM = 64
HEADS = 8
HEAD_PAIRS = HEADS // 2
WIN_R = 8
WIN_C = 16
D_IN = 3 * D_A + 4 * D_B
EPS = 1e-6
NEG_INF = -1e30

LANES = 128
MXU_DIM = 256
VMEM_LIMIT = 56 * 1024 * 1024

TM = 512
Q_ROWS = 4
KEY_ROWS = 12
TQ = Q_ROWS * GRID_W
TKW = KEY_ROWS * GRID_W
ADA_ROWS = 16
ADA_BN = 768

_NT = (((1,), (1,)), ((), ()))


def _silu(x):
    return x / (1.0 + jnp.exp(-x))


def _gelu(x):
    return 0.5 * x * (1.0 + lax.erf(x * np.float32(np.sqrt(0.5))))


def _dot(a, b):
    return jnp.dot(a, b, preferred_element_type=jnp.float32)


def _ada_kernel(c_ref, w_ref, b_ref, o_ref):
    a = _silu(c_ref[...])
    o_ref[...] = jnp.dot(a, w_ref[...], preferred_element_type=jnp.float32,
                         precision=lax.Precision.HIGHEST) + b_ref[...]


def _ada_call(cc, w_ada, b_ada):
    n = w_ada.shape[1]
    return pl.pallas_call(
        _ada_kernel,
        grid=(n // ADA_BN,),
        in_specs=[
            pl.BlockSpec((ADA_ROWS, D_MODEL), lambda j: (0, 0)),
            pl.BlockSpec((D_MODEL, ADA_BN), lambda j: (0, j)),
            pl.BlockSpec((1, ADA_BN), lambda j: (0, j)),
        ],
        out_specs=pl.BlockSpec((ADA_ROWS, ADA_BN), lambda j: (0, j)),
        out_shape=jax.ShapeDtypeStruct((ADA_ROWS, n), jnp.float32),
        compiler_params=pltpu.CompilerParams(dimension_semantics=("arbitrary",)),
        name="ada_params",
    )(cc, w_ada, b_ada)


def _prenorm(x, g, shift, scale):
    ms = jnp.mean(x * x, axis=-1, keepdims=True)
    h = x * lax.rsqrt(ms + EPS) * g
    return (h * (1.0 + scale) + shift).astype(jnp.bfloat16)


def _head_rms(z, bd, gain):
    sq = (z * z).astype(jnp.bfloat16)
    ss = jnp.concatenate(
        [_dot(sq[:, c * MXU_DIM:(c + 1) * MXU_DIM], bd) for c in range(D_B // MXU_DIM)], axis=-1)
    return z * lax.rsqrt(ss * (1.0 / HEAD_DIM) + EPS) * gain


def _store_pairs(ref, z):
    for p in range(HEAD_PAIRS):
        ref[0, p] = z[:, p * LANES:(p + 1) * LANES].astype(ref.dtype)


def _proj_kernel(x_ref, shift_ref, scale_ref, ng_ref, w_ref, sg_ref, ws_ref, bs_ref,
                 qg_ref, kg_ref, bd_ref,
                 oa_ref, gb_ref, q_ref, k_ref, v_ref):
    hb = _prenorm(x_ref[0], ng_ref[...], shift_ref[0], scale_ref[0])

    def zcols(lo, width):
        return _dot(hb, w_ref[:, lo:lo + width])

    gu = _gelu(zcols(0, D_A))
    gv = _gelu(zcols(D_A, D_A))
    sa = _silu(zcols(2 * D_A, D_A))
    for g in range(SGU_GROUPS):
        cs = slice(g * LANES, (g + 1) * LANES)
        vg = gv[:, cs]
        ms = jnp.mean(vg * vg, axis=-1, keepdims=True)
        vn = (vg * lax.rsqrt(ms + EPS) * sg_ref[:, cs]).astype(jnp.bfloat16)
        for c in range(TM // CHUNK):
            rs = slice(c * CHUNK, (c + 1) * CHUNK)
            mixed = _dot(ws_ref[g], vn[rs]) + bs_ref[:, cs]
            oa_ref[0, rs, cs] = (gu[rs, cs] * mixed * sa[rs, cs]).astype(oa_ref.dtype)

    bd = bd_ref[...]
    _store_pairs(q_ref, _head_rms(zcols(3 * D_A, D_B), bd, qg_ref[...]))
    _store_pairs(k_ref, _head_rms(zcols(3 * D_A + D_B, D_B), bd, kg_ref[...]))
    _store_pairs(v_ref, zcols(3 * D_A + 2 * D_B, D_B))
    gb_ref[0] = _silu(zcols(3 * D_A + 3 * D_B, D_B)).astype(gb_ref.dtype)


def _proj_call(x, shift, scale, ng, w_in, sg, ws, bs, qg, kg, bd):
    B, L, _ = x.shape
    const2 = lambda b, i: (0, 0)
    pair_spec = pl.BlockSpec((1, HEAD_PAIRS, TM, LANES), lambda b, i: (b, 0, i, 0))
    half_spec = pl.BlockSpec((1, TM, D_A), lambda b, i: (b, i, 0))
    pair_shape = jax.ShapeDtypeStruct((B, HEAD_PAIRS, L, LANES), jnp.bfloat16)
    half_shape = jax.ShapeDtypeStruct((B, L, D_A), jnp.bfloat16)
    return pl.pallas_call(
        _proj_kernel,
        grid=(B, L // TM),
        in_specs=[
            pl.BlockSpec((1, TM, D_MODEL), lambda b, i: (b, i, 0)),
            pl.BlockSpec((1, 1, D_MODEL), lambda b, i: (b, 0, 0)),
            pl.BlockSpec((1, 1, D_MODEL), lambda b, i: (b, 0, 0)),
            pl.BlockSpec((1, D_MODEL), const2),
            pl.BlockSpec((D_MODEL, D_IN), const2),
            pl.BlockSpec((1, D_A), const2),
            pl.BlockSpec((SGU_GROUPS, CHUNK, CHUNK), lambda b, i: (0, 0, 0)),
            pl.BlockSpec((CHUNK, D_A), const2),
            pl.BlockSpec((1, D_B), const2),
            pl.BlockSpec((1, D_B), const2),
            pl.BlockSpec((MXU_DIM, MXU_DIM), const2),
        ],
        out_specs=[half_spec, half_spec, pair_spec, pair_spec, pair_spec],
        out_shape=[half_shape, half_shape, pair_shape, pair_shape, pair_shape],
        compiler_params=pltpu.CompilerParams(
            dimension_semantics=("arbitrary", "arbitrary"), vmem_limit_bytes=VMEM_LIMIT),
        name="latent_proj",
    )(x, shift, scale, ng, w_in, sg, ws, bs, qg, kg, bd)


def _ctx_kernel(x_ref, shift_ref, scale_ref, ng_ref, w_ref, kg_ref, bd_ref, k_ref, v_ref):
    hb = _prenorm(x_ref[0], ng_ref[...], shift_ref[...], scale_ref[...])
    _store_pairs(k_ref, _head_rms(_dot(hb, w_ref[:, 0:D_B]), bd_ref[...], kg_ref[...]))
    _store_pairs(v_ref, _dot(hb, w_ref[:, D_B:2 * D_B]))


def _ctx_call(ctx, cshift, cscale, ng, w_in, kg, bd):
    B, C, _ = ctx.shape
    const2 = lambda b: (0, 0)
    kv_block = (3 * D_A + D_B) // (2 * D_B)
    pair_spec = pl.BlockSpec((1, HEAD_PAIRS, C, LANES), lambda b: (b, 0, 0, 0))
    pair_shape = jax.ShapeDtypeStruct((B, HEAD_PAIRS, C, LANES), jnp.bfloat16)
    return pl.pallas_call(
        _ctx_kernel,
        grid=(B,),
        in_specs=[
            pl.BlockSpec((1, C, D_MODEL), lambda b: (b, 0, 0)),
            pl.BlockSpec((1, D_MODEL), const2),
            pl.BlockSpec((1, D_MODEL), const2),
            pl.BlockSpec((1, D_MODEL), const2),
            pl.BlockSpec((D_MODEL, 2 * D_B), lambda b: (0, kv_block)),
            pl.BlockSpec((1, D_B), const2),
            pl.BlockSpec((MXU_DIM, MXU_DIM), const2),
        ],
        out_specs=[pair_spec, pair_spec],
        out_shape=[pair_shape, pair_shape],
        compiler_params=pltpu.CompilerParams(
            dimension_semantics=("arbitrary",), vmem_limit_bytes=VMEM_LIMIT),
        name="ctx_kv",
    )(ctx, cshift, cscale, ng, w_in, kg, bd)


def _key_row_start(i, n_steps):
    return jnp.clip(Q_ROWS * i - WIN_R // 2, 0, n_steps * Q_ROWS - KEY_ROWS)


def _attn_kernel(x_ref, gate_ref, oa_ref, gb_ref, q_ref, k_ref, v_ref, kc_ref, vc_ref,
                 tab_ref, wout_ref, o_ref, mix_ref):
    i = pl.program_id(1)
    tok0 = pl.multiple_of(_key_row_start(i, pl.num_programs(1)) * GRID_W, GRID_W)
    lane_head = lax.broadcasted_iota(jnp.int32, (TQ, LANES), 1) // HEAD_DIM

    mix_ref[:, 0:D_A] = oa_ref[0]
    for p in range(HEAD_PAIRS):
        qp = q_ref[0, p]
        kw = k_ref[0, p, pl.ds(tok0, TKW), :]
        vw = v_ref[0, p, pl.ds(tok0, TKW), :]
        kc = kc_ref[0, p]
        vc = vc_ref[0, p]
        outs = []
        for hh in range(2):
            qh = jnp.where(lane_head == hh, qp, jnp.zeros_like(qp))
            s_w = lax.dot_general(qh, kw, _NT, preferred_element_type=jnp.float32)
            s_w = s_w + tab_ref[0, 2 * p + hh]
            s_c = lax.dot_general(qh, kc, _NT, preferred_element_type=jnp.float32)
            m = jnp.maximum(jnp.max(s_w, axis=-1, keepdims=True),
                            jnp.max(s_c, axis=-1, keepdims=True))
            p_w = jnp.exp(s_w - m)
            p_c = jnp.exp(s_c - m)
            denom = jnp.sum(p_w, axis=-1, keepdims=True) + jnp.sum(p_c, axis=-1, keepdims=True)
            o = _dot(p_w.astype(jnp.bfloat16), vw) + _dot(p_c.astype(jnp.bfloat16), vc)
            outs.append(o / denom)
        o_pair = jnp.where(lane_head == 0, outs[0], outs[1])
        cs = slice(p * LANES, (p + 1) * LANES)
        mix_ref[:, D_A + p * LANES:D_A + (p + 1) * LANES] = (
            o_pair * gb_ref[0, :, cs].astype(jnp.float32)).astype(mix_ref.dtype)

    mix = _dot(mix_ref[...], wout_ref[...])
    o_ref[0] = x_ref[0] + gate_ref[0] * mix


def _attn_call(x, gate, oa, gb, q, k, v, kc, vc, tab, w_out):
    B, L, _ = x.shape
    C = kc.shape[2]
    n_steps = L // TQ

    def tab_index(b, i):
        t = jnp.where(i == 0, 0, jnp.where(i == n_steps - 1, 2, 1))
        return (t, 0, 0, 0)

    tok_spec = lambda width: pl.BlockSpec((1, TQ, width), lambda b, i: (b, i, 0))
    full_pairs = lambda n: pl.BlockSpec((1, HEAD_PAIRS, n, LANES), lambda b, i: (b, 0, 0, 0))
    return pl.pallas_call(
        _attn_kernel,
        grid=(B, n_steps),
        in_specs=[
            tok_spec(D_MODEL),
            pl.BlockSpec((1, 1, D_MODEL), lambda b, i: (b, 0, 0)),
            tok_spec(D_A),
            tok_spec(D_B),
            pl.BlockSpec((1, HEAD_PAIRS, TQ, LANES), lambda b, i: (b, 0, i, 0)),
            full_pairs(L),
            full_pairs(L),
            full_pairs(C),
            full_pairs(C),
            pl.BlockSpec((1, HEADS, TQ, TKW), tab_index),
            pl.BlockSpec((D_A + D_B, D_MODEL), lambda b, i: (0, 0)),
        ],
        out_specs=tok_spec(D_MODEL),
        out_shape=jax.ShapeDtypeStruct((B, L, D_MODEL), jnp.float32),
        scratch_shapes=[pltpu.VMEM((TQ, D_A + D_B), jnp.bfloat16)],
        compiler_params=pltpu.CompilerParams(
            dimension_semantics=("arbitrary", "arbitrary"), vmem_limit_bytes=VMEM_LIMIT),
        name="nbr_attn_out",
    )(x, gate, oa, gb, q, k, v, kc, vc, tab, w_out)


def _bias_tables(rpb, rows):
    n_steps = rows // Q_ROWS
    cols = np.arange(GRID_W)
    c0 = np.clip(cols - WIN_C // 2, 0, GRID_W - WIN_C)
    in_win = (cols[None, :] >= c0[:, None]) & (cols[None, :] < c0[:, None] + WIN_C)
    dc = np.clip(cols[None, :] - cols[:, None] + (WIN_C - 1), 0, 2 * WIN_C - 2)
    tabs = []
    for step in (0, n_steps // 2, n_steps - 1):
        start = int(np.clip(Q_ROWS * step - WIN_R // 2, 0, rows - KEY_ROWS))
        r = Q_ROWS * step + np.arange(Q_ROWS)
        r0 = np.clip(r - WIN_R // 2, 0, rows - WIN_R)
        key_row = start + np.arange(KEY_ROWS)
        row_ok = (key_row[None, :] >= r0[:, None]) & (key_row[None, :] < r0[:, None] + WIN_R)
        dr = np.clip(key_row[None, :] - r[:, None] + (WIN_R - 1), 0, 2 * WIN_R - 2)
        dr_idx = np.broadcast_to(dr[:, None, :, None], (Q_ROWS, GRID_W, KEY_ROWS, GRID_W))
        dc_idx = np.broadcast_to(dc[None, :, None, :], (Q_ROWS, GRID_W, KEY_ROWS, GRID_W))
        ok = row_ok[:, None, :, None] & in_win[None, :, None, :]
        bias = rpb.astype(jnp.float32)[:, dr_idx.reshape(TQ, TKW), dc_idx.reshape(TQ, TKW)]
        tabs.append(jnp.where(ok.reshape(1, TQ, TKW), bias, NEG_INF))
    return jnp.stack(tabs)


def kernel(x, c, ctx, c_ctx, w_ada, b_ada, norm_g, w_in, sgu_norm_g, w_spatial, b_spatial,
           q_norm_g, k_norm_g, rpb, w_out):
    B, L, D = x.shape
    depth = w_ada.shape[0]
    f32, bf16 = jnp.float32, jnp.bfloat16
    bd = jnp.asarray(np.kron(np.eye(MXU_DIM // HEAD_DIM), np.ones((HEAD_DIM, HEAD_DIM))), bf16)
    cc = jnp.zeros((ADA_ROWS, D), f32).at[:B].set(c).at[B].set(c_ctx)
    assert depth == 1
    for layer in range(depth):
        mod = _ada_call(cc, w_ada[layer], b_ada[layer][None, :])
        shift, scale, gate = (mod[:B, j * D:(j + 1) * D].reshape(B, 1, D) for j in range(3))
        cshift, cscale = (mod[B:B + 1, j * D:(j + 1) * D] for j in range(2))

        ng = norm_g[layer][None, :]
        w_in_b = w_in[layer].astype(bf16)
        kg = jnp.tile(k_norm_g[layer], HEADS)[None, :]
        qg = jnp.tile(q_norm_g[layer], HEADS)[None, :] * (HEAD_DIM ** -0.5)
        bs = jnp.repeat(b_spatial[layer].T, D_A // SGU_GROUPS, axis=1)
        oa, gb, q, k, v = _proj_call(
            x, shift, scale, ng, w_in_b, sgu_norm_g[layer][None, :],
            w_spatial[layer].astype(bf16), bs, qg, kg, bd)
        kc, vc = _ctx_call(ctx, cshift, cscale, ng, w_in_b, kg, bd)
        tab = _bias_tables(rpb[layer], L // GRID_W)
        x = _attn_call(x, gate, oa, gb, q, k, v, kc, vc, tab, w_out[layer].astype(bf16))
    return x
```

```python
import numpy as np
import jax
import jax.numpy as jnp
from jax import lax
from jax.experimental import pallas as pl
from jax.experimental.pallas import tpu as pltpu

D_MODEL = 1024
GRID_W = 64
D_A = 512
D_B = 512
CHUNK = 128
SGU_GROUPS = 4
HEAD_DIM = 64
HEADS = 8
HEAD_PAIRS = HEADS // 2
WIN_R = 8
WIN_C = 16
D_IN = 3 * D_A + 4 * D_B
EPS = 1e-6
NEG_INF = -1e30

LANES = 128
MXU_DIM = 256
VMEM_LIMIT = 56 * 1024 * 1024

TM = 512
Q_ROWS = 4
KEY_ROWS = 12
TQ = Q_ROWS * GRID_W
TKW = KEY_ROWS * GRID_W
KEY_PAIRS = KEY_ROWS // 2
ADA_ROWS = 16
ADA_BN = 768

_NT = (((1,), (1,)), ((), ()))


def _silu(x):
    return x / (1.0 + jnp.exp(-x))


def _gelu(x):
    return 0.5 * x * (1.0 + lax.erf(x * np.float32(np.sqrt(0.5))))


def _dot(a, b):
    return jnp.dot(a, b, preferred_element_type=jnp.float32)


def _ada_kernel(c_ref, w_ref, b_ref, o_ref):
    a = _silu(c_ref[...])
    o_ref[...] = jnp.dot(a, w_ref[...], preferred_element_type=jnp.float32,
                         precision=lax.Precision.HIGHEST) + b_ref[...]


def _ada_call(cc, w_ada, b_ada):
    n = w_ada.shape[1]
    return pl.pallas_call(
        _ada_kernel,
        grid=(n // ADA_BN,),
        in_specs=[
            pl.BlockSpec((ADA_ROWS, D_MODEL), lambda j: (0, 0)),
            pl.BlockSpec((D_MODEL, ADA_BN), lambda j: (0, j)),
            pl.BlockSpec((1, ADA_BN), lambda j: (0, j)),
        ],
        out_specs=pl.BlockSpec((ADA_ROWS, ADA_BN), lambda j: (0, j)),
        out_shape=jax.ShapeDtypeStruct((ADA_ROWS, n), jnp.float32),
        compiler_params=pltpu.CompilerParams(dimension_semantics=("arbitrary",)),
        name="ada_params",
    )(cc, w_ada, b_ada)


def _prenorm(x, g, shift, scale):
    ms = jnp.mean(x * x, axis=-1, keepdims=True)
    h = x * lax.rsqrt(ms + EPS) * g
    return (h * (1.0 + scale) + shift).astype(jnp.bfloat16)


def _head_rms(z, bd, gain):
    sq = (z * z).astype(jnp.bfloat16)
    ss = jnp.concatenate(
        [_dot(sq[:, c * MXU_DIM:(c + 1) * MXU_DIM], bd) for c in range(D_B // MXU_DIM)], axis=-1)
    return z * lax.rsqrt(ss * (1.0 / HEAD_DIM) + EPS) * gain


def _store_pairs(ref, z):
    for p in range(HEAD_PAIRS):
        ref[0, p] = z[:, p * LANES:(p + 1) * LANES].astype(ref.dtype)


def _proj_kernel(x_ref, shift_ref, scale_ref, ng_ref, w_ref, sg_ref, ws_ref, bs_ref,
                 qg_ref, kg_ref, bd_ref,
                 oa_ref, gb_ref, q_ref, k_ref, v_ref):
    hb = _prenorm(x_ref[0], ng_ref[...], shift_ref[0], scale_ref[0])

    def zcols(lo, width):
        return _dot(hb, w_ref[:, lo:lo + width])

    gu = _gelu(zcols(0, D_A))
    gv = _gelu(zcols(D_A, D_A))
    sa = _silu(zcols(2 * D_A, D_A))
    for g in range(SGU_GROUPS):
        cs = slice(g * LANES, (g + 1) * LANES)
        vg = gv[:, cs]
        ms = jnp.mean(vg * vg, axis=-1, keepdims=True)
        vn = (vg * lax.rsqrt(ms + EPS) * sg_ref[:, cs]).astype(jnp.bfloat16)
        for c in range(TM // CHUNK):
            rs = slice(c * CHUNK, (c + 1) * CHUNK)
            mixed = _dot(ws_ref[g], vn[rs]) + bs_ref[:, cs]
            oa_ref[0, rs, cs] = (gu[rs, cs] * mixed * sa[rs, cs]).astype(oa_ref.dtype)

    bd = bd_ref[...]
    _store_pairs(q_ref, _head_rms(zcols(3 * D_A, D_B), bd, qg_ref[...]))
    _store_pairs(k_ref, _head_rms(zcols(3 * D_A + D_B, D_B), bd, kg_ref[...]))
    _store_pairs(v_ref, zcols(3 * D_A + 2 * D_B, D_B))
    gb_ref[0] = _silu(zcols(3 * D_A + 3 * D_B, D_B)).astype(gb_ref.dtype)


def _proj_call(x, shift, scale, ng, w_in, sg, ws, bs, qg, kg, bd):
    B, L, _ = x.shape
    const2 = lambda b, i: (0, 0)
    pair_spec = pl.BlockSpec((1, HEAD_PAIRS, TM, LANES), lambda b, i: (b, 0, i, 0))
    half_spec = pl.BlockSpec((1, TM, D_A), lambda b, i: (b, i, 0))
    pair_shape = jax.ShapeDtypeStruct((B, HEAD_PAIRS, L, LANES), jnp.bfloat16)
    half_shape = jax.ShapeDtypeStruct((B, L, D_A), jnp.bfloat16)
    return pl.pallas_call(
        _proj_kernel,
        grid=(B, L // TM),
        in_specs=[
            pl.BlockSpec((1, TM, D_MODEL), lambda b, i: (b, i, 0)),
            pl.BlockSpec((1, 1, D_MODEL), lambda b, i: (b, 0, 0)),
            pl.BlockSpec((1, 1, D_MODEL), lambda b, i: (b, 0, 0)),
            pl.BlockSpec((1, D_MODEL), const2),
            pl.BlockSpec((D_MODEL, D_IN), const2),
            pl.BlockSpec((1, D_A), const2),
            pl.BlockSpec((SGU_GROUPS, CHUNK, CHUNK), lambda b, i: (0, 0, 0)),
            pl.BlockSpec((CHUNK, D_A), const2),
            pl.BlockSpec((1, D_B), const2),
            pl.BlockSpec((1, D_B), const2),
            pl.BlockSpec((MXU_DIM, MXU_DIM), const2),
        ],
        out_specs=[half_spec, half_spec, pair_spec, pair_spec, pair_spec],
        out_shape=[half_shape, half_shape, pair_shape, pair_shape, pair_shape],
        compiler_params=pltpu.CompilerParams(
            dimension_semantics=("arbitrary", "arbitrary"), vmem_limit_bytes=VMEM_LIMIT),
        name="latent_proj",
    )(x, shift, scale, ng, w_in, sg, ws, bs, qg, kg, bd)


def _ctx_kernel(x_ref, shift_ref, scale_ref, ng_ref, w_ref, kg_ref, bd_ref, k_ref, v_ref):
    hb = _prenorm(x_ref[0], ng_ref[...], shift_ref[...], scale_ref[...])
    _store_pairs(k_ref, _head_rms(_dot(hb, w_ref[:, 0:D_B]), bd_ref[...], kg_ref[...]))
    _store_pairs(v_ref, _dot(hb, w_ref[:, D_B:2 * D_B]))


def _ctx_call(ctx, cshift, cscale, ng, w_in, kg, bd):
    B, C, _ = ctx.shape
    const2 = lambda b: (0, 0)
    kv_block = (3 * D_A + D_B) // (2 * D_B)
    pair_spec = pl.BlockSpec((1, HEAD_PAIRS, C, LANES), lambda b: (b, 0, 0, 0))
    pair_shape = jax.ShapeDtypeStruct((B, HEAD_PAIRS, C, LANES), jnp.bfloat16)
    return pl.pallas_call(
        _ctx_kernel,
        grid=(B,),
        in_specs=[
            pl.BlockSpec((1, C, D_MODEL), lambda b: (b, 0, 0)),
            pl.BlockSpec((1, D_MODEL), const2),
            pl.BlockSpec((1, D_MODEL), const2),
            pl.BlockSpec((1, D_MODEL), const2),
            pl.BlockSpec((D_MODEL, 2 * D_B), lambda b: (0, kv_block)),
            pl.BlockSpec((1, D_B), const2),
            pl.BlockSpec((MXU_DIM, MXU_DIM), const2),
        ],
        out_specs=[pair_spec, pair_spec],
        out_shape=[pair_shape, pair_shape],
        compiler_params=pltpu.CompilerParams(
            dimension_semantics=("arbitrary",), vmem_limit_bytes=VMEM_LIMIT),
        name="ctx_kv",
    )(ctx, cshift, cscale, ng, w_in, kg, bd)


def _window_geometry(rows):
    n_steps = rows // Q_ROWS
    starts = np.clip(Q_ROWS * np.arange(n_steps) - WIN_R // 2, 0, rows - KEY_ROWS)
    entries, codes = [], np.zeros((n_steps, Q_ROWS, KEY_PAIRS), np.int32)
    for i in range(n_steps):
        for rho in range(Q_ROWS):
            r = Q_ROWS * i + rho
            r0 = min(max(r - WIN_R // 2, 0), rows - WIN_R)
            for p in range(KEY_PAIRS):
                pair = []
                for key_row in (starts[i] + 2 * p, starts[i] + 2 * p + 1):
                    ok = r0 <= key_row < r0 + WIN_R
                    pair.append(int(key_row - r + WIN_R - 1) if ok else None)
                pair = tuple(pair)
                if pair not in entries:
                    entries.append(pair)
                codes[i, rho, p] = entries.index(pair)
    return codes, entries


def _rpb_kernel(rpb_ref, dc_ref, o_ref):
    dc_map = dc_ref[...]
    acc = jnp.full(o_ref.shape, NEG_INF, jnp.float32)
    for dc in range(2 * WIN_C - 1):
        acc = jnp.where(dc_map == dc, rpb_ref[:, dc:dc + 1], acc)
    o_ref[...] = acc


def _rpb_call(rpb):
    H, n_dr, n_dc = rpb.shape
    cols = np.arange(GRID_W)
    c0 = np.clip(cols - WIN_C // 2, 0, GRID_W - WIN_C)
    in_win = (cols[None, :] >= c0[:, None]) & (cols[None, :] < c0[:, None] + WIN_C)
    dc = np.clip(cols[None, :] - cols[:, None] + (WIN_C - 1), 0, 2 * WIN_C - 2)
    dc_map = jnp.asarray(np.where(in_win, dc, -1).reshape(1, GRID_W * GRID_W), jnp.int32)
    rpb2 = jnp.pad(rpb.astype(jnp.float32).reshape(H * n_dr, n_dc), ((0, 0), (0, LANES - n_dc)))
    bn = 4 * LANES
    return pl.pallas_call(
        _rpb_kernel,
        grid=(GRID_W * GRID_W // bn,),
        in_specs=[pl.BlockSpec((H * n_dr, LANES), lambda j: (0, 0)),
                  pl.BlockSpec((1, bn), lambda j: (0, j))],
        out_specs=pl.BlockSpec((H * n_dr, bn), lambda j: (0, j)),
        out_shape=jax.ShapeDtypeStruct((H * n_dr, GRID_W * GRID_W), jnp.float32),
        compiler_params=pltpu.CompilerParams(dimension_semantics=("arbitrary",)),
        name="rpb_toeplitz",
    )(rpb2, dc_map).reshape(H, n_dr, GRID_W, GRID_W)


def _pair_table(tcol, entries):
    neg = jnp.full(tcol[:, 0].shape, NEG_INF, tcol.dtype)
    half = lambda d: neg if d is None else tcol[:, d]
    return jnp.stack([jnp.concatenate([half(dl), half(dr)], axis=-1) for dl, dr in entries], axis=1)


def _attn_kernel(codes_ref, x_ref, gate_ref, oa_ref, gb_ref, q_ref, k_ref, v_ref, kc_ref, vc_ref,
                 tab_ref, wout_ref, o_ref, mix_ref):
    i = pl.program_id(1)
    n_steps = pl.num_programs(1)
    start = jnp.clip(Q_ROWS * i - WIN_R // 2, 0, n_steps * Q_ROWS - KEY_ROWS)
    tok0 = pl.multiple_of(start * GRID_W, GRID_W)
    lane_head = lax.broadcasted_iota(jnp.int32, (TQ, LANES), 1) // HEAD_DIM

    def bias(h):
        rows = []
        for rho in range(Q_ROWS):
            base = (i * Q_ROWS + rho) * KEY_PAIRS
            rows.append(jnp.concatenate(
                [tab_ref[h, codes_ref[base + p]] for p in range(KEY_PAIRS)], axis=1))
        return jnp.concatenate(rows, axis=0)

    mix_ref[:, 0:D_A] = oa_ref[0]
    for p in range(HEAD_PAIRS):
        qp = q_ref[0, p]
        kw = k_ref[0, p, pl.ds(tok0, TKW), :]
        vw = v_ref[0, p, pl.ds(tok0, TKW), :]
        kc = kc_ref[0, p]
        vc = vc_ref[0, p]
        outs = []
        for hh in range(2):
            qh = jnp.where(lane_head == hh, qp, jnp.zeros_like(qp))
            s_w = lax.dot_general(qh, kw, _NT, preferred_element_type=jnp.float32)
            s_w = s_w + bias(2 * p + hh)
            s_c = lax.dot_general(qh, kc, _NT, preferred_element_type=jnp.float32)
            m = jnp.maximum(jnp.max(s_w, axis=-1, keepdims=True),
                            jnp.max(s_c, axis=-1, keepdims=True))
            p_w = jnp.exp(s_w - m)
            p_c = jnp.exp(s_c - m)
            denom = jnp.sum(p_w, axis=-1, keepdims=True) + jnp.sum(p_c, axis=-1, keepdims=True)
            o = _dot(p_w.astype(jnp.bfloat16), vw) + _dot(p_c.astype(jnp.bfloat16), vc)
            outs.append(o / denom)
        o_pair = jnp.where(lane_head == 0, outs[0], outs[1])
        cs = slice(p * LANES, (p + 1) * LANES)
        mix_ref[:, D_A + p * LANES:D_A + (p + 1) * LANES] = (
            o_pair * gb_ref[0, :, cs].astype(jnp.float32)).astype(mix_ref.dtype)

    mix = _dot(mix_ref[...], wout_ref[...])
    o_ref[0] = x_ref[0] + gate_ref[0] * mix


def _attn_call(codes, x, gate, oa, gb, q, k, v, kc, vc, tab, w_out):
    B, L, _ = x.shape
    C = kc.shape[2]
    n_entries = tab.shape[1]
    tok_spec = lambda width: pl.BlockSpec((1, TQ, width), lambda b, i, c: (b, i, 0))
    full_pairs = lambda n: pl.BlockSpec((1, HEAD_PAIRS, n, LANES), lambda b, i, c: (b, 0, 0, 0))
    grid_spec = pltpu.PrefetchScalarGridSpec(
        num_scalar_prefetch=1,
        grid=(B, L // TQ),
        in_specs=[
            tok_spec(D_MODEL),
            pl.BlockSpec((1, 1, D_MODEL), lambda b, i, c: (b, 0, 0)),
            tok_spec(D_A),
            tok_spec(D_B),
            pl.BlockSpec((1, HEAD_PAIRS, TQ, LANES), lambda b, i, c: (b, 0, i, 0)),
            full_pairs(L),
            full_pairs(L),
            full_pairs(C),
            full_pairs(C),
            pl.BlockSpec((HEADS, n_entries, GRID_W, LANES), lambda b, i, c: (0, 0, 0, 0)),
            pl.BlockSpec((D_A + D_B, D_MODEL), lambda b, i, c: (0, 0)),
        ],
        out_specs=tok_spec(D_MODEL),
        scratch_shapes=[pltpu.VMEM((TQ, D_A + D_B), jnp.bfloat16)],
    )
    return pl.pallas_call(
        _attn_kernel,
        grid_spec=grid_spec,
        out_shape=jax.ShapeDtypeStruct((B, L, D_MODEL), jnp.float32),
        compiler_params=pltpu.CompilerParams(
            dimension_semantics=("arbitrary", "arbitrary"), vmem_limit_bytes=VMEM_LIMIT),
        name="nbr_attn_out",
    )(codes, x, gate, oa, gb, q, k, v, kc, vc, tab, w_out)


def kernel(x, c, ctx, c_ctx, w_ada, b_ada, norm_g, w_in, sgu_norm_g, w_spatial, b_spatial,
           q_norm_g, k_norm_g, rpb, w_out):
    B, L, D = x.shape
    depth = w_ada.shape[0]
    f32, bf16 = jnp.float32, jnp.bfloat16
    bd = jnp.asarray(np.kron(np.eye(MXU_DIM // HEAD_DIM), np.ones((HEAD_DIM, HEAD_DIM))), bf16)
    cc = jnp.zeros((ADA_ROWS, D), f32).at[:B].set(c).at[B].set(c_ctx)
    codes, entries = _window_geometry(L // GRID_W)
    codes = jnp.asarray(codes.reshape(-1))
    assert depth == 1
    for layer in range(depth):
        mod = _ada_call(cc, w_ada[layer], b_ada[layer][None, :])
        shift, scale, gate = (mod[:B, j * D:(j + 1) * D].reshape(B, 1, D) for j in range(3))
        cshift, cscale = (mod[B:B + 1, j * D:(j + 1) * D] for j in range(2))

        ng = norm_g[layer][None, :]
        w_in_b = w_in[layer].astype(bf16)
        kg = jnp.tile(k_norm_g[layer], HEADS)[None, :]
        qg = jnp.tile(q_norm_g[layer], HEADS)[None, :] * (HEAD_DIM ** -0.5)
        bs = jnp.repeat(b_spatial[layer].T, D_A // SGU_GROUPS, axis=1)
        oa, gb, q, k, v = _proj_call(
            x, shift, scale, ng, w_in_b, sgu_norm_g[layer][None, :],
            w_spatial[layer].astype(bf16), bs, qg, kg, bd)
        kc, vc = _ctx_call(ctx, cshift, cscale, ng, w_in_b, kg, bd)
        tab = _pair_table(_rpb_call(rpb[layer]), entries)
        x = _attn_call(codes, x, gate, oa, gb, q, k, v, kc, vc, tab, w_out[layer].astype(bf16))
    return x
```

```python
import numpy as np
import jax
import jax.numpy as jnp
from jax import lax
from jax.experimental import pallas as pl
from jax.experimental.pallas import tpu as pltpu

D_MODEL = 1024
GRID_W = 64
D_A = 512
D_B = 512
CHUNK = 128
SGU_GROUPS = 4
HEAD_DIM = 64
HEADS = 8
HEAD_GROUP = 4
GROUP_W = HEAD_GROUP * HEAD_DIM
WIN_R = 8
WIN_C = 16
D_IN = 3 * D_A + 4 * D_B
EPS = 1e-6
NEG_INF = -1e30

LANES = 128
MXU_DIM = 256
VMEM_LIMIT = 56 * 1024 * 1024

TM = 512
Q_ROWS = 4
TQ = Q_ROWS * GRID_W
TKW = WIN_R * GRID_W
ADA_ROWS = 16
ADA_BN = 768

_NT = (((1,), (1,)), ((), ()))


def _silu(x):
    return x / (1.0 + jnp.exp(-x))


def _gelu(x):
    return 0.5 * x * (1.0 + lax.erf(x * np.float32(np.sqrt(0.5))))


def _dot(a, b):
    return jnp.dot(a, b, preferred_element_type=jnp.float32)


def _ada_kernel(c_ref, w_ref, b_ref, o_ref):
    a = _silu(c_ref[...])
    o_ref[...] = jnp.dot(a, w_ref[...], preferred_element_type=jnp.float32,
                         precision=lax.Precision.HIGHEST) + b_ref[...]


def _ada_call(cc, w_ada, b_ada):
    n = w_ada.shape[1]
    return pl.pallas_call(
        _ada_kernel,
        grid=(n // ADA_BN,),
        in_specs=[
            pl.BlockSpec((ADA_ROWS, D_MODEL), lambda j: (0, 0)),
            pl.BlockSpec((D_MODEL, ADA_BN), lambda j: (0, j)),
            pl.BlockSpec((1, ADA_BN), lambda j: (0, j)),
        ],
        out_specs=pl.BlockSpec((ADA_ROWS, ADA_BN), lambda j: (0, j)),
        out_shape=jax.ShapeDtypeStruct((ADA_ROWS, n), jnp.float32),
        compiler_params=pltpu.CompilerParams(dimension_semantics=("arbitrary",)),
        name="ada_params",
    )(cc, w_ada, b_ada)


def _prenorm(x, g, shift, scale):
    ms = jnp.mean(x * x, axis=-1, keepdims=True)
    h = x * lax.rsqrt(ms + EPS) * g
    return (h * (1.0 + scale) + shift).astype(jnp.bfloat16)


def _head_rms(z, bd, gain):
    sq = (z * z).astype(jnp.bfloat16)
    ss = jnp.concatenate(
        [_dot(sq[:, c * MXU_DIM:(c + 1) * MXU_DIM], bd) for c in range(D_B // MXU_DIM)], axis=-1)
    return z * lax.rsqrt(ss * (1.0 / HEAD_DIM) + EPS) * gain


def _proj_kernel(x_ref, shift_ref, scale_ref, ng_ref, w_ref, sg_ref, ws_ref, bs_ref,
                 qg_ref, kg_ref, bd_ref,
                 oa_ref, gb_ref, q_ref, k_ref, v_ref):
    hb = _prenorm(x_ref[0], ng_ref[...], shift_ref[0], scale_ref[0])

    def zcols(lo, width):
        return _dot(hb, w_ref[:, lo:lo + width])

    gu = _gelu(zcols(0, D_A))
    gv = _gelu(zcols(D_A, D_A))
    sa = _silu(zcols(2 * D_A, D_A))
    for g in range(SGU_GROUPS):
        cs = slice(g * LANES, (g + 1) * LANES)
        vg = gv[:, cs]
        ms = jnp.mean(vg * vg, axis=-1, keepdims=True)
        vn = (vg * lax.rsqrt(ms + EPS) * sg_ref[:, cs]).astype(jnp.bfloat16)
        chunks = [slice(c * CHUNK, (c + 1) * CHUNK) for c in range(TM // CHUNK)]
        mixed = _dot(ws_ref[g], jnp.concatenate([vn[rs] for rs in chunks], axis=1))
        for c, rs in enumerate(chunks):
            mixed_c = mixed[:, c * CHUNK:(c + 1) * CHUNK] + bs_ref[:, cs]
            oa_ref[0, rs, cs] = (gu[rs, cs] * mixed_c * sa[rs, cs]).astype(oa_ref.dtype)

    bd = bd_ref[...]
    q_ref[0] = _head_rms(zcols(3 * D_A, D_B), bd, qg_ref[...]).astype(q_ref.dtype)
    k_ref[0] = _head_rms(zcols(3 * D_A + D_B, D_B), bd, kg_ref[...]).astype(k_ref.dtype)
    v_ref[0] = zcols(3 * D_A + 2 * D_B, D_B).astype(v_ref.dtype)
    gb_ref[0] = _silu(zcols(3 * D_A + 3 * D_B, D_B)).astype(gb_ref.dtype)


def _proj_call(x, shift, scale, ng, w_in, sg, ws, bs, qg, kg, bd):
    B, L, _ = x.shape
    const2 = lambda b, i: (0, 0)
    half_spec = pl.BlockSpec((1, TM, D_A), lambda b, i: (b, i, 0))
    half_shape = jax.ShapeDtypeStruct((B, L, D_A), jnp.bfloat16)
    return pl.pallas_call(
        _proj_kernel,
        grid=(B, L // TM),
        in_specs=[
            pl.BlockSpec((1, TM, D_MODEL), lambda b, i: (b, i, 0)),
            pl.BlockSpec((1, 1, D_MODEL), lambda b, i: (b, 0, 0)),
            pl.BlockSpec((1, 1, D_MODEL), lambda b, i: (b, 0, 0)),
            pl.BlockSpec((1, D_MODEL), const2),
            pl.BlockSpec((D_MODEL, D_IN), const2),
            pl.BlockSpec((1, D_A), const2),
            pl.BlockSpec((SGU_GROUPS, CHUNK, CHUNK), lambda b, i: (0, 0, 0)),
            pl.BlockSpec((CHUNK, D_A), const2),
            pl.BlockSpec((1, D_B), const2),
            pl.BlockSpec((1, D_B), const2),
            pl.BlockSpec((MXU_DIM, MXU_DIM), const2),
        ],
        out_specs=[half_spec] * 5,
        out_shape=[half_shape] * 5,
        compiler_params=pltpu.CompilerParams(
            dimension_semantics=("arbitrary", "arbitrary"), vmem_limit_bytes=VMEM_LIMIT),
        name="latent_proj",
    )(x, shift, scale, ng, w_in, sg, ws, bs, qg, kg, bd)


def _ctx_kernel(x_ref, shift_ref, scale_ref, ng_ref, w_ref, kg_ref, bd_ref, k_ref, v_ref):
    hb = _prenorm(x_ref[0], ng_ref[...], shift_ref[...], scale_ref[...])
    k_ref[0] = _head_rms(_dot(hb, w_ref[:, 0:D_B]), bd_ref[...], kg_ref[...]).astype(k_ref.dtype)
    v_ref[0] = _dot(hb, w_ref[:, D_B:2 * D_B]).astype(v_ref.dtype)


def _ctx_call(ctx, cshift, cscale, ng, w_in, kg, bd):
    B, C, _ = ctx.shape
    const2 = lambda b: (0, 0)
    kv_block = (3 * D_A + D_B) // (2 * D_B)
    kv_spec = pl.BlockSpec((1, C, D_B), lambda b: (b, 0, 0))
    kv_shape = jax.ShapeDtypeStruct((B, C, D_B), jnp.bfloat16)
    return pl.pallas_call(
        _ctx_kernel,
        grid=(B,),
        in_specs=[
            pl.BlockSpec((1, C, D_MODEL), lambda b: (b, 0, 0)),
            pl.BlockSpec((1, D_MODEL), const2),
            pl.BlockSpec((1, D_MODEL), const2),
            pl.BlockSpec((1, D_MODEL), const2),
            pl.BlockSpec((D_MODEL, 2 * D_B), lambda b: (0, kv_block)),
            pl.BlockSpec((1, D_B), const2),
            pl.BlockSpec((MXU_DIM, MXU_DIM), const2),
        ],
        out_specs=[kv_spec, kv_spec],
        out_shape=[kv_shape, kv_shape],
        compiler_params=pltpu.CompilerParams(
            dimension_semantics=("arbitrary",), vmem_limit_bytes=VMEM_LIMIT),
        name="ctx_kv",
    )(ctx, cshift, cscale, ng, w_in, kg, bd)


def _rpb_kernel(rpb_ref, dc_ref, o_ref):
    dc_map = dc_ref[...]
    acc = jnp.full(o_ref.shape, NEG_INF, jnp.float32)
    for dc in range(2 * WIN_C - 1):
        acc = jnp.where(dc_map == dc, rpb_ref[:, dc:dc + 1], acc)
    o_ref[...] = acc


def _rpb_call(rpb):
    H, n_dr, n_dc = rpb.shape
    cols = np.arange(GRID_W)
    c0 = np.clip(cols - WIN_C // 2, 0, GRID_W - WIN_C)
    in_win = (cols[None, :] >= c0[:, None]) & (cols[None, :] < c0[:, None] + WIN_C)
    dc = np.clip(cols[None, :] - cols[:, None] + (WIN_C - 1), 0, 2 * WIN_C - 2)
    dc_map = jnp.asarray(np.where(in_win, dc, -1).reshape(1, GRID_W * GRID_W), jnp.int32)
    rpb2 = jnp.pad(rpb.astype(jnp.float32).reshape(H * n_dr, n_dc), ((0, 0), (0, LANES - n_dc)))
    bn = 4 * LANES
    return pl.pallas_call(
        _rpb_kernel,
        grid=(GRID_W * GRID_W // bn,),
        in_specs=[pl.BlockSpec((H * n_dr, LANES), lambda j: (0, 0)),
                  pl.BlockSpec((1, bn), lambda j: (0, j))],
        out_specs=pl.BlockSpec((H * n_dr, bn), lambda j: (0, j)),
        out_shape=jax.ShapeDtypeStruct((H * n_dr, GRID_W * GRID_W), jnp.float32),
        compiler_params=pltpu.CompilerParams(dimension_semantics=("arbitrary",)),
        name="rpb_toeplitz",
    )(rpb2, dc_map).reshape(H, n_dr, GRID_W, GRID_W)


def _pair_table(tcol):
    return jnp.concatenate([tcol[:, :-1], tcol[:, 1:]], axis=-1)


def _attn_kernel(x_ref, gate_ref, oa_ref, gb_ref, q_ref, k_ref, v_ref, kc_ref, vc_ref,
                 tab_ref, wout_ref, o_ref, mix_ref):
    i = pl.program_id(1)
    rows = pl.num_programs(1) * Q_ROWS
    shape = (GROUP_W, GROUP_W)
    row_head = lax.broadcasted_iota(jnp.int32, shape, 0) // HEAD_DIM
    lane_head = lax.broadcasted_iota(jnp.int32, shape, 1) // HEAD_DIM
    own_head = row_head == lane_head

    mix_ref[:, 0:D_A] = oa_ref[0]
    for rho in range(Q_ROWS):
        r = i * Q_ROWS + rho
        r0 = jnp.clip(r - WIN_R // 2, 0, rows - WIN_R)
        d0 = r0 - r + (WIN_R - 1)
        tok0 = pl.multiple_of(r0 * GRID_W, GRID_W)
        qrows = slice(rho * GRID_W, (rho + 1) * GRID_W)
        for g in range(HEADS // HEAD_GROUP):
            lanes = slice(g * GROUP_W, (g + 1) * GROUP_W)
            qr = q_ref[0, qrows, lanes]
            q_bd = jnp.where(own_head, jnp.concatenate([qr] * HEAD_GROUP, axis=0),
                             jnp.zeros(shape, qr.dtype))
            kw = k_ref[0, pl.ds(tok0, TKW), lanes]
            vw = v_ref[0, pl.ds(tok0, TKW), lanes]
            kc = kc_ref[0, :, lanes]
            vc = vc_ref[0, :, lanes]
            bias = jnp.concatenate(
                [jnp.concatenate([tab_ref[g * HEAD_GROUP + h, d0 + 2 * t] for t in range(WIN_R // 2)],
                                 axis=1) for h in range(HEAD_GROUP)], axis=0)
            s_w = lax.dot_general(q_bd, kw, _NT, preferred_element_type=jnp.float32) + bias
            s_c = lax.dot_general(q_bd, kc, _NT, preferred_element_type=jnp.float32)
            m = jnp.maximum(jnp.max(s_w, axis=-1, keepdims=True),
                            jnp.max(s_c, axis=-1, keepdims=True))
            p_w = jnp.exp(s_w - m)
            p_c = jnp.exp(s_c - m)
            denom = jnp.sum(p_w, axis=-1, keepdims=True) + jnp.sum(p_c, axis=-1, keepdims=True)
            o = _dot(p_w.astype(jnp.bfloat16), vw) + _dot(p_c.astype(jnp.bfloat16), vc)
            o = jnp.where(own_head, o / denom, 0.0)
            og = o[0:GRID_W]
            for h in range(1, HEAD_GROUP):
                og = og + o[h * GRID_W:(h + 1) * GRID_W]
            mix_ref[qrows, D_A + g * GROUP_W:D_A + (g + 1) * GROUP_W] = (
                og * gb_ref[0, qrows, lanes].astype(jnp.float32)).astype(mix_ref.dtype)

    mix = _dot(mix_ref[...], wout_ref[...])
    o_ref[0] = x_ref[0] + gate_ref[0] * mix


def _attn_call(x, gate, oa, gb, q, k, v, kc, vc, tab, w_out):
    B, L, _ = x.shape
    C = kc.shape[1]
    tok_spec = lambda width: pl.BlockSpec((1, TQ, width), lambda b, i: (b, i, 0))
    batch_spec = lambda n: pl.BlockSpec((1, n, D_B), lambda b, i: (b, 0, 0))
    return pl.pallas_call(
        _attn_kernel,
        grid=(B, L // TQ),
        in_specs=[
            tok_spec(D_MODEL),
            pl.BlockSpec((1, 1, D_MODEL), lambda b, i: (b, 0, 0)),
            tok_spec(D_A),
            tok_spec(D_B),
            tok_spec(D_B),
            batch_spec(L),
            batch_spec(L),
            batch_spec(C),
            batch_spec(C),
            pl.BlockSpec(tab.shape, lambda b, i: (0, 0, 0, 0)),
            pl.BlockSpec((D_A + D_B, D_MODEL), lambda b, i: (0, 0)),
        ],
        out_specs=tok_spec(D_MODEL),
        out_shape=jax.ShapeDtypeStruct((B, L, D_MODEL), jnp.float32),
        scratch_shapes=[pltpu.VMEM((TQ, D_A + D_B), jnp.bfloat16)],
        compiler_params=pltpu.CompilerParams(
            dimension_semantics=("arbitrary", "arbitrary"), vmem_limit_bytes=VMEM_LIMIT),
        name="nbr_attn_out",
    )(x, gate, oa, gb, q, k, v, kc, vc, tab, w_out)


def kernel(x, c, ctx, c_ctx, w_ada, b_ada, norm_g, w_in, sgu_norm_g, w_spatial, b_spatial,
           q_norm_g, k_norm_g, rpb, w_out):
    B, L, D = x.shape
    depth = w_ada.shape[0]
    f32, bf16 = jnp.float32, jnp.bfloat16
    bd = jnp.asarray(np.kron(np.eye(MXU_DIM // HEAD_DIM), np.ones((HEAD_DIM, HEAD_DIM))), bf16)
    cc = jnp.zeros((ADA_ROWS, D), f32).at[:B].set(c).at[B].set(c_ctx)
    assert depth == 1
    for layer in range(depth):
        mod = _ada_call(cc, w_ada[layer], b_ada[layer][None, :])
        shift, scale, gate = (mod[:B, j * D:(j + 1) * D].reshape(B, 1, D) for j in range(3))
        cshift, cscale = (mod[B:B + 1, j * D:(j + 1) * D] for j in range(2))

        ng = norm_g[layer][None, :]
        w_in_b = w_in[layer].astype(bf16)
        kg = jnp.tile(k_norm_g[layer], HEADS)[None, :]
        qg = jnp.tile(q_norm_g[layer], HEADS)[None, :] * (HEAD_DIM ** -0.5)
        bs = jnp.repeat(b_spatial[layer].T, D_A // SGU_GROUPS, axis=1)
        oa, gb, q, k, v = _proj_call(
            x, shift, scale, ng, w_in_b, sgu_norm_g[layer][None, :],
            w_spatial[layer].astype(bf16), bs, qg, kg, bd)
        kc, vc = _ctx_call(ctx, cshift, cscale, ng, w_in_b, kg, bd)
        tab = _pair_table(_rpb_call(rpb[layer]))
        x = _attn_call(x, gate, oa, gb, q, k, v, kc, vc, tab, w_out[layer].astype(bf16))
    return x
```

```python
import numpy as np
import jax
import jax.numpy as jnp
from jax import lax
from jax.experimental import pallas as pl
from jax.experimental.pallas import tpu as pltpu

D_MODEL = 1024
GRID_W = 64
D_A = 512
D_B = 512
CHUNK = 128
SGU_GROUPS = 4
HEAD_DIM = 64
HEADS = 8
HEAD_GROUP = 4
GROUP_W = HEAD_GROUP * HEAD_DIM
WIN_R = 8
WIN_C = 16
D_IN = 3 * D_A + 4 * D_B
EPS = 1e-6
NEG_INF = -1e30

LANES = 128
MXU_DIM = 256
VMEM_LIMIT = 56 * 1024 * 1024

TM = 1024
Q_ROWS = 8
OUT_ROWS = 4
TQ = Q_ROWS * GRID_W
TKW = WIN_R * GRID_W
ADA_ROWS = 16
ADA_BN = 768

_NT = (((1,), (1,)), ((), ()))


def _silu(x):
    return x / (1.0 + jnp.exp(-x))


def _gelu(x):
    return 0.5 * x * (1.0 + lax.erf(x * np.float32(np.sqrt(0.5))))


def _dot(a, b):
    return jnp.dot(a, b, preferred_element_type=jnp.float32)


def _ada_kernel(c_ref, w_ref, b_ref, o_ref):
    a = _silu(c_ref[...])
    o_ref[...] = jnp.dot(a, w_ref[...], preferred_element_type=jnp.float32,
                         precision=lax.Precision.HIGHEST) + b_ref[...]


def _ada_call(cc, w_ada, b_ada):
    n = w_ada.shape[1]
    return pl.pallas_call(
        _ada_kernel,
        grid=(n // ADA_BN,),
        in_specs=[
            pl.BlockSpec((ADA_ROWS, D_MODEL), lambda j: (0, 0)),
            pl.BlockSpec((D_MODEL, ADA_BN), lambda j: (0, j)),
            pl.BlockSpec((1, ADA_BN), lambda j: (0, j)),
        ],
        out_specs=pl.BlockSpec((ADA_ROWS, ADA_BN), lambda j: (0, j)),
        out_shape=jax.ShapeDtypeStruct((ADA_ROWS, n), jnp.float32),
        compiler_params=pltpu.CompilerParams(dimension_semantics=("arbitrary",)),
        name="ada_params",
    )(cc, w_ada, b_ada)


def _prenorm(x, g, shift, scale):
    ms = jnp.mean(x * x, axis=-1, keepdims=True)
    h = x * lax.rsqrt(ms + EPS) * g
    return (h * (1.0 + scale) + shift).astype(jnp.bfloat16)


def _head_rms(z, bd, gain):
    sq = (z * z).astype(jnp.bfloat16)
    ss = jnp.concatenate(
        [_dot(sq[:, c * MXU_DIM:(c + 1) * MXU_DIM], bd) for c in range(D_B // MXU_DIM)], axis=-1)
    return z * lax.rsqrt(ss * (1.0 / HEAD_DIM) + EPS) * gain


def _proj_kernel(x_ref, shift_ref, scale_ref, ng_ref, w_ref, sg_ref, ws_ref, bs_ref,
                 qg_ref, kg_ref, bd_ref,
                 oa_ref, gb_ref, q_ref, k_ref, v_ref):
    hb = _prenorm(x_ref[0], ng_ref[...], shift_ref[0], scale_ref[0])

    def zcols(lo, width):
        return _dot(hb, w_ref[:, lo:lo + width])

    gu = _gelu(zcols(0, D_A))
    gv = _gelu(zcols(D_A, D_A))
    sa = _silu(zcols(2 * D_A, D_A))
    for g in range(SGU_GROUPS):
        cs = slice(g * LANES, (g + 1) * LANES)
        vg = gv[:, cs]
        ms = jnp.mean(vg * vg, axis=-1, keepdims=True)
        vn = (vg * lax.rsqrt(ms + EPS) * sg_ref[:, cs]).astype(jnp.bfloat16)
        chunks = [slice(c * CHUNK, (c + 1) * CHUNK) for c in range(TM // CHUNK)]
        mixed = _dot(ws_ref[g], jnp.concatenate([vn[rs] for rs in chunks], axis=1))
        for c, rs in enumerate(chunks):
            mixed_c = mixed[:, c * CHUNK:(c + 1) * CHUNK] + bs_ref[:, cs]
            oa_ref[0, rs, cs] = (gu[rs, cs] * mixed_c * sa[rs, cs]).astype(oa_ref.dtype)

    bd = bd_ref[...]
    q_ref[0] = _head_rms(zcols(3 * D_A, D_B), bd, qg_ref[...]).astype(q_ref.dtype)
    k_ref[0] = _head_rms(zcols(3 * D_A + D_B, D_B), bd, kg_ref[...]).astype(k_ref.dtype)
    v_ref[0] = zcols(3 * D_A + 2 * D_B, D_B).astype(v_ref.dtype)
    gb_ref[0] = _silu(zcols(3 * D_A + 3 * D_B, D_B)).astype(gb_ref.dtype)


def _proj_call(x, shift, scale, ng, w_in, sg, ws, bs, qg, kg, bd):
    B, L, _ = x.shape
    const2 = lambda b, i: (0, 0)
    half_spec = pl.BlockSpec((1, TM, D_A), lambda b, i: (b, i, 0))
    half_shape = jax.ShapeDtypeStruct((B, L, D_A), jnp.bfloat16)
    return pl.pallas_call(
        _proj_kernel,
        grid=(B, L // TM),
        in_specs=[
            pl.BlockSpec((1, TM, D_MODEL), lambda b, i: (b, i, 0)),
            pl.BlockSpec((1, 1, D_MODEL), lambda b, i: (b, 0, 0)),
            pl.BlockSpec((1, 1, D_MODEL), lambda b, i: (b, 0, 0)),
            pl.BlockSpec((1, D_MODEL), const2),
            pl.BlockSpec((D_MODEL, D_IN), const2),
            pl.BlockSpec((1, D_A), const2),
            pl.BlockSpec((SGU_GROUPS, CHUNK, CHUNK), lambda b, i: (0, 0, 0)),
            pl.BlockSpec((CHUNK, D_A), const2),
            pl.BlockSpec((1, D_B), const2),
            pl.BlockSpec((1, D_B), const2),
            pl.BlockSpec((MXU_DIM, MXU_DIM), const2),
        ],
        out_specs=[half_spec] * 5,
        out_shape=[half_shape] * 5,
        compiler_params=pltpu.CompilerParams(
            dimension_semantics=("arbitrary", "arbitrary"), vmem_limit_bytes=VMEM_LIMIT),
        name="latent_proj",
    )(x, shift, scale, ng, w_in, sg, ws, bs, qg, kg, bd)


def _ctx_kernel(x_ref, shift_ref, scale_ref, ng_ref, w_ref, kg_ref, bd_ref, k_ref, v_ref):
    hb = _prenorm(x_ref[0], ng_ref[...], shift_ref[...], scale_ref[...])
    k_ref[0] = _head_rms(_dot(hb, w_ref[:, 0:D_B]), bd_ref[...], kg_ref[...]).astype(k_ref.dtype)
    v_ref[0] = _dot(hb, w_ref[:, D_B:2 * D_B]).astype(v_ref.dtype)


def _ctx_call(ctx, cshift, cscale, ng, w_in, kg, bd):
    B, C, _ = ctx.shape
    const2 = lambda b: (0, 0)
    kv_block = (3 * D_A + D_B) // (2 * D_B)
    kv_spec = pl.BlockSpec((1, C, D_B), lambda b: (b, 0, 0))
    kv_shape = jax.ShapeDtypeStruct((B, C, D_B), jnp.bfloat16)
    return pl.pallas_call(
        _ctx_kernel,
        grid=(B,),
        in_specs=[
            pl.BlockSpec((1, C, D_MODEL), lambda b: (b, 0, 0)),
            pl.BlockSpec((1, D_MODEL), const2),
            pl.BlockSpec((1, D_MODEL), const2),
            pl.BlockSpec((1, D_MODEL), const2),
            pl.BlockSpec((D_MODEL, 2 * D_B), lambda b: (0, kv_block)),
            pl.BlockSpec((1, D_B), const2),
            pl.BlockSpec((MXU_DIM, MXU_DIM), const2),
        ],
        out_specs=[kv_spec, kv_spec],
        out_shape=[kv_shape, kv_shape],
        compiler_params=pltpu.CompilerParams(
            dimension_semantics=("arbitrary",), vmem_limit_bytes=VMEM_LIMIT),
        name="ctx_kv",
    )(ctx, cshift, cscale, ng, w_in, kg, bd)


def _rpb_kernel(rpb_ref, dc_ref, o_ref):
    dc_map = dc_ref[...]
    acc = jnp.full(o_ref.shape, NEG_INF, jnp.float32)
    for dc in range(2 * WIN_C - 1):
        acc = jnp.where(dc_map == dc, rpb_ref[:, dc:dc + 1], acc)
    o_ref[...] = acc


def _rpb_call(rpb):
    H, n_dr, n_dc = rpb.shape
    cols = np.arange(GRID_W)
    c0 = np.clip(cols - WIN_C // 2, 0, GRID_W - WIN_C)
    in_win = (cols[None, :] >= c0[:, None]) & (cols[None, :] < c0[:, None] + WIN_C)
    dc = np.clip(cols[None, :] - cols[:, None] + (WIN_C - 1), 0, 2 * WIN_C - 2)
    dc_map = jnp.asarray(np.where(in_win, dc, -1).reshape(1, GRID_W * GRID_W), jnp.int32)
    rpb2 = jnp.pad(rpb.astype(jnp.float32).reshape(H * n_dr, n_dc), ((0, 0), (0, LANES - n_dc)))
    bn = 4 * LANES
    return pl.pallas_call(
        _rpb_kernel,
        grid=(GRID_W * GRID_W // bn,),
        in_specs=[pl.BlockSpec((H * n_dr, LANES), lambda j: (0, 0)),
                  pl.BlockSpec((1, bn), lambda j: (0, j))],
        out_specs=pl.BlockSpec((H * n_dr, bn), lambda j: (0, j)),
        out_shape=jax.ShapeDtypeStruct((H * n_dr, GRID_W * GRID_W), jnp.float32),
        compiler_params=pltpu.CompilerParams(dimension_semantics=("arbitrary",)),
        name="rpb_toeplitz",
    )(rpb2, dc_map).reshape(H, n_dr, GRID_W, GRID_W)


def _pair_table(tcol):
    return jnp.concatenate([tcol[:, :-1], tcol[:, 1:]], axis=-1)


def _attn_kernel(x_ref, gate_ref, oa_ref, gb_ref, q_ref, k_ref, v_ref, kc_ref, vc_ref,
                 tab_ref, wout_ref, o_ref, mix_ref):
    i = pl.program_id(1)
    rows = pl.num_programs(1) * Q_ROWS
    shape = (GROUP_W, GROUP_W)
    row_head = lax.broadcasted_iota(jnp.int32, shape, 0) // HEAD_DIM
    lane_head = lax.broadcasted_iota(jnp.int32, shape, 1) // HEAD_DIM
    own_head = row_head == lane_head

    def out_proj(first_row):
        ts = slice(first_row * GRID_W, (first_row + OUT_ROWS) * GRID_W)
        o_ref[0, ts] = x_ref[0, ts] + gate_ref[0] * _dot(mix_ref[ts], wout_ref[...])

    mix_ref[:, 0:D_A] = oa_ref[0]
    for rho in range(Q_ROWS):
        if rho % OUT_ROWS == 0 and rho > 0:
            out_proj(rho - OUT_ROWS)
        r = i * Q_ROWS + rho
        r0 = jnp.clip(r - WIN_R // 2, 0, rows - WIN_R)
        d0 = r0 - r + (WIN_R - 1)
        tok0 = pl.multiple_of(r0 * GRID_W, GRID_W)
        qrows = slice(rho * GRID_W, (rho + 1) * GRID_W)
        for g in range(HEADS // HEAD_GROUP):
            lanes = slice(g * GROUP_W, (g + 1) * GROUP_W)
            qr = q_ref[0, qrows, lanes]
            q_bd = jnp.where(own_head, jnp.concatenate([qr] * HEAD_GROUP, axis=0),
                             jnp.zeros(shape, qr.dtype))
            kw = k_ref[0, pl.ds(tok0, TKW), lanes]
            vw = v_ref[0, pl.ds(tok0, TKW), lanes]
            kc = kc_ref[0, :, lanes]
            vc = vc_ref[0, :, lanes]
            bias = jnp.concatenate(
                [jnp.concatenate([tab_ref[g * HEAD_GROUP + h, d0 + 2 * t] for t in range(WIN_R // 2)],
                                 axis=1) for h in range(HEAD_GROUP)], axis=0)
            s_w = lax.dot_general(q_bd, kw, _NT, preferred_element_type=jnp.float32) + bias
            s_c = lax.dot_general(q_bd, kc, _NT, preferred_element_type=jnp.float32)
            m = jnp.maximum(jnp.max(s_w, axis=-1, keepdims=True),
                            jnp.max(s_c, axis=-1, keepdims=True))
            p_w = jnp.exp(s_w - m)
            p_c = jnp.exp(s_c - m)
            denom = jnp.sum(p_w, axis=-1, keepdims=True) + jnp.sum(p_c, axis=-1, keepdims=True)
            o = _dot(p_w.astype(jnp.bfloat16), vw) + _dot(p_c.astype(jnp.bfloat16), vc)
            o = jnp.where(own_head, o / denom, 0.0)
            og = o[0:GRID_W]
            for h in range(1, HEAD_GROUP):
                og = og + o[h * GRID_W:(h + 1) * GRID_W]
            mix_ref[qrows, D_A + g * GROUP_W:D_A + (g + 1) * GROUP_W] = (
                og * gb_ref[0, qrows, lanes].astype(jnp.float32)).astype(mix_ref.dtype)

    out_proj(Q_ROWS - OUT_ROWS)


def _attn_call(x, gate, oa, gb, q, k, v, kc, vc, tab, w_out):
    B, L, _ = x.shape
    C = kc.shape[1]
    tok_spec = lambda width: pl.BlockSpec((1, TQ, width), lambda b, i: (b, i, 0))
    batch_spec = lambda n: pl.BlockSpec((1, n, D_B), lambda b, i: (b, 0, 0))
    return pl.pallas_call(
        _attn_kernel,
        grid=(B, L // TQ),
        in_specs=[
            tok_spec(D_MODEL),
            pl.BlockSpec((1, 1, D_MODEL), lambda b, i: (b, 0, 0)),
            tok_spec(D_A),
            tok_spec(D_B),
            tok_spec(D_B),
            batch_spec(L),
            batch_spec(L),
            batch_spec(C),
            batch_spec(C),
            pl.BlockSpec(tab.shape, lambda b, i: (0, 0, 0, 0)),
            pl.BlockSpec((D_A + D_B, D_MODEL), lambda b, i: (0, 0)),
        ],
        out_specs=tok_spec(D_MODEL),
        out_shape=jax.ShapeDtypeStruct((B, L, D_MODEL), jnp.float32),
        scratch_shapes=[pltpu.VMEM((TQ, D_A + D_B), jnp.bfloat16)],
        compiler_params=pltpu.CompilerParams(
            dimension_semantics=("arbitrary", "arbitrary"), vmem_limit_bytes=VMEM_LIMIT),
        name="nbr_attn_out",
    )(x, gate, oa, gb, q, k, v, kc, vc, tab, w_out)


def kernel(x, c, ctx, c_ctx, w_ada, b_ada, norm_g, w_in, sgu_norm_g, w_spatial, b_spatial,
           q_norm_g, k_norm_g, rpb, w_out):
    B, L, D = x.shape
    depth = w_ada.shape[0]
    f32, bf16 = jnp.float32, jnp.bfloat16
    bd = jnp.asarray(np.kron(np.eye(MXU_DIM // HEAD_DIM), np.ones((HEAD_DIM, HEAD_DIM))), bf16)
    cc = jnp.zeros((ADA_ROWS, D), f32).at[:B].set(c).at[B].set(c_ctx)
    assert depth == 1
    for layer in range(depth):
        mod = _ada_call(cc, w_ada[layer], b_ada[layer][None, :])
        shift, scale, gate = (mod[:B, j * D:(j + 1) * D].reshape(B, 1, D) for j in range(3))
        cshift, cscale = (mod[B:B + 1, j * D:(j + 1) * D] for j in range(2))

        ng = norm_g[layer][None, :]
        w_in_b = w_in[layer].astype(bf16)
        kg = jnp.tile(k_norm_g[layer], HEADS)[None, :]
        qg = jnp.tile(q_norm_g[layer], HEADS)[None, :] * (HEAD_DIM ** -0.5)
        bs = jnp.repeat(b_spatial[layer].T, D_A // SGU_GROUPS, axis=1)
        oa, gb, q, k, v = _proj_call(
            x, shift, scale, ng, w_in_b, sgu_norm_g[layer][None, :],
            w_spatial[layer].astype(bf16), bs, qg, kg, bd)
        kc, vc = _ctx_call(ctx, cshift, cscale, ng, w_in_b, kg, bd)
        tab = _pair_table(_rpb_call(rpb[layer]))
        x = _attn_call(x, gate, oa, gb, q, k, v, kc, vc, tab, w_out[layer].astype(bf16))
    return x
```

```python
import numpy as np
import jax
import jax.numpy as jnp
from jax import lax
from jax.experimental import pallas as pl
from jax.experimental.pallas import tpu as pltpu

D_MODEL = 1024
GRID_W = 64
D_A = 512
D_B = 512
CHUNK = 128
SGU_GROUPS = 4
HEAD_DIM = 64
HEADS = 8
HEAD_GROUP = 4
GROUP_W = HEAD_GROUP * HEAD_DIM
WIN_R = 8
WIN_C = 16
D_IN = 3 * D_A + 4 * D_B
EPS = 1e-6
NEG_INF = -1e30
LOG2E = float(np.log2(np.e))

LANES = 128
MXU_DIM = 256
VMEM_LIMIT = 56 * 1024 * 1024

TM = 512
Q_ROWS = 8
OUT_ROWS = 4
N_SLOTS = 2
SOFTMAX_ROWS = 32
TQ = Q_ROWS * GRID_W
TKW = WIN_R * GRID_W
ADA_ROWS = 16
ADA_BN = 768

_NT = (((1,), (1,)), ((), ()))


def _silu(x):
    return x / (1.0 + jnp.exp(-x))


def _gelu(x):
    return 0.5 * x * (1.0 + lax.erf(x * np.float32(np.sqrt(0.5))))


def _dot(a, b):
    return jnp.dot(a, b, preferred_element_type=jnp.float32)


def _ada_kernel(c_ref, w_ref, b_ref, o_ref):
    a = _silu(c_ref[...])
    o_ref[...] = jnp.dot(a, w_ref[...], preferred_element_type=jnp.float32,
                         precision=lax.Precision.HIGHEST) + b_ref[...]


def _ada_call(cc, w_ada, b_ada):
    n = w_ada.shape[1]
    return pl.pallas_call(
        _ada_kernel,
        grid=(n // ADA_BN,),
        in_specs=[
            pl.BlockSpec((ADA_ROWS, D_MODEL), lambda j: (0, 0)),
            pl.BlockSpec((D_MODEL, ADA_BN), lambda j: (0, j)),
            pl.BlockSpec((1, ADA_BN), lambda j: (0, j)),
        ],
        out_specs=pl.BlockSpec((ADA_ROWS, ADA_BN), lambda j: (0, j)),
        out_shape=jax.ShapeDtypeStruct((ADA_ROWS, n), jnp.float32),
        compiler_params=pltpu.CompilerParams(dimension_semantics=("arbitrary",)),
        name="ada_params",
    )(cc, w_ada, b_ada)


def _prenorm(x, g, shift, scale):
    ms = jnp.mean(x * x, axis=-1, keepdims=True)
    h = x * lax.rsqrt(ms + EPS) * g
    return (h * (1.0 + scale) + shift).astype(jnp.bfloat16)


def _head_rms(z, bd, gain):
    sq = (z * z).astype(jnp.bfloat16)
    ss = jnp.concatenate(
        [_dot(sq[:, c * MXU_DIM:(c + 1) * MXU_DIM], bd) for c in range(D_B // MXU_DIM)], axis=-1)
    return z * lax.rsqrt(ss * (1.0 / HEAD_DIM) + EPS) * gain


def _proj_kernel(x_ref, shift_ref, scale_ref, ng_ref, w_ref, sg_ref, ws_ref, bs_ref,
                 qg_ref, kg_ref, bd_ref,
                 oa_ref, gb_ref, q_ref, k_ref, v_ref):
    hb = _prenorm(x_ref[0], ng_ref[...], shift_ref[0], scale_ref[0])

    def zcols(lo, width):
        return _dot(hb, w_ref[:, lo:lo + width])

    gu = _gelu(zcols(0, D_A))
    gv = _gelu(zcols(D_A, D_A))
    sa = _silu(zcols(2 * D_A, D_A))
    for g in range(SGU_GROUPS):
        cs = slice(g * LANES, (g + 1) * LANES)
        vg = gv[:, cs]
        ms = jnp.mean(vg * vg, axis=-1, keepdims=True)
        vn = (vg * lax.rsqrt(ms + EPS) * sg_ref[:, cs]).astype(jnp.bfloat16)
        chunks = [slice(c * CHUNK, (c + 1) * CHUNK) for c in range(TM // CHUNK)]
        mixed = _dot(ws_ref[g], jnp.concatenate([vn[rs] for rs in chunks], axis=1))
        for c, rs in enumerate(chunks):
            mixed_c = mixed[:, c * CHUNK:(c + 1) * CHUNK] + bs_ref[:, cs]
            oa_ref[0, rs, cs] = (gu[rs, cs] * mixed_c * sa[rs, cs]).astype(oa_ref.dtype)

    bd = bd_ref[...]
    q_ref[0] = _head_rms(zcols(3 * D_A, D_B), bd, qg_ref[...]).astype(q_ref.dtype)
    k_ref[0] = _head_rms(zcols(3 * D_A + D_B, D_B), bd, kg_ref[...]).astype(k_ref.dtype)
    v_ref[0] = zcols(3 * D_A + 2 * D_B, D_B).astype(v_ref.dtype)
    gb_ref[0] = _silu(zcols(3 * D_A + 3 * D_B, D_B)).astype(gb_ref.dtype)


def _proj_call(x, shift, scale, ng, w_in, sg, ws, bs, qg, kg, bd):
    B, L, _ = x.shape
    const2 = lambda b, i: (0, 0)
    half_spec = pl.BlockSpec((1, TM, D_A), lambda b, i: (b, i, 0))
    half_shape = jax.ShapeDtypeStruct((B, L, D_A), jnp.bfloat16)
    return pl.pallas_call(
        _proj_kernel,
        grid=(B, L // TM),
        in_specs=[
            pl.BlockSpec((1, TM, D_MODEL), lambda b, i: (b, i, 0)),
            pl.BlockSpec((1, 1, D_MODEL), lambda b, i: (b, 0, 0)),
            pl.BlockSpec((1, 1, D_MODEL), lambda b, i: (b, 0, 0)),
            pl.BlockSpec((1, D_MODEL), const2),
            pl.BlockSpec((D_MODEL, D_IN), const2),
            pl.BlockSpec((1, D_A), const2),
            pl.BlockSpec((SGU_GROUPS, CHUNK, CHUNK), lambda b, i: (0, 0, 0)),
            pl.BlockSpec((CHUNK, D_A), const2),
            pl.BlockSpec((1, D_B), const2),
            pl.BlockSpec((1, D_B), const2),
            pl.BlockSpec((MXU_DIM, MXU_DIM), const2),
        ],
        out_specs=[half_spec] * 5,
        out_shape=[half_shape] * 5,
        compiler_params=pltpu.CompilerParams(
            dimension_semantics=("arbitrary", "arbitrary"), vmem_limit_bytes=VMEM_LIMIT),
        name="latent_proj",
    )(x, shift, scale, ng, w_in, sg, ws, bs, qg, kg, bd)


def _ctx_kernel(x_ref, shift_ref, scale_ref, ng_ref, w_ref, kg_ref, bd_ref, k_ref, v_ref):
    hb = _prenorm(x_ref[0], ng_ref[...], shift_ref[...], scale_ref[...])
    k_ref[0] = _head_rms(_dot(hb, w_ref[:, 0:D_B]), bd_ref[...], kg_ref[...]).astype(k_ref.dtype)
    v_ref[0] = _dot(hb, w_ref[:, D_B:2 * D_B]).astype(v_ref.dtype)


def _ctx_call(ctx, cshift, cscale, ng, w_in, kg, bd):
    B, C, _ = ctx.shape
    const2 = lambda b: (0, 0)
    kv_block = (3 * D_A + D_B) // (2 * D_B)
    kv_spec = pl.BlockSpec((1, C, D_B), lambda b: (b, 0, 0))
    kv_shape = jax.ShapeDtypeStruct((B, C, D_B), jnp.bfloat16)
    return pl.pallas_call(
        _ctx_kernel,
        grid=(B,),
        in_specs=[
            pl.BlockSpec((1, C, D_MODEL), lambda b: (b, 0, 0)),
            pl.BlockSpec((1, D_MODEL), const2),
            pl.BlockSpec((1, D_MODEL), const2),
            pl.BlockSpec((1, D_MODEL), const2),
            pl.BlockSpec((D_MODEL, 2 * D_B), lambda b: (0, kv_block)),
            pl.BlockSpec((1, D_B), const2),
            pl.BlockSpec((MXU_DIM, MXU_DIM), const2),
        ],
        out_specs=[kv_spec, kv_spec],
        out_shape=[kv_shape, kv_shape],
        compiler_params=pltpu.CompilerParams(
            dimension_semantics=("arbitrary",), vmem_limit_bytes=VMEM_LIMIT),
        name="ctx_kv",
    )(ctx, cshift, cscale, ng, w_in, kg, bd)


def _rpb_kernel(rpb_ref, dc_ref, o_ref):
    dc_map = dc_ref[...]
    acc = jnp.full(o_ref.shape, NEG_INF, jnp.float32)
    for dc in range(2 * WIN_C - 1):
        acc = jnp.where(dc_map == dc, rpb_ref[:, dc:dc + 1] * LOG2E, acc)
    o_ref[...] = acc


def _rpb_call(rpb):
    H, n_dr, n_dc = rpb.shape
    cols = np.arange(GRID_W)
    c0 = np.clip(cols - WIN_C // 2, 0, GRID_W - WIN_C)
    in_win = (cols[None, :] >= c0[:, None]) & (cols[None, :] < c0[:, None] + WIN_C)
    dc = np.clip(cols[None, :] - cols[:, None] + (WIN_C - 1), 0, 2 * WIN_C - 2)
    dc_map = jnp.asarray(np.where(in_win, dc, -1).reshape(1, GRID_W * GRID_W), jnp.int32)
    rpb2 = jnp.pad(rpb.astype(jnp.float32).reshape(H * n_dr, n_dc), ((0, 0), (0, LANES - n_dc)))
    bn = 4 * LANES
    return pl.pallas_call(
        _rpb_kernel,
        grid=(GRID_W * GRID_W // bn,),
        in_specs=[pl.BlockSpec((H * n_dr, LANES), lambda j: (0, 0)),
                  pl.BlockSpec((1, bn), lambda j: (0, j))],
        out_specs=pl.BlockSpec((H * n_dr, bn), lambda j: (0, j)),
        out_shape=jax.ShapeDtypeStruct((H * n_dr, GRID_W * GRID_W), jnp.float32),
        compiler_params=pltpu.CompilerParams(dimension_semantics=("arbitrary",)),
        name="rpb_toeplitz",
    )(rpb2, dc_map).reshape(H, n_dr, GRID_W, GRID_W)


def _pair_table(tcol):
    return jnp.concatenate([tcol[:, :-1], tcol[:, 1:]], axis=-1)


def _attn_kernel(x_ref, gate_ref, oa_ref, gb_ref, q_ref, k_ref, v_ref, kc_ref, vc_ref,
                 tab_ref, wout_ref, o_ref, mix_ref, s_ref, p_ref, l_ref):
    i = pl.program_id(1)
    rows = pl.num_programs(1) * Q_ROWS
    shape = (GROUP_W, GROUP_W)
    row_head = lax.broadcasted_iota(jnp.int32, shape, 0) // HEAD_DIM
    lane_head = lax.broadcasted_iota(jnp.int32, shape, 1) // HEAD_DIM
    own_head = row_head == lane_head

    n_ctx = kc_ref.shape[1]
    groups = range(HEADS // HEAD_GROUP)

    def window(rho):
        r = i * Q_ROWS + rho
        r0 = jnp.clip(r - WIN_R // 2, 0, rows - WIN_R)
        d0 = r0 - r + (WIN_R - 1)
        return pl.multiple_of(r0 * GRID_W, GRID_W), d0

    def qrows(rho):
        if isinstance(rho, int):
            return slice(rho * GRID_W, (rho + 1) * GRID_W)
        return pl.ds(pl.multiple_of(rho * GRID_W, GRID_W), GRID_W)

    def scores(rho, slot):
        tok0, d0 = window(rho)
        for g in groups:
            lanes = slice(g * GROUP_W, (g + 1) * GROUP_W)
            qr = q_ref[0, qrows(rho), lanes]
            q_bd = jnp.where(own_head, jnp.concatenate([qr] * HEAD_GROUP, axis=0),
                             jnp.zeros(shape, qr.dtype))
            bias = jnp.concatenate(
                [jnp.concatenate([tab_ref[g * HEAD_GROUP + h, d0 + 2 * t] for t in range(WIN_R // 2)],
                                 axis=1) for h in range(HEAD_GROUP)], axis=0)
            s_ref[slot, g, :, 0:TKW] = lax.dot_general(
                q_bd, k_ref[0, pl.ds(tok0, TKW), lanes], _NT,
                preferred_element_type=jnp.float32) + bias
            s_ref[slot, g, :, TKW:TKW + n_ctx] = lax.dot_general(
                q_bd, kc_ref[0, :, lanes], _NT, preferred_element_type=jnp.float32)

    def softmax(slot):
        for g in groups:
            for c in range(GROUP_W // SOFTMAX_ROWS):
                rs = slice(c * SOFTMAX_ROWS, (c + 1) * SOFTMAX_ROWS)
                s = s_ref[slot, g, rs, :]
                p = jnp.exp2(s - jnp.max(s, axis=-1, keepdims=True))
                l_ref[slot, g, rs, :] = jnp.sum(p, axis=-1, keepdims=True)
                p_ref[slot, g, rs, :] = p.astype(p_ref.dtype)

    def values(rho, slot):
        tok0, _ = window(rho)
        for g in groups:
            lanes = slice(g * GROUP_W, (g + 1) * GROUP_W)
            o = (_dot(p_ref[slot, g, :, 0:TKW], v_ref[0, pl.ds(tok0, TKW), lanes])
                 + _dot(p_ref[slot, g, :, TKW:TKW + n_ctx], vc_ref[0, :, lanes]))
            o = jnp.where(own_head, o / l_ref[slot, g], 0.0)
            og = o[0:GRID_W]
            for h in range(1, HEAD_GROUP):
                og = og + o[h * GRID_W:(h + 1) * GRID_W]
            mix_ref[qrows(rho), D_A + g * GROUP_W:D_A + (g + 1) * GROUP_W] = (
                og * gb_ref[0, qrows(rho), lanes].astype(jnp.float32)).astype(mix_ref.dtype)

    mix_ref[:, 0:D_A] = oa_ref[0]
    scores(0, 0)
    for rho in range(Q_ROWS):
        if rho + 1 < Q_ROWS:
            scores(rho + 1, (rho + 1) % N_SLOTS)
        softmax(rho % N_SLOTS)
        values(rho, rho % N_SLOTS)
        if (rho + 1) % OUT_ROWS == 0:
            ts = slice((rho + 1 - OUT_ROWS) * GRID_W, (rho + 1) * GRID_W)
            o_ref[0, ts] = x_ref[0, ts] + gate_ref[0] * _dot(mix_ref[ts], wout_ref[...])


def _attn_call(x, gate, oa, gb, q, k, v, kc, vc, tab, w_out):
    B, L, _ = x.shape
    C = kc.shape[1]
    n_groups = HEADS // HEAD_GROUP
    tok_spec = lambda width: pl.BlockSpec((1, TQ, width), lambda b, i: (b, i, 0))
    batch_spec = lambda n: pl.BlockSpec((1, n, D_B), lambda b, i: (b, 0, 0))
    return pl.pallas_call(
        _attn_kernel,
        grid=(B, L // TQ),
        in_specs=[
            tok_spec(D_MODEL),
            pl.BlockSpec((1, 1, D_MODEL), lambda b, i: (b, 0, 0)),
            tok_spec(D_A),
            tok_spec(D_B),
            tok_spec(D_B),
            batch_spec(L),
            batch_spec(L),
            batch_spec(C),
            batch_spec(C),
            pl.BlockSpec(tab.shape, lambda b, i: (0, 0, 0, 0)),
            pl.BlockSpec((D_A + D_B, D_MODEL), lambda b, i: (0, 0)),
        ],
        out_specs=tok_spec(D_MODEL),
        out_shape=jax.ShapeDtypeStruct((B, L, D_MODEL), jnp.float32),
        scratch_shapes=[pltpu.VMEM((TQ, D_A + D_B), jnp.bfloat16),
                        pltpu.VMEM((N_SLOTS, n_groups, GROUP_W, TKW + C), jnp.float32),
                        pltpu.VMEM((N_SLOTS, n_groups, GROUP_W, TKW + C), jnp.bfloat16),
                        pltpu.VMEM((N_SLOTS, n_groups, GROUP_W, 1), jnp.float32)],
        compiler_params=pltpu.CompilerParams(
            dimension_semantics=("arbitrary", "arbitrary"), vmem_limit_bytes=VMEM_LIMIT),
        name="nbr_attn_out",
    )(x, gate, oa, gb, q, k, v, kc, vc, tab, w_out)


def kernel(x, c, ctx, c_ctx, w_ada, b_ada, norm_g, w_in, sgu_norm_g, w_spatial, b_spatial,
           q_norm_g, k_norm_g, rpb, w_out):
    B, L, D = x.shape
    depth = w_ada.shape[0]
    f32, bf16 = jnp.float32, jnp.bfloat16
    bd = jnp.asarray(np.kron(np.eye(MXU_DIM // HEAD_DIM), np.ones((HEAD_DIM, HEAD_DIM))), bf16)
    cc = jnp.zeros((ADA_ROWS, D), f32).at[:B].set(c).at[B].set(c_ctx)
    assert depth == 1
    for layer in range(depth):
        mod = _ada_call(cc, w_ada[layer], b_ada[layer][None, :])
        shift, scale, gate = (mod[:B, j * D:(j + 1) * D].reshape(B, 1, D) for j in range(3))
        cshift, cscale = (mod[B:B + 1, j * D:(j + 1) * D] for j in range(2))

        ng = norm_g[layer][None, :]
        w_in_b = w_in[layer].astype(bf16)
        kg = jnp.tile(k_norm_g[layer], HEADS)[None, :]
        qg = jnp.tile(q_norm_g[layer], HEADS)[None, :] * (HEAD_DIM ** -0.5 * LOG2E)
        bs = jnp.repeat(b_spatial[layer].T, D_A // SGU_GROUPS, axis=1)
        oa, gb, q, k, v = _proj_call(
            x, shift, scale, ng, w_in_b, sgu_norm_g[layer][None, :],
            w_spatial[layer].astype(bf16), bs, qg, kg, bd)
        kc, vc = _ctx_call(ctx, cshift, cscale, ng, w_in_b, kg, bd)
        tab = _pair_table(_rpb_call(rpb[layer]))
        x = _attn_call(x, gate, oa, gb, q, k, v, kc, vc, tab, w_out[layer].astype(bf16))
    return x
```

```python
import numpy as np
import jax
import jax.numpy as jnp
from jax import lax
from jax.experimental import pallas as pl
from jax.experimental.pallas import tpu as pltpu

D_MODEL = 1024
GRID_W = 64
D_A = 512
D_B = 512
CHUNK = 128
SGU_GROUPS = 4
HEAD_DIM = 64
HEADS = 8
HEAD_GROUP = 4
GROUP_W = HEAD_GROUP * HEAD_DIM
WIN_R = 8
WIN_C = 16
D_IN = 3 * D_A + 4 * D_B
EPS = 1e-6
NEG_INF = -1e30
LOG2E = float(np.log2(np.e))

LANES = 128
MXU_DIM = 256
VMEM_LIMIT = 56 * 1024 * 1024

TM = 512
Q_ROWS = 8
OUT_ROWS = 4
N_SLOTS = 2
SOFTMAX_ROWS = 32
TQ = Q_ROWS * GRID_W
TKW = WIN_R * GRID_W
ADA_ROWS = 16
ADA_BN = 768

_NT = (((1,), (1,)), ((), ()))


def _silu(x):
    return x / (1.0 + jnp.exp(-x))


def _gelu(x):
    return 0.5 * x * (1.0 + lax.erf(x * np.float32(np.sqrt(0.5))))


def _dot(a, b):
    return jnp.dot(a, b, preferred_element_type=jnp.float32)


def _ada_kernel(c_ref, w_ref, b_ref, o_ref):
    a, w = _silu(c_ref[...]), w_ref[...]
    a_hi, w_hi = a.astype(jnp.bfloat16), w.astype(jnp.bfloat16)
    a_lo = (a - a_hi.astype(jnp.float32)).astype(jnp.bfloat16)
    w_lo = (w - w_hi.astype(jnp.float32)).astype(jnp.bfloat16)
    o_ref[...] = _dot(a_hi, w_hi) + (_dot(a_lo, w_hi) + _dot(a_hi, w_lo)) + b_ref[...]


def _ada_call(cc, w_ada, b_ada):
    n = w_ada.shape[1]
    return pl.pallas_call(
        _ada_kernel,
        grid=(n // ADA_BN,),
        in_specs=[
            pl.BlockSpec((ADA_ROWS, D_MODEL), lambda j: (0, 0)),
            pl.BlockSpec((D_MODEL, ADA_BN), lambda j: (0, j)),
            pl.BlockSpec((1, ADA_BN), lambda j: (0, j)),
        ],
        out_specs=pl.BlockSpec((ADA_ROWS, ADA_BN), lambda j: (0, j)),
        out_shape=jax.ShapeDtypeStruct((ADA_ROWS, n), jnp.float32),
        compiler_params=pltpu.CompilerParams(dimension_semantics=("arbitrary",)),
        name="ada_params",
    )(cc, w_ada, b_ada)


def _prenorm(x, g, shift, scale):
    ms = jnp.mean(x * x, axis=-1, keepdims=True)
    h = x * lax.rsqrt(ms + EPS) * g
    return (h * (1.0 + scale) + shift).astype(jnp.bfloat16)


def _head_rms(z, bd, gain):
    sq = (z * z).astype(jnp.bfloat16)
    ss = jnp.concatenate(
        [_dot(sq[:, c * MXU_DIM:(c + 1) * MXU_DIM], bd) for c in range(D_B // MXU_DIM)], axis=-1)
    return z * lax.rsqrt(ss * (1.0 / HEAD_DIM) + EPS) * gain


def _proj_kernel(x_ref, shift_ref, scale_ref, ng_ref, w_ref, sg_ref, ws_ref, bs_ref,
                 qg_ref, kg_ref, bd_ref,
                 oa_ref, gb_ref, q_ref, k_ref, v_ref, wb_ref):
    @pl.when((pl.program_id(0) == 0) & (pl.program_id(1) == 0))
    def _():
        for lo in range(0, D_IN, D_A):
            wb_ref[:, lo:lo + D_A] = w_ref[:, lo:lo + D_A].astype(wb_ref.dtype)

    hb = _prenorm(x_ref[0], ng_ref[...], shift_ref[0], scale_ref[0])

    def zcols(lo, width):
        return _dot(hb, wb_ref[:, lo:lo + width])

    gu = _gelu(zcols(0, D_A))
    gv = _gelu(zcols(D_A, D_A))
    sa = _silu(zcols(2 * D_A, D_A))
    for g in range(SGU_GROUPS):
        cs = slice(g * LANES, (g + 1) * LANES)
        vg = gv[:, cs]
        ms = jnp.mean(vg * vg, axis=-1, keepdims=True)
        vn = (vg * lax.rsqrt(ms + EPS) * sg_ref[:, cs]).astype(jnp.bfloat16)
        chunks = [slice(c * CHUNK, (c + 1) * CHUNK) for c in range(TM // CHUNK)]
        mixed = _dot(ws_ref[g], jnp.concatenate([vn[rs] for rs in chunks], axis=1))
        for c, rs in enumerate(chunks):
            mixed_c = mixed[:, c * CHUNK:(c + 1) * CHUNK] + bs_ref[:, cs]
            oa_ref[0, rs, cs] = (gu[rs, cs] * mixed_c * sa[rs, cs]).astype(oa_ref.dtype)

    bd = bd_ref[...]
    q_ref[0] = _head_rms(zcols(3 * D_A, D_B), bd, qg_ref[...]).astype(q_ref.dtype)
    k_ref[0] = _head_rms(zcols(3 * D_A + D_B, D_B), bd, kg_ref[...]).astype(k_ref.dtype)
    v_ref[0] = zcols(3 * D_A + 2 * D_B, D_B).astype(v_ref.dtype)
    gb_ref[0] = _silu(zcols(3 * D_A + 3 * D_B, D_B)).astype(gb_ref.dtype)


def _proj_call(x, shift, scale, ng, w_in, sg, ws, bs, qg, kg, bd):
    B, L, _ = x.shape
    const2 = lambda b, i: (0, 0)
    half_spec = pl.BlockSpec((1, TM, D_A), lambda b, i: (b, i, 0))
    half_shape = jax.ShapeDtypeStruct((B, L, D_A), jnp.bfloat16)
    return pl.pallas_call(
        _proj_kernel,
        grid=(B, L // TM),
        in_specs=[
            pl.BlockSpec((1, TM, D_MODEL), lambda b, i: (b, i, 0)),
            pl.BlockSpec((1, 1, D_MODEL), lambda b, i: (b, 0, 0)),
            pl.BlockSpec((1, 1, D_MODEL), lambda b, i: (b, 0, 0)),
            pl.BlockSpec((1, D_MODEL), const2),
            pl.BlockSpec((D_MODEL, D_IN), const2),
            pl.BlockSpec((1, D_A), const2),
            pl.BlockSpec((SGU_GROUPS, CHUNK, CHUNK), lambda b, i: (0, 0, 0)),
            pl.BlockSpec((CHUNK, D_A), const2),
            pl.BlockSpec((1, D_B), const2),
            pl.BlockSpec((1, D_B), const2),
            pl.BlockSpec((MXU_DIM, MXU_DIM), const2),
        ],
        out_specs=[half_spec] * 5,
        out_shape=[half_shape] * 5,
        scratch_shapes=[pltpu.VMEM((D_MODEL, D_IN), jnp.bfloat16)],
        compiler_params=pltpu.CompilerParams(
            dimension_semantics=("arbitrary", "arbitrary"), vmem_limit_bytes=VMEM_LIMIT),
        name="latent_proj",
    )(x, shift, scale, ng, w_in, sg, ws, bs, qg, kg, bd)


def _ctx_kernel(x_ref, shift_ref, scale_ref, ng_ref, w_ref, kg_ref, bd_ref, k_ref, v_ref):
    hb = _prenorm(x_ref[0], ng_ref[...], shift_ref[...], scale_ref[...])
    wk = w_ref[:, 0:D_B].astype(jnp.bfloat16)
    wv = w_ref[:, D_B:2 * D_B].astype(jnp.bfloat16)
    k_ref[0] = _head_rms(_dot(hb, wk), bd_ref[...], kg_ref[...]).astype(k_ref.dtype)
    v_ref[0] = _dot(hb, wv).astype(v_ref.dtype)


def _ctx_call(ctx, cshift, cscale, ng, w_in, kg, bd):
    B, C, _ = ctx.shape
    const2 = lambda b: (0, 0)
    kv_block = (3 * D_A + D_B) // (2 * D_B)
    kv_spec = pl.BlockSpec((1, C, D_B), lambda b: (b, 0, 0))
    kv_shape = jax.ShapeDtypeStruct((B, C, D_B), jnp.bfloat16)
    return pl.pallas_call(
        _ctx_kernel,
        grid=(B,),
        in_specs=[
            pl.BlockSpec((1, C, D_MODEL), lambda b: (b, 0, 0)),
            pl.BlockSpec((1, D_MODEL), const2),
            pl.BlockSpec((1, D_MODEL), const2),
            pl.BlockSpec((1, D_MODEL), const2),
            pl.BlockSpec((D_MODEL, 2 * D_B), lambda b: (0, kv_block)),
            pl.BlockSpec((1, D_B), const2),
            pl.BlockSpec((MXU_DIM, MXU_DIM), const2),
        ],
        out_specs=[kv_spec, kv_spec],
        out_shape=[kv_shape, kv_shape],
        compiler_params=pltpu.CompilerParams(
            dimension_semantics=("arbitrary",), vmem_limit_bytes=VMEM_LIMIT),
        name="ctx_kv",
    )(ctx, cshift, cscale, ng, w_in, kg, bd)


def _rpb_kernel(rpb_ref, win_ref, o_ref):
    n_heads, n_pairs = o_ref.shape[:2]
    n_dr = n_pairs + 1
    blk = o_ref.shape[2:]
    in_win = win_ref[...] != 0
    left_half = lax.broadcasted_iota(jnp.int32, blk, 1) < GRID_W
    first = LANES - (WIN_C - 1)
    for h in range(n_heads):
        for dr in range(n_pairs):
            row = h * n_dr + dr
            lo = pltpu.roll(jnp.broadcast_to(rpb_ref[row:row + 1, :], blk), first, 1,
                            stride=1, stride_axis=0)
            hi = pltpu.roll(jnp.broadcast_to(rpb_ref[row + 1:row + 2, :], blk),
                            (first + GRID_W) % LANES, 1, stride=1, stride_axis=0)
            o_ref[h, dr] = jnp.where(in_win, jnp.where(left_half, lo, hi) * LOG2E, NEG_INF)


def _rpb_call(rpb):
    H, n_dr, n_dc = rpb.shape
    cols = np.arange(GRID_W)
    c0 = np.clip(cols - WIN_C // 2, 0, GRID_W - WIN_C)
    in_win = (cols[None, :] >= c0[:, None]) & (cols[None, :] < c0[:, None] + WIN_C)
    win = jnp.asarray(np.tile(in_win, (1, LANES // GRID_W)), jnp.int32)
    rpb2 = jnp.pad(rpb.astype(jnp.float32).reshape(H * n_dr, n_dc), ((0, 0), (0, LANES - n_dc)))
    return pl.pallas_call(
        _rpb_kernel,
        out_shape=jax.ShapeDtypeStruct((H, n_dr - 1, GRID_W, LANES), jnp.float32),
        name="rpb_toeplitz",
    )(rpb2, win)


def _attn_kernel(x_ref, gate_ref, oa_ref, gb_ref, q_ref, k_ref, v_ref, kc_ref, vc_ref,
                 tab_ref, wout_ref, o_ref, mix_ref, s_ref, p_ref, l_ref, wob_ref):
    i = pl.program_id(1)

    @pl.when((pl.program_id(0) == 0) & (i == 0))
    def _():
        wob_ref[...] = wout_ref[...].astype(wob_ref.dtype)

    rows = pl.num_programs(1) * Q_ROWS
    shape = (GROUP_W, GROUP_W)
    row_head = lax.broadcasted_iota(jnp.int32, shape, 0) // HEAD_DIM
    lane_head = lax.broadcasted_iota(jnp.int32, shape, 1) // HEAD_DIM
    own_head = row_head == lane_head

    n_ctx = kc_ref.shape[1]
    groups = range(HEADS // HEAD_GROUP)

    def window(rho):
        r = i * Q_ROWS + rho
        r0 = jnp.clip(r - WIN_R // 2, 0, rows - WIN_R)
        d0 = r0 - r + (WIN_R - 1)
        return pl.multiple_of(r0 * GRID_W, GRID_W), d0

    def qrows(rho):
        if isinstance(rho, int):
            return slice(rho * GRID_W, (rho + 1) * GRID_W)
        return pl.ds(pl.multiple_of(rho * GRID_W, GRID_W), GRID_W)

    def scores(rho, slot):
        tok0, d0 = window(rho)
        for g in groups:
            lanes = slice(g * GROUP_W, (g + 1) * GROUP_W)
            qr = q_ref[0, qrows(rho), lanes]
            q_bd = jnp.where(own_head, jnp.concatenate([qr] * HEAD_GROUP, axis=0),
                             jnp.zeros(shape, qr.dtype))
            bias = jnp.concatenate(
                [jnp.concatenate([tab_ref[g * HEAD_GROUP + h, d0 + 2 * t] for t in range(WIN_R // 2)],
                                 axis=1) for h in range(HEAD_GROUP)], axis=0)
            s_ref[slot, g, :, 0:TKW] = lax.dot_general(
                q_bd, k_ref[0, pl.ds(tok0, TKW), lanes], _NT,
                preferred_element_type=jnp.float32) + bias
            s_ref[slot, g, :, TKW:TKW + n_ctx] = lax.dot_general(
                q_bd, kc_ref[0, :, lanes], _NT, preferred_element_type=jnp.float32)

    def softmax(slot):
        for g in groups:
            for c in range(GROUP_W // SOFTMAX_ROWS):
                rs = slice(c * SOFTMAX_ROWS, (c + 1) * SOFTMAX_ROWS)
                s = s_ref[slot, g, rs, :]
                p = jnp.exp2(s - jnp.max(s, axis=-1, keepdims=True))
                l_ref[slot, g, rs, :] = jnp.sum(p, axis=-1, keepdims=True)
                p_ref[slot, g, rs, :] = p.astype(p_ref.dtype)

    def values(rho, slot):
        tok0, _ = window(rho)
        for g in groups:
            lanes = slice(g * GROUP_W, (g + 1) * GROUP_W)
            o = (_dot(p_ref[slot, g, :, 0:TKW], v_ref[0, pl.ds(tok0, TKW), lanes])
                 + _dot(p_ref[slot, g, :, TKW:TKW + n_ctx], vc_ref[0, :, lanes]))
            o = jnp.where(own_head, o / l_ref[slot, g], 0.0)
            og = o[0:GRID_W]
            for h in range(1, HEAD_GROUP):
                og = og + o[h * GRID_W:(h + 1) * GRID_W]
            mix_ref[qrows(rho), D_A + g * GROUP_W:D_A + (g + 1) * GROUP_W] = (
                og * gb_ref[0, qrows(rho), lanes].astype(jnp.float32)).astype(mix_ref.dtype)

    mix_ref[:, 0:D_A] = oa_ref[0]
    scores(0, 0)
    for rho in range(Q_ROWS):
        if rho + 1 < Q_ROWS:
            scores(rho + 1, (rho + 1) % N_SLOTS)
        softmax(rho % N_SLOTS)
        values(rho, rho % N_SLOTS)
        if (rho + 1) % OUT_ROWS == 0:
            ts = slice((rho + 1 - OUT_ROWS) * GRID_W, (rho + 1) * GRID_W)
            o_ref[0, ts] = x_ref[0, ts] + gate_ref[0] * _dot(mix_ref[ts], wob_ref[...])


def _attn_call(x, gate, oa, gb, q, k, v, kc, vc, tab, w_out):
    B, L, _ = x.shape
    C = kc.shape[1]
    n_groups = HEADS // HEAD_GROUP
    tok_spec = lambda width: pl.BlockSpec((1, TQ, width), lambda b, i: (b, i, 0))
    batch_spec = lambda n: pl.BlockSpec((1, n, D_B), lambda b, i: (b, 0, 0))
    return pl.pallas_call(
        _attn_kernel,
        grid=(B, L // TQ),
        in_specs=[
            tok_spec(D_MODEL),
            pl.BlockSpec((1, 1, D_MODEL), lambda b, i: (b, 0, 0)),
            tok_spec(D_A),
            tok_spec(D_B),
            tok_spec(D_B),
            batch_spec(L),
            batch_spec(L),
            batch_spec(C),
            batch_spec(C),
            pl.BlockSpec(tab.shape, lambda b, i: (0, 0, 0, 0)),
            pl.BlockSpec((D_A + D_B, D_MODEL), lambda b, i: (0, 0)),
        ],
        out_specs=tok_spec(D_MODEL),
        out_shape=jax.ShapeDtypeStruct((B, L, D_MODEL), jnp.float32),
        scratch_shapes=[pltpu.VMEM((TQ, D_A + D_B), jnp.bfloat16),
                        pltpu.VMEM((N_SLOTS, n_groups, GROUP_W, TKW + C), jnp.float32),
                        pltpu.VMEM((N_SLOTS, n_groups, GROUP_W, TKW + C), jnp.bfloat16),
                        pltpu.VMEM((N_SLOTS, n_groups, GROUP_W, 1), jnp.float32),
                        pltpu.VMEM((D_A + D_B, D_MODEL), jnp.bfloat16)],
        compiler_params=pltpu.CompilerParams(
            dimension_semantics=("arbitrary", "arbitrary"), vmem_limit_bytes=VMEM_LIMIT),
        name="nbr_attn_out",
    )(x, gate, oa, gb, q, k, v, kc, vc, tab, w_out)


def kernel(x, c, ctx, c_ctx, w_ada, b_ada, norm_g, w_in, sgu_norm_g, w_spatial, b_spatial,
           q_norm_g, k_norm_g, rpb, w_out):
    B, L, D = x.shape
    depth = w_ada.shape[0]
    f32, bf16 = jnp.float32, jnp.bfloat16
    bd = jnp.asarray(np.kron(np.eye(MXU_DIM // HEAD_DIM), np.ones((HEAD_DIM, HEAD_DIM))), bf16)
    cc = jnp.concatenate([c, c_ctx[None, :], jnp.zeros((ADA_ROWS - B - 1, D), f32)], axis=0)
    assert depth == 1
    for layer in range(depth):
        mod = _ada_call(cc, w_ada[layer], b_ada[layer][None, :])
        shift, scale, gate = (mod[:B, j * D:(j + 1) * D].reshape(B, 1, D) for j in range(3))
        cshift, cscale = (mod[B:B + 1, j * D:(j + 1) * D] for j in range(2))

        ng = norm_g[layer][None, :]
        kg = jnp.tile(k_norm_g[layer], HEADS)[None, :]
        qg = jnp.tile(q_norm_g[layer], HEADS)[None, :] * (HEAD_DIM ** -0.5 * LOG2E)
        bs = jnp.repeat(b_spatial[layer].T, D_A // SGU_GROUPS, axis=1)
        oa, gb, q, k, v = _proj_call(
            x, shift, scale, ng, w_in[layer], sgu_norm_g[layer][None, :],
            w_spatial[layer].astype(bf16), bs, qg, kg, bd)
        kc, vc = _ctx_call(ctx, cshift, cscale, ng, w_in[layer], kg, bd)
        tab = _rpb_call(rpb[layer])
        x = _attn_call(x, gate, oa, gb, q, k, v, kc, vc, tab, w_out[layer])
    return x
```

```python
import numpy as np
import jax
import jax.numpy as jnp
from jax import lax
from jax.experimental import pallas as pl
from jax.experimental.pallas import tpu as pltpu

D_MODEL = 1024
GRID_W = 64
D_A = 512
D_B = 512
CHUNK = 128
SGU_GROUPS = 4
HEAD_DIM = 64
HEADS = 8
HEAD_GROUP = 4
GROUP_W = HEAD_GROUP * HEAD_DIM
WIN_R = 8
WIN_C = 16
D_IN = 3 * D_A + 4 * D_B
EPS = 1e-6
NEG_INF = -1e30
LOG2E = float(np.log2(np.e))

LANES = 128
MXU_DIM = 256
VMEM_LIMIT = 56 * 1024 * 1024

TM = 512
Q_ROWS = 8
OUT_ROWS = 4
N_SLOTS = 2
SOFTMAX_ROWS = 32
TQ = Q_ROWS * GRID_W
TKW = WIN_R * GRID_W
ADA_ROWS = 16
ADA_BN = 768

_NT = (((1,), (1,)), ((), ()))


def _silu(x):
    return x / (1.0 + jnp.exp(-x))


def _gelu(x):
    return 0.5 * x * (1.0 + lax.erf(x * np.float32(np.sqrt(0.5))))


def _dot(a, b):
    return jnp.dot(a, b, preferred_element_type=jnp.float32)


def _ada_kernel(c_ref, w_ref, b_ref, o_ref):
    a, w = _silu(c_ref[...]), w_ref[...]
    a_hi, w_hi = a.astype(jnp.bfloat16), w.astype(jnp.bfloat16)
    a_lo = (a - a_hi.astype(jnp.float32)).astype(jnp.bfloat16)
    w_lo = (w - w_hi.astype(jnp.float32)).astype(jnp.bfloat16)
    o_ref[...] = _dot(a_hi, w_hi) + (_dot(a_lo, w_hi) + _dot(a_hi, w_lo)) + b_ref[...]


def _ada_call(cc, w_ada, b_ada):
    n = w_ada.shape[1]
    return pl.pallas_call(
        _ada_kernel,
        grid=(n // ADA_BN,),
        in_specs=[
            pl.BlockSpec((ADA_ROWS, D_MODEL), lambda j: (0, 0)),
            pl.BlockSpec((D_MODEL, ADA_BN), lambda j: (0, j)),
            pl.BlockSpec((1, ADA_BN), lambda j: (0, j)),
        ],
        out_specs=pl.BlockSpec((ADA_ROWS, ADA_BN), lambda j: (0, j)),
        out_shape=jax.ShapeDtypeStruct((ADA_ROWS, n), jnp.float32),
        compiler_params=pltpu.CompilerParams(dimension_semantics=("arbitrary",)),
        name="ada_params",
    )(cc, w_ada, b_ada)


def _prenorm(x, g, shift, scale):
    ms = jnp.mean(x * x, axis=-1, keepdims=True)
    h = x * lax.rsqrt(ms + EPS) * g
    return (h * (1.0 + scale) + shift).astype(jnp.bfloat16)


def _head_rms(z, bd, gain):
    sq = (z * z).astype(jnp.bfloat16)
    ss = jnp.concatenate(
        [_dot(sq[:, c * MXU_DIM:(c + 1) * MXU_DIM], bd) for c in range(D_B // MXU_DIM)], axis=-1)
    return z * lax.rsqrt(ss * (1.0 / HEAD_DIM) + EPS) * gain


def _proj_kernel(x_ref, shift_ref, scale_ref, ng_ref, w_ref, sg_ref, ws_ref, bs_ref,
                 qg_ref, kg_ref, bd_ref,
                 oa_ref, gb_ref, q_ref, k_ref, v_ref, wb_ref):
    @pl.when((pl.program_id(0) == 0) & (pl.program_id(1) == 0))
    def _():
        for lo in range(0, D_IN, D_A):
            wb_ref[:, lo:lo + D_A] = w_ref[:, lo:lo + D_A].astype(wb_ref.dtype)

    hb = _prenorm(x_ref[0], ng_ref[...], shift_ref[0], scale_ref[0])

    def zcols(lo, width):
        return _dot(hb, wb_ref[:, lo:lo + width])

    gu = _gelu(zcols(0, D_A))
    gv = _gelu(zcols(D_A, D_A))
    sa = _silu(zcols(2 * D_A, D_A))
    for g in range(SGU_GROUPS):
        cs = slice(g * LANES, (g + 1) * LANES)
        vg = gv[:, cs]
        ms = jnp.mean(vg * vg, axis=-1, keepdims=True)
        vn = (vg * lax.rsqrt(ms + EPS) * sg_ref[:, cs]).astype(jnp.bfloat16)
        chunks = [slice(c * CHUNK, (c + 1) * CHUNK) for c in range(TM // CHUNK)]
        mixed = _dot(ws_ref[g], jnp.concatenate([vn[rs] for rs in chunks], axis=1))
        for c, rs in enumerate(chunks):
            mixed_c = mixed[:, c * CHUNK:(c + 1) * CHUNK] + bs_ref[:, cs]
            oa_ref[0, rs, cs] = (gu[rs, cs] * mixed_c * sa[rs, cs]).astype(oa_ref.dtype)

    bd = bd_ref[...]
    q_ref[0] = _head_rms(zcols(3 * D_A, D_B), bd, qg_ref[...]).astype(q_ref.dtype)
    k_ref[0] = _head_rms(zcols(3 * D_A + D_B, D_B), bd, kg_ref[...]).astype(k_ref.dtype)
    v_ref[0] = zcols(3 * D_A + 2 * D_B, D_B).astype(v_ref.dtype)
    gb_ref[0] = _silu(zcols(3 * D_A + 3 * D_B, D_B)).astype(gb_ref.dtype)


def _proj_call(x, shift, scale, ng, w_in, sg, ws, bs, qg, kg, bd):
    B, L, _ = x.shape
    const2 = lambda b, i: (0, 0)
    half_spec = pl.BlockSpec((1, TM, D_A), lambda b, i: (b, i, 0))
    half_shape = jax.ShapeDtypeStruct((B, L, D_A), jnp.bfloat16)
    return pl.pallas_call(
        _proj_kernel,
        grid=(B, L // TM),
        in_specs=[
            pl.BlockSpec((1, TM, D_MODEL), lambda b, i: (b, i, 0)),
            pl.BlockSpec((1, 1, D_MODEL), lambda b, i: (b, 0, 0)),
            pl.BlockSpec((1, 1, D_MODEL), lambda b, i: (b, 0, 0)),
            pl.BlockSpec((1, D_MODEL), const2),
            pl.BlockSpec((D_MODEL, D_IN), const2),
            pl.BlockSpec((1, D_A), const2),
            pl.BlockSpec((SGU_GROUPS, CHUNK, CHUNK), lambda b, i: (0, 0, 0)),
            pl.BlockSpec((CHUNK, D_A), const2),
            pl.BlockSpec((1, D_B), const2),
            pl.BlockSpec((1, D_B), const2),
            pl.BlockSpec((MXU_DIM, MXU_DIM), const2),
        ],
        out_specs=[half_spec] * 5,
        out_shape=[half_shape] * 5,
        scratch_shapes=[pltpu.VMEM((D_MODEL, D_IN), jnp.bfloat16)],
        compiler_params=pltpu.CompilerParams(
            dimension_semantics=("arbitrary", "arbitrary"), vmem_limit_bytes=VMEM_LIMIT),
        name="latent_proj",
    )(x, shift, scale, ng, w_in, sg, ws, bs, qg, kg, bd)


def _ctx_kernel(x_ref, shift_ref, scale_ref, ng_ref, w_ref, kg_ref, bd_ref, k_ref, v_ref):
    hb = _prenorm(x_ref[0], ng_ref[...], shift_ref[...], scale_ref[...])
    wk = w_ref[:, 0:D_B].astype(jnp.bfloat16)
    wv = w_ref[:, D_B:2 * D_B].astype(jnp.bfloat16)
    k_ref[0] = _head_rms(_dot(hb, wk), bd_ref[...], kg_ref[...]).astype(k_ref.dtype)
    v_ref[0] = _dot(hb, wv).astype(v_ref.dtype)


def _ctx_call(ctx, cshift, cscale, ng, w_in, kg, bd):
    B, C, _ = ctx.shape
    const2 = lambda b: (0, 0)
    kv_block = (3 * D_A + D_B) // (2 * D_B)
    kv_spec = pl.BlockSpec((1, C, D_B), lambda b: (b, 0, 0))
    kv_shape = jax.ShapeDtypeStruct((B, C, D_B), jnp.bfloat16)
    return pl.pallas_call(
        _ctx_kernel,
        grid=(B,),
        in_specs=[
            pl.BlockSpec((1, C, D_MODEL), lambda b: (b, 0, 0)),
            pl.BlockSpec((1, D_MODEL), const2),
            pl.BlockSpec((1, D_MODEL), const2),
            pl.BlockSpec((1, D_MODEL), const2),
            pl.BlockSpec((D_MODEL, 2 * D_B), lambda b: (0, kv_block)),
            pl.BlockSpec((1, D_B), const2),
            pl.BlockSpec((MXU_DIM, MXU_DIM), const2),
        ],
        out_specs=[kv_spec, kv_spec],
        out_shape=[kv_shape, kv_shape],
        compiler_params=pltpu.CompilerParams(
            dimension_semantics=("arbitrary",), vmem_limit_bytes=VMEM_LIMIT),
        name="ctx_kv",
    )(ctx, cshift, cscale, ng, w_in, kg, bd)


def _rpb_kernel(rpb_ref, win_ref, o_ref):
    n_heads, n_pairs = o_ref.shape[:2]
    n_dr = n_pairs + 1
    blk = o_ref.shape[2:]
    in_win = win_ref[...] != 0
    left_half = lax.broadcasted_iota(jnp.int32, blk, 1) < GRID_W
    first = LANES - (WIN_C - 1)
    for h in range(n_heads):
        for dr in range(n_pairs):
            row = h * n_dr + dr
            lo = pltpu.roll(jnp.broadcast_to(rpb_ref[row:row + 1, :], blk), first, 1,
                            stride=1, stride_axis=0)
            hi = pltpu.roll(jnp.broadcast_to(rpb_ref[row + 1:row + 2, :], blk),
                            (first + GRID_W) % LANES, 1, stride=1, stride_axis=0)
            o_ref[h, dr] = jnp.where(in_win, jnp.where(left_half, lo, hi) * LOG2E, NEG_INF)


def _rpb_call(rpb):
    H, n_dr, n_dc = rpb.shape
    cols = np.arange(GRID_W)
    c0 = np.clip(cols - WIN_C // 2, 0, GRID_W - WIN_C)
    in_win = (cols[None, :] >= c0[:, None]) & (cols[None, :] < c0[:, None] + WIN_C)
    win = jnp.asarray(np.tile(in_win, (1, LANES // GRID_W)), jnp.int32)
    rpb2 = jnp.pad(rpb.astype(jnp.float32).reshape(H * n_dr, n_dc), ((0, 0), (0, LANES - n_dc)))
    return pl.pallas_call(
        _rpb_kernel,
        out_shape=jax.ShapeDtypeStruct((H, n_dr - 1, GRID_W, LANES), jnp.float32),
        name="rpb_toeplitz",
    )(rpb2, win)


def _attn_kernel(x_ref, gate_ref, oa_ref, gb_ref, q_ref, k_ref, v_ref, kc_ref, vc_ref,
                 tab_ref, wout_ref, o_ref, mix_ref, s_ref, p_ref, l_ref, wob_ref):
    i = pl.program_id(1)

    @pl.when((pl.program_id(0) == 0) & (i == 0))
    def _():
        wob_ref[...] = wout_ref[...].astype(wob_ref.dtype)

    rows = pl.num_programs(1) * Q_ROWS
    shape = (GROUP_W, GROUP_W)
    row_head = lax.broadcasted_iota(jnp.int32, shape, 0) // HEAD_DIM
    lane_head = lax.broadcasted_iota(jnp.int32, shape, 1) // HEAD_DIM
    own_head = row_head == lane_head

    n_ctx = kc_ref.shape[1]
    groups = range(HEADS // HEAD_GROUP)

    def window(rho):
        r = i * Q_ROWS + rho
        r0 = jnp.clip(r - WIN_R // 2, 0, rows - WIN_R)
        d0 = r0 - r + (WIN_R - 1)
        return pl.multiple_of(r0 * GRID_W, GRID_W), d0

    def qrows(rho):
        if isinstance(rho, int):
            return slice(rho * GRID_W, (rho + 1) * GRID_W)
        return pl.ds(pl.multiple_of(rho * GRID_W, GRID_W), GRID_W)

    def scores(rho, slot):
        tok0, _ = window(rho)
        for g in groups:
            lanes = slice(g * GROUP_W, (g + 1) * GROUP_W)
            qr = q_ref[0, qrows(rho), lanes]
            q_bd = jnp.where(own_head, jnp.concatenate([qr] * HEAD_GROUP, axis=0),
                             jnp.zeros(shape, qr.dtype))
            s_ref[slot, g, :, 0:TKW] = lax.dot_general(
                q_bd, k_ref[0, pl.ds(tok0, TKW), lanes], _NT, preferred_element_type=jnp.float32)
            s_ref[slot, g, :, TKW:TKW + n_ctx] = lax.dot_general(
                q_bd, kc_ref[0, :, lanes], _NT, preferred_element_type=jnp.float32)

    def softmax(rho, slot):
        _, d0 = window(rho)
        for g in groups:
            for c in range(GROUP_W // SOFTMAX_ROWS):
                rs = slice(c * SOFTMAX_ROWS, (c + 1) * SOFTMAX_ROWS)
                head, qc0 = divmod(c * SOFTMAX_ROWS, GRID_W)
                bias = jnp.concatenate(
                    [tab_ref[g * HEAD_GROUP + head, d0 + 2 * t, qc0:qc0 + SOFTMAX_ROWS, :]
                     for t in range(WIN_R // 2)], axis=1)
                s = jnp.concatenate([s_ref[slot, g, rs, 0:TKW] + bias,
                                     s_ref[slot, g, rs, TKW:TKW + n_ctx]], axis=1)
                p = jnp.exp2((s - jnp.max(s, axis=-1, keepdims=True)).astype(p_ref.dtype))
                l_ref[slot, g, rs, :] = jnp.sum(p.astype(jnp.float32), axis=-1, keepdims=True)
                p_ref[slot, g, rs, :] = p

    def values(rho, slot):
        tok0, _ = window(rho)
        for g in groups:
            lanes = slice(g * GROUP_W, (g + 1) * GROUP_W)
            o = (_dot(p_ref[slot, g, :, 0:TKW], v_ref[0, pl.ds(tok0, TKW), lanes])
                 + _dot(p_ref[slot, g, :, TKW:TKW + n_ctx], vc_ref[0, :, lanes]))
            o = jnp.where(own_head, o / l_ref[slot, g], 0.0)
            og = o[0:GRID_W]
            for h in range(1, HEAD_GROUP):
                og = og + o[h * GRID_W:(h + 1) * GRID_W]
            mix_ref[qrows(rho), D_A + g * GROUP_W:D_A + (g + 1) * GROUP_W] = (
                og * gb_ref[0, qrows(rho), lanes].astype(jnp.float32)).astype(mix_ref.dtype)

    mix_ref[:, 0:D_A] = oa_ref[0]
    scores(0, 0)
    for rho in range(Q_ROWS):
        if rho + 1 < Q_ROWS:
            scores(rho + 1, (rho + 1) % N_SLOTS)
        softmax(rho, rho % N_SLOTS)
        values(rho, rho % N_SLOTS)
        if (rho + 1) % OUT_ROWS == 0:
            ts = slice((rho + 1 - OUT_ROWS) * GRID_W, (rho + 1) * GRID_W)
            o_ref[0, ts] = x_ref[0, ts] + gate_ref[0] * _dot(mix_ref[ts], wob_ref[...])


def _attn_call(x, gate, oa, gb, q, k, v, kc, vc, tab, w_out):
    B, L, _ = x.shape
    C = kc.shape[1]
    n_groups = HEADS // HEAD_GROUP
    tok_spec = lambda width: pl.BlockSpec((1, TQ, width), lambda b, i: (b, i, 0))
    batch_spec = lambda n: pl.BlockSpec((1, n, D_B), lambda b, i: (b, 0, 0))
    return pl.pallas_call(
        _attn_kernel,
        grid=(B, L // TQ),
        in_specs=[
            tok_spec(D_MODEL),
            pl.BlockSpec((1, 1, D_MODEL), lambda b, i: (b, 0, 0)),
            tok_spec(D_A),
            tok_spec(D_B),
            tok_spec(D_B),
            batch_spec(L),
            batch_spec(L),
            batch_spec(C),
            batch_spec(C),
            pl.BlockSpec(tab.shape, lambda b, i: (0, 0, 0, 0)),
            pl.BlockSpec((D_A + D_B, D_MODEL), lambda b, i: (0, 0)),
        ],
        out_specs=tok_spec(D_MODEL),
        out_shape=jax.ShapeDtypeStruct((B, L, D_MODEL), jnp.float32),
        scratch_shapes=[pltpu.VMEM((TQ, D_A + D_B), jnp.bfloat16),
                        pltpu.VMEM((N_SLOTS, n_groups, GROUP_W, TKW + C), jnp.float32),
                        pltpu.VMEM((N_SLOTS, n_groups, GROUP_W, TKW + C), jnp.bfloat16),
                        pltpu.VMEM((N_SLOTS, n_groups, GROUP_W, 1), jnp.float32),
                        pltpu.VMEM((D_A + D_B, D_MODEL), jnp.bfloat16)],
        compiler_params=pltpu.CompilerParams(
            dimension_semantics=("arbitrary", "arbitrary"), vmem_limit_bytes=VMEM_LIMIT),
        name="nbr_attn_out",
    )(x, gate, oa, gb, q, k, v, kc, vc, tab, w_out)


def kernel(x, c, ctx, c_ctx, w_ada, b_ada, norm_g, w_in, sgu_norm_g, w_spatial, b_spatial,
           q_norm_g, k_norm_g, rpb, w_out):
    B, L, D = x.shape
    depth = w_ada.shape[0]
    f32, bf16 = jnp.float32, jnp.bfloat16
    bd = jnp.asarray(np.kron(np.eye(MXU_DIM // HEAD_DIM), np.ones((HEAD_DIM, HEAD_DIM))), bf16)
    cc = jnp.concatenate([c, c_ctx[None, :], jnp.zeros((ADA_ROWS - B - 1, D), f32)], axis=0)
    assert depth == 1
    for layer in range(depth):
        mod = _ada_call(cc, w_ada[layer], b_ada[layer][None, :])
        shift, scale, gate = (mod[:B, j * D:(j + 1) * D].reshape(B, 1, D) for j in range(3))
        cshift, cscale = (mod[B:B + 1, j * D:(j + 1) * D] for j in range(2))

        ng = norm_g[layer][None, :]
        kg = jnp.tile(k_norm_g[layer], HEADS)[None, :]
        qg = jnp.tile(q_norm_g[layer], HEADS)[None, :] * (HEAD_DIM ** -0.5 * LOG2E)
        bs = jnp.repeat(b_spatial[layer].T, D_A // SGU_GROUPS, axis=1)
        oa, gb, q, k, v = _proj_call(
            x, shift, scale, ng, w_in[layer], sgu_norm_g[layer][None, :],
            w_spatial[layer].astype(bf16), bs, qg, kg, bd)
        kc, vc = _ctx_call(ctx, cshift, cscale, ng, w_in[layer], kg, bd)
        tab = _rpb_call(rpb[layer])
        x = _attn_call(x, gate, oa, gb, q, k, v, kc, vc, tab, w_out[layer])
    return x
```

```python
import numpy as np
import jax
import jax.numpy as jnp
from jax import lax
from jax.experimental import pallas as pl
from jax.experimental.pallas import tpu as pltpu

D_MODEL = 1024
GRID_W = 64
D_A = 512
D_B = 512
CHUNK = 128
SGU_GROUPS = 4
HEAD_DIM = 64
HEADS = 8
HEAD_GROUP = 4
GROUP_W = HEAD_GROUP * HEAD_DIM
WIN_R = 8
WIN_C = 16
D_IN = 3 * D_A + 4 * D_B
EPS = 1e-6
NEG_INF = -1e30
LOG2E = float(np.log2(np.e))

LANES = 128
MXU_DIM = 256
VMEM_LIMIT = 56 * 1024 * 1024

TM = 512
Q_ROWS = 16
N_SLOTS = 4
LEAD = 2
TO = 1024
SOFTMAX_ROWS = 32
TQ = Q_ROWS * GRID_W
TKW = WIN_R * GRID_W
ADA_ROWS = 16
ADA_BN = 768

_NT = (((1,), (1,)), ((), ()))


def _silu(x):
    return x / (1.0 + jnp.exp(-x))


def _gelu(x):
    return 0.5 * x * (1.0 + lax.erf(x * np.float32(np.sqrt(0.5))))


def _dot(a, b):
    return jnp.dot(a, b, preferred_element_type=jnp.float32)


def _ada_kernel(c_ref, w_ref, b_ref, o_ref):
    a, w = _silu(c_ref[...]), w_ref[...]
    a_hi, w_hi = a.astype(jnp.bfloat16), w.astype(jnp.bfloat16)
    a_lo = (a - a_hi.astype(jnp.float32)).astype(jnp.bfloat16)
    w_lo = (w - w_hi.astype(jnp.float32)).astype(jnp.bfloat16)
    o_ref[...] = _dot(a_hi, w_hi) + (_dot(a_lo, w_hi) + _dot(a_hi, w_lo)) + b_ref[...]


def _ada_call(cc, w_ada, b_ada):
    n = w_ada.shape[1]
    return pl.pallas_call(
        _ada_kernel,
        grid=(n // ADA_BN,),
        in_specs=[
            pl.BlockSpec((ADA_ROWS, D_MODEL), lambda j: (0, 0)),
            pl.BlockSpec((D_MODEL, ADA_BN), lambda j: (0, j)),
            pl.BlockSpec((1, ADA_BN), lambda j: (0, j)),
        ],
        out_specs=pl.BlockSpec((ADA_ROWS, ADA_BN), lambda j: (0, j)),
        out_shape=jax.ShapeDtypeStruct((ADA_ROWS, n), jnp.float32),
        compiler_params=pltpu.CompilerParams(dimension_semantics=("arbitrary",)),
        name="ada_params",
    )(cc, w_ada, b_ada)


def _prenorm(x, g, shift, scale):
    ms = jnp.mean(x * x, axis=-1, keepdims=True)
    h = x * lax.rsqrt(ms + EPS) * g
    return (h * (1.0 + scale) + shift).astype(jnp.bfloat16)


def _head_rms(z, bd, gain):
    sq = (z * z).astype(jnp.bfloat16)
    ss = jnp.concatenate(
        [_dot(sq[:, c * MXU_DIM:(c + 1) * MXU_DIM], bd) for c in range(D_B // MXU_DIM)], axis=-1)
    return z * lax.rsqrt(ss * (1.0 / HEAD_DIM) + EPS) * gain


def _proj_kernel(x_ref, shift_ref, scale_ref, ng_ref, w_ref, sg_ref, ws_ref, bs_ref,
                 qg_ref, kg_ref, bd_ref,
                 oa_ref, gb_ref, q_ref, k_ref, v_ref, wb_ref):
    @pl.when((pl.program_id(0) == 0) & (pl.program_id(1) == 0))
    def _():
        for lo in range(0, D_IN, D_A):
            wb_ref[:, lo:lo + D_A] = w_ref[:, lo:lo + D_A].astype(wb_ref.dtype)

    hb = _prenorm(x_ref[0], ng_ref[...], shift_ref[0], scale_ref[0])

    def zcols(lo, width):
        return _dot(hb, wb_ref[:, lo:lo + width])

    gu = _gelu(zcols(0, D_A))
    gv = _gelu(zcols(D_A, D_A))
    sa = _silu(zcols(2 * D_A, D_A))
    for g in range(SGU_GROUPS):
        cs = slice(g * LANES, (g + 1) * LANES)
        vg = gv[:, cs]
        ms = jnp.mean(vg * vg, axis=-1, keepdims=True)
        vn = (vg * lax.rsqrt(ms + EPS) * sg_ref[:, cs]).astype(jnp.bfloat16)
        chunks = [slice(c * CHUNK, (c + 1) * CHUNK) for c in range(TM // CHUNK)]
        mixed = _dot(ws_ref[g], jnp.concatenate([vn[rs] for rs in chunks], axis=1))
        for c, rs in enumerate(chunks):
            mixed_c = mixed[:, c * CHUNK:(c + 1) * CHUNK] + bs_ref[:, cs]
            oa_ref[0, rs, cs] = (gu[rs, cs] * mixed_c * sa[rs, cs]).astype(oa_ref.dtype)

    bd = bd_ref[...]
    q_ref[0] = _head_rms(zcols(3 * D_A, D_B), bd, qg_ref[...]).astype(q_ref.dtype)
    k_ref[0] = _head_rms(zcols(3 * D_A + D_B, D_B), bd, kg_ref[...]).astype(k_ref.dtype)
    v_ref[0] = zcols(3 * D_A + 2 * D_B, D_B).astype(v_ref.dtype)
    gb_ref[0] = _silu(zcols(3 * D_A + 3 * D_B, D_B)).astype(gb_ref.dtype)


def _proj_call(x, shift, scale, ng, w_in, sg, ws, bs, qg, kg, bd):
    B, L, _ = x.shape
    const2 = lambda b, i: (0, 0)
    half_spec = pl.BlockSpec((1, TM, D_A), lambda b, i: (b, i, 0))
    half_shape = jax.ShapeDtypeStruct((B, L, D_A), jnp.bfloat16)
    return pl.pallas_call(
        _proj_kernel,
        grid=(B, L // TM),
        in_specs=[
            pl.BlockSpec((1, TM, D_MODEL), lambda b, i: (b, i, 0)),
            pl.BlockSpec((1, 1, D_MODEL), lambda b, i: (b, 0, 0)),
            pl.BlockSpec((1, 1, D_MODEL), lambda b, i: (b, 0, 0)),
            pl.BlockSpec((1, D_MODEL), const2),
            pl.BlockSpec((D_MODEL, D_IN), const2),
            pl.BlockSpec((1, D_A), const2),
            pl.BlockSpec((SGU_GROUPS, CHUNK, CHUNK), lambda b, i: (0, 0, 0)),
            pl.BlockSpec((CHUNK, D_A), const2),
            pl.BlockSpec((1, D_B), const2),
            pl.BlockSpec((1, D_B), const2),
            pl.BlockSpec((MXU_DIM, MXU_DIM), const2),
        ],
        out_specs=[half_spec] * 5,
        out_shape=[half_shape] * 5,
        scratch_shapes=[pltpu.VMEM((D_MODEL, D_IN), jnp.bfloat16)],
        compiler_params=pltpu.CompilerParams(
            dimension_semantics=("arbitrary", "arbitrary"), vmem_limit_bytes=VMEM_LIMIT),
        name="latent_proj",
    )(x, shift, scale, ng, w_in, sg, ws, bs, qg, kg, bd)


def _ctx_kernel(x_ref, shift_ref, scale_ref, ng_ref, w_ref, kg_ref, bd_ref, k_ref, v_ref):
    hb = _prenorm(x_ref[0], ng_ref[...], shift_ref[...], scale_ref[...])
    wk = w_ref[:, 0:D_B].astype(jnp.bfloat16)
    wv = w_ref[:, D_B:2 * D_B].astype(jnp.bfloat16)
    k_ref[0] = _head_rms(_dot(hb, wk), bd_ref[...], kg_ref[...]).astype(k_ref.dtype)
    v_ref[0] = _dot(hb, wv).astype(v_ref.dtype)


def _ctx_call(ctx, cshift, cscale, ng, w_in, kg, bd):
    B, C, _ = ctx.shape
    const2 = lambda b: (0, 0)
    kv_block = (3 * D_A + D_B) // (2 * D_B)
    kv_spec = pl.BlockSpec((1, C, D_B), lambda b: (b, 0, 0))
    kv_shape = jax.ShapeDtypeStruct((B, C, D_B), jnp.bfloat16)
    return pl.pallas_call(
        _ctx_kernel,
        grid=(B,),
        in_specs=[
            pl.BlockSpec((1, C, D_MODEL), lambda b: (b, 0, 0)),
            pl.BlockSpec((1, D_MODEL), const2),
            pl.BlockSpec((1, D_MODEL), const2),
            pl.BlockSpec((1, D_MODEL), const2),
            pl.BlockSpec((D_MODEL, 2 * D_B), lambda b: (0, kv_block)),
            pl.BlockSpec((1, D_B), const2),
            pl.BlockSpec((MXU_DIM, MXU_DIM), const2),
        ],
        out_specs=[kv_spec, kv_spec],
        out_shape=[kv_shape, kv_shape],
        compiler_params=pltpu.CompilerParams(
            dimension_semantics=("arbitrary",), vmem_limit_bytes=VMEM_LIMIT),
        name="ctx_kv",
    )(ctx, cshift, cscale, ng, w_in, kg, bd)


def _rpb_kernel(rpb_ref, win_ref, o_ref):
    n_heads, n_pairs = o_ref.shape[:2]
    n_dr = n_pairs + 1
    blk = o_ref.shape[2:]
    in_win = win_ref[...] != 0
    left_half = lax.broadcasted_iota(jnp.int32, blk, 1) < GRID_W
    first = LANES - (WIN_C - 1)
    for h in range(n_heads):
        for dr in range(n_pairs):
            row = h * n_dr + dr
            lo = pltpu.roll(jnp.broadcast_to(rpb_ref[row:row + 1, :], blk), first, 1,
                            stride=1, stride_axis=0)
            hi = pltpu.roll(jnp.broadcast_to(rpb_ref[row + 1:row + 2, :], blk),
                            (first + GRID_W) % LANES, 1, stride=1, stride_axis=0)
            o_ref[h, dr] = jnp.where(in_win, jnp.where(left_half, lo, hi) * LOG2E, NEG_INF)


def _rpb_call(rpb):
    H, n_dr, n_dc = rpb.shape
    cols = np.arange(GRID_W)
    c0 = np.clip(cols - WIN_C // 2, 0, GRID_W - WIN_C)
    in_win = (cols[None, :] >= c0[:, None]) & (cols[None, :] < c0[:, None] + WIN_C)
    win = jnp.asarray(np.tile(in_win, (1, LANES // GRID_W)), jnp.int32)
    rpb2 = jnp.pad(rpb.astype(jnp.float32).reshape(H * n_dr, n_dc), ((0, 0), (0, LANES - n_dc)))
    return pl.pallas_call(
        _rpb_kernel,
        out_shape=jax.ShapeDtypeStruct((H, n_dr - 1, GRID_W, LANES), jnp.float32),
        name="rpb_toeplitz",
    )(rpb2, win)


def _attn_kernel(gb_ref, q_ref, k_ref, v_ref, kc_ref, vc_ref, tab_ref, o_ref, *slot_refs):
    i = pl.program_id(1)
    s_refs, p_refs, m_refs, l_refs = (slot_refs[k * N_SLOTS:(k + 1) * N_SLOTS] for k in range(4))
    rows = pl.num_programs(1) * Q_ROWS
    shape = (GROUP_W, GROUP_W)
    row_head = lax.broadcasted_iota(jnp.int32, shape, 0) // HEAD_DIM
    lane_head = lax.broadcasted_iota(jnp.int32, shape, 1) // HEAD_DIM
    own_head = row_head == lane_head

    n_ctx = kc_ref.shape[1]
    groups = range(HEADS // HEAD_GROUP)

    def window(rho):
        r = i * Q_ROWS + rho
        r0 = jnp.clip(r - WIN_R // 2, 0, rows - WIN_R)
        d0 = r0 - r + (WIN_R - 1)
        return pl.multiple_of(r0 * GRID_W, GRID_W), d0

    def qrows(rho):
        if isinstance(rho, int):
            return slice(rho * GRID_W, (rho + 1) * GRID_W)
        return pl.ds(pl.multiple_of(rho * GRID_W, GRID_W), GRID_W)

    def scores(rho, slot):
        tok0, d0 = window(rho)
        for g in groups:
            lanes = slice(g * GROUP_W, (g + 1) * GROUP_W)
            qr = q_ref[0, qrows(rho), lanes]
            q_bd = jnp.where(own_head, jnp.concatenate([qr] * HEAD_GROUP, axis=0),
                             jnp.zeros(shape, qr.dtype))
            bias = jnp.concatenate(
                [jnp.concatenate([tab_ref[g * HEAD_GROUP + h, d0 + 2 * t] for t in range(WIN_R // 2)],
                                 axis=1) for h in range(HEAD_GROUP)], axis=0)
            s_refs[slot][g, :, 0:TKW] = lax.dot_general(
                q_bd, k_ref[0, pl.ds(tok0, TKW), lanes], _NT,
                preferred_element_type=jnp.float32) + bias
            s_refs[slot][g, :, TKW:TKW + n_ctx] = lax.dot_general(
                q_bd, kc_ref[0, :, lanes], _NT, preferred_element_type=jnp.float32)

    def softmax(slot):
        chunks = [slice(c * SOFTMAX_ROWS, (c + 1) * SOFTMAX_ROWS)
                  for c in range(GROUP_W // SOFTMAX_ROWS)]
        for g in groups:
            for rs in chunks:
                m = jnp.max(s_refs[slot][g, rs, :], axis=-1, keepdims=True)
                m_refs[slot][g, rs, :] = jnp.broadcast_to(m, (SOFTMAX_ROWS, LANES))
        for g in groups:
            for rs in chunks:
                m = jnp.tile(m_refs[slot][g, rs, :], (1, (TKW + n_ctx) // LANES))
                p = jnp.exp2(s_refs[slot][g, rs, :] - m)
                l_refs[slot][g, rs, :] = jnp.broadcast_to(
                    jnp.sum(p, axis=-1, keepdims=True), (SOFTMAX_ROWS, LANES))
                p_refs[slot][g, rs, :] = p.astype(p_refs[slot].dtype)

    def values(rho, slot):
        tok0, _ = window(rho)
        for g in groups:
            lanes = slice(g * GROUP_W, (g + 1) * GROUP_W)
            o = (_dot(p_refs[slot][g, :, 0:TKW], v_ref[0, pl.ds(tok0, TKW), lanes])
                 + _dot(p_refs[slot][g, :, TKW:TKW + n_ctx], vc_ref[0, :, lanes]))
            denom = jnp.tile(l_refs[slot][g], (1, GROUP_W // LANES))
            o = jnp.where(own_head, o / denom, 0.0)
            og = o[0:GRID_W]
            for h in range(1, HEAD_GROUP):
                og = og + o[h * GRID_W:(h + 1) * GRID_W]
            o_ref[0, qrows(rho), lanes] = (
                og * gb_ref[0, qrows(rho), lanes].astype(jnp.float32)).astype(o_ref.dtype)

    def phase(r, slot):
        if not isinstance(r, int) or r + LEAD < Q_ROWS:
            scores(r + LEAD, (slot + LEAD) % N_SLOTS)
        if not isinstance(r, int) or 0 <= r < Q_ROWS:
            softmax(slot)
        if not isinstance(r, int) or r - LEAD >= 0:
            values(r - LEAD, (slot - LEAD) % N_SLOTS)

    for r in range(-LEAD, LEAD):
        phase(r, r % N_SLOTS)

    def trip(j, carry):
        for t in range(N_SLOTS):
            phase(LEAD + N_SLOTS * j + t, (LEAD + t) % N_SLOTS)
        return carry

    lax.fori_loop(0, (Q_ROWS - 2 * LEAD) // N_SLOTS, trip, 0)
    for r in range(Q_ROWS - LEAD, Q_ROWS + LEAD):
        phase(r, r % N_SLOTS)


def _attn_call(gb, q, k, v, kc, vc, tab):
    B, L, _ = q.shape
    C = kc.shape[1]
    n_groups = HEADS // HEAD_GROUP
    tok_spec = pl.BlockSpec((1, TQ, D_B), lambda b, i: (b, i, 0))
    batch_spec = lambda n: pl.BlockSpec((1, n, D_B), lambda b, i: (b, 0, 0))
    return pl.pallas_call(
        _attn_kernel,
        grid=(B, L // TQ),
        in_specs=[tok_spec, tok_spec, batch_spec(L), batch_spec(L), batch_spec(C), batch_spec(C),
                  pl.BlockSpec(tab.shape, lambda b, i: (0, 0, 0, 0))],
        out_specs=tok_spec,
        out_shape=jax.ShapeDtypeStruct((B, L, D_B), jnp.bfloat16),
        scratch_shapes=(
            [pltpu.VMEM((n_groups, GROUP_W, TKW + C), jnp.float32)] * N_SLOTS
            + [pltpu.VMEM((n_groups, GROUP_W, TKW + C), jnp.bfloat16)] * N_SLOTS
            + [pltpu.VMEM((n_groups, GROUP_W, LANES), jnp.float32)] * (2 * N_SLOTS)),
        compiler_params=pltpu.CompilerParams(
            dimension_semantics=("arbitrary", "arbitrary"), vmem_limit_bytes=VMEM_LIMIT),
        name="nbr_attn",
    )(gb, q, k, v, kc, vc, tab)


def _out_kernel(x_ref, gate_ref, oa_ref, ob_ref, w_ref, o_ref, wb_ref):
    @pl.when((pl.program_id(0) == 0) & (pl.program_id(1) == 0))
    def _():
        wb_ref[...] = w_ref[...].astype(wb_ref.dtype)

    mix = _dot(oa_ref[0], wb_ref[0:D_A, :]) + _dot(ob_ref[0], wb_ref[D_A:D_A + D_B, :])
    o_ref[0] = x_ref[0] + gate_ref[0] * mix


def _out_call(x, gate, oa, ob, w_out):
    B, L, D = x.shape
    tok_spec = lambda width: pl.BlockSpec((1, TO, width), lambda b, i: (b, i, 0))
    return pl.pallas_call(
        _out_kernel,
        grid=(B, L // TO),
        in_specs=[tok_spec(D), pl.BlockSpec((1, 1, D), lambda b, i: (b, 0, 0)),
                  tok_spec(D_A), tok_spec(D_B),
                  pl.BlockSpec((D_A + D_B, D), lambda b, i: (0, 0))],
        out_specs=tok_spec(D),
        out_shape=jax.ShapeDtypeStruct((B, L, D), jnp.float32),
        scratch_shapes=[pltpu.VMEM((D_A + D_B, D), jnp.bfloat16)],
        compiler_params=pltpu.CompilerParams(
            dimension_semantics=("arbitrary", "arbitrary"), vmem_limit_bytes=VMEM_LIMIT),
        name="out_proj",
    )(x, gate, oa, ob, w_out)


def kernel(x, c, ctx, c_ctx, w_ada, b_ada, norm_g, w_in, sgu_norm_g, w_spatial, b_spatial,
           q_norm_g, k_norm_g, rpb, w_out):
    B, L, D = x.shape
    depth = w_ada.shape[0]
    f32, bf16 = jnp.float32, jnp.bfloat16
    bd = jnp.asarray(np.kron(np.eye(MXU_DIM // HEAD_DIM), np.ones((HEAD_DIM, HEAD_DIM))), bf16)
    cc = jnp.concatenate([c, c_ctx[None, :], jnp.zeros((ADA_ROWS - B - 1, D), f32)], axis=0)
    assert depth == 1
    for layer in range(depth):
        mod = _ada_call(cc, w_ada[layer], b_ada[layer][None, :])
        shift, scale, gate = (mod[:B, j * D:(j + 1) * D].reshape(B, 1, D) for j in range(3))
        cshift, cscale = (mod[B:B + 1, j * D:(j + 1) * D] for j in range(2))

        ng = norm_g[layer][None, :]
        kg = jnp.tile(k_norm_g[layer], HEADS)[None, :]
        qg = jnp.tile(q_norm_g[layer], HEADS)[None, :] * (HEAD_DIM ** -0.5 * LOG2E)
        bs = jnp.repeat(b_spatial[layer].T, D_A // SGU_GROUPS, axis=1)
        oa, gb, q, k, v = _proj_call(
            x, shift, scale, ng, w_in[layer], sgu_norm_g[layer][None, :],
            w_spatial[layer].astype(bf16), bs, qg, kg, bd)
        kc, vc = _ctx_call(ctx, cshift, cscale, ng, w_in[layer], kg, bd)
        tab = _rpb_call(rpb[layer])
        ob = _attn_call(gb, q, k, v, kc, vc, tab)
        x = _out_call(x, gate, oa, ob, w_out[layer])
    return x
```

```python
import numpy as np
import jax
import jax.numpy as jnp
from jax import lax
from jax.experimental import pallas as pl
from jax.experimental.pallas import tpu as pltpu

D_MODEL = 1024
GRID_W = 64
D_A = 512
D_B = 512
CHUNK = 128
SGU_GROUPS = 4
HEAD_DIM = 64
HEADS = 8
HEAD_GROUP = 4
GROUP_W = HEAD_GROUP * HEAD_DIM
WIN_R = 8
WIN_C = 16
D_IN = 3 * D_A + 4 * D_B
EPS = 1e-6
NEG_INF = -1e30
LOG2E = float(np.log2(np.e))

LANES = 128
MXU_DIM = 256
VMEM_LIMIT = 56 * 1024 * 1024

TM = 512
Q_ROWS = 8
OUT_ROWS = 4
N_SLOTS = 2
SOFTMAX_ROWS = 32
CTX_TM = 1024
TQ = Q_ROWS * GRID_W
TKW = WIN_R * GRID_W
ADA_ROWS = 16
ADA_BN = 768

_NT = (((1,), (1,)), ((), ()))


def _silu(x):
    return x / (1.0 + jnp.exp(-x))


def _gelu(x):
    return 0.5 * x * (1.0 + lax.erf(x * np.float32(np.sqrt(0.5))))


def _dot(a, b):
    return jnp.dot(a, b, preferred_element_type=jnp.float32)


def _ada_kernel(c_ref, w_ref, b_ref, o_ref):
    a, w = _silu(c_ref[...]), w_ref[...]
    a_hi, w_hi = a.astype(jnp.bfloat16), w.astype(jnp.bfloat16)
    a_lo = (a - a_hi.astype(jnp.float32)).astype(jnp.bfloat16)
    w_lo = (w - w_hi.astype(jnp.float32)).astype(jnp.bfloat16)
    o_ref[...] = _dot(a_hi, w_hi) + (_dot(a_lo, w_hi) + _dot(a_hi, w_lo)) + b_ref[...]


def _ada_call(cc, w_ada, b_ada):
    n = w_ada.shape[1]
    return pl.pallas_call(
        _ada_kernel,
        grid=(n // ADA_BN,),
        in_specs=[
            pl.BlockSpec((ADA_ROWS, D_MODEL), lambda j: (0, 0)),
            pl.BlockSpec((D_MODEL, ADA_BN), lambda j: (0, j)),
            pl.BlockSpec((1, ADA_BN), lambda j: (0, j)),
        ],
        out_specs=pl.BlockSpec((ADA_ROWS, ADA_BN), lambda j: (0, j)),
        out_shape=jax.ShapeDtypeStruct((ADA_ROWS, n), jnp.float32),
        compiler_params=pltpu.CompilerParams(dimension_semantics=("arbitrary",)),
        name="ada_params",
    )(cc, w_ada, b_ada)


def _prenorm(x, g, shift, scale):
    ms = jnp.mean(x * x, axis=-1, keepdims=True)
    h = x * lax.rsqrt(ms + EPS) * g
    return (h * (1.0 + scale) + shift).astype(jnp.bfloat16)


def _head_rms(z, bd, gain):
    sq = (z * z).astype(jnp.bfloat16)
    ss = jnp.concatenate(
        [_dot(sq[:, c * MXU_DIM:(c + 1) * MXU_DIM], bd) for c in range(D_B // MXU_DIM)], axis=-1)
    return z * lax.rsqrt(ss * (1.0 / HEAD_DIM) + EPS) * gain


def _proj_kernel(x_ref, shift_ref, scale_ref, ng_ref, w_ref, sg_ref, ws_ref, bs_ref,
                 qg_ref, kg_ref, bd_ref,
                 oa_ref, gb_ref, q_ref, k_ref, v_ref, wb_ref):
    @pl.when((pl.program_id(0) == 0) & (pl.program_id(1) == 0))
    def _():
        for lo in range(0, D_IN, D_A):
            wb_ref[:, lo:lo + D_A] = w_ref[:, lo:lo + D_A].astype(wb_ref.dtype)

    hb = _prenorm(x_ref[0], ng_ref[...], shift_ref[0], scale_ref[0])

    def zcols(lo, width):
        return _dot(hb, wb_ref[:, lo:lo + width])

    gu = _gelu(zcols(0, D_A))
    gv = _gelu(zcols(D_A, D_A))
    sa = _silu(zcols(2 * D_A, D_A))
    for g in range(SGU_GROUPS):
        cs = slice(g * LANES, (g + 1) * LANES)
        vg = gv[:, cs]
        ms = jnp.mean(vg * vg, axis=-1, keepdims=True)
        vn = (vg * lax.rsqrt(ms + EPS) * sg_ref[:, cs]).astype(jnp.bfloat16)
        chunks = [slice(c * CHUNK, (c + 1) * CHUNK) for c in range(TM // CHUNK)]
        mixed = _dot(ws_ref[g], jnp.concatenate([vn[rs] for rs in chunks], axis=1))
        for c, rs in enumerate(chunks):
            mixed_c = mixed[:, c * CHUNK:(c + 1) * CHUNK] + bs_ref[:, cs]
            oa_ref[0, rs, cs] = (gu[rs, cs] * mixed_c * sa[rs, cs]).astype(oa_ref.dtype)

    bd = bd_ref[...]
    q_ref[0] = _head_rms(zcols(3 * D_A, D_B), bd, qg_ref[...]).astype(q_ref.dtype)
    k_ref[0] = _head_rms(zcols(3 * D_A + D_B, D_B), bd, kg_ref[...]).astype(k_ref.dtype)
    v_ref[0] = zcols(3 * D_A + 2 * D_B, D_B).astype(v_ref.dtype)
    gb_ref[0] = _silu(zcols(3 * D_A + 3 * D_B, D_B)).astype(gb_ref.dtype)


def _proj_call(x, shift, scale, ng, w_in, sg, ws, bs, qg, kg, bd):
    B, L, _ = x.shape
    const2 = lambda b, i: (0, 0)
    half_spec = pl.BlockSpec((1, TM, D_A), lambda b, i: (b, i, 0))
    half_shape = jax.ShapeDtypeStruct((B, L, D_A), jnp.bfloat16)
    return pl.pallas_call(
        _proj_kernel,
        grid=(B, L // TM),
        in_specs=[
            pl.BlockSpec((1, TM, D_MODEL), lambda b, i: (b, i, 0)),
            pl.BlockSpec((1, 1, D_MODEL), lambda b, i: (b, 0, 0)),
            pl.BlockSpec((1, 1, D_MODEL), lambda b, i: (b, 0, 0)),
            pl.BlockSpec((1, D_MODEL), const2),
            pl.BlockSpec((D_MODEL, D_IN), const2),
            pl.BlockSpec((1, D_A), const2),
            pl.BlockSpec((SGU_GROUPS, CHUNK, CHUNK), lambda b, i: (0, 0, 0)),
            pl.BlockSpec((CHUNK, D_A), const2),
            pl.BlockSpec((1, D_B), const2),
            pl.BlockSpec((1, D_B), const2),
            pl.BlockSpec((MXU_DIM, MXU_DIM), const2),
        ],
        out_specs=[half_spec] * 5,
        out_shape=[half_shape] * 5,
        scratch_shapes=[pltpu.VMEM((D_MODEL, D_IN), jnp.bfloat16)],
        compiler_params=pltpu.CompilerParams(
            dimension_semantics=("arbitrary", "arbitrary"), vmem_limit_bytes=VMEM_LIMIT),
        name="latent_proj",
    )(x, shift, scale, ng, w_in, sg, ws, bs, qg, kg, bd)


def _ctx_kernel(x_ref, shift_ref, scale_ref, ng_ref, w_ref, kg_ref, bd_ref, k_ref, v_ref):
    hb = _prenorm(x_ref[...], ng_ref[...], shift_ref[...], scale_ref[...])
    wk = w_ref[:, 0:D_B].astype(jnp.bfloat16)
    wv = w_ref[:, D_B:2 * D_B].astype(jnp.bfloat16)
    k_ref[...] = _head_rms(_dot(hb, wk), bd_ref[...], kg_ref[...]).astype(k_ref.dtype)
    v_ref[...] = _dot(hb, wv).astype(v_ref.dtype)


def _ctx_call(ctx, cshift, cscale, ng, w_in, kg, bd):
    B, C, _ = ctx.shape
    const2 = lambda b: (0, 0)
    kv_block = (3 * D_A + D_B) // (2 * D_B)
    kv_spec = pl.BlockSpec((CTX_TM, D_B), lambda b: (b, 0))
    kv_shape = jax.ShapeDtypeStruct((B * C, D_B), jnp.bfloat16)
    kc, vc = pl.pallas_call(
        _ctx_kernel,
        grid=(B * C // CTX_TM,),
        in_specs=[
            pl.BlockSpec((CTX_TM, D_MODEL), lambda b: (b, 0)),
            pl.BlockSpec((1, D_MODEL), const2),
            pl.BlockSpec((1, D_MODEL), const2),
            pl.BlockSpec((1, D_MODEL), const2),
            pl.BlockSpec((D_MODEL, 2 * D_B), lambda b: (0, kv_block)),
            pl.BlockSpec((1, D_B), const2),
            pl.BlockSpec((MXU_DIM, MXU_DIM), const2),
        ],
        out_specs=[kv_spec, kv_spec],
        out_shape=[kv_shape, kv_shape],
        compiler_params=pltpu.CompilerParams(
            dimension_semantics=("arbitrary",), vmem_limit_bytes=VMEM_LIMIT),
        name="ctx_kv",
    )(ctx.reshape(B * C, D_MODEL), cshift, cscale, ng, w_in, kg, bd)
    return kc.reshape(B, C, D_B), vc.reshape(B, C, D_B)


def _rpb_kernel(rpb_ref, win_ref, o_ref):
    n_heads, n_pairs = o_ref.shape[:2]
    n_dr = n_pairs + 1
    blk = o_ref.shape[2:]
    in_win = win_ref[...] != 0
    left_half = lax.broadcasted_iota(jnp.int32, blk, 1) < GRID_W
    first = LANES - (WIN_C - 1)
    for h in range(n_heads):
        for dr in range(n_pairs):
            row = h * n_dr + dr
            lo = pltpu.roll(jnp.broadcast_to(rpb_ref[row:row + 1, :], blk), first, 1,
                            stride=1, stride_axis=0)
            hi = pltpu.roll(jnp.broadcast_to(rpb_ref[row + 1:row + 2, :], blk),
                            (first + GRID_W) % LANES, 1, stride=1, stride_axis=0)
            o_ref[h, dr] = jnp.where(in_win, jnp.where(left_half, lo, hi) * LOG2E, NEG_INF)


def _rpb_call(rpb):
    H, n_dr, n_dc = rpb.shape
    cols = np.arange(GRID_W)
    c0 = np.clip(cols - WIN_C // 2, 0, GRID_W - WIN_C)
    in_win = (cols[None, :] >= c0[:, None]) & (cols[None, :] < c0[:, None] + WIN_C)
    win = jnp.asarray(np.tile(in_win, (1, LANES // GRID_W)), jnp.int32)
    rpb2 = jnp.pad(rpb.astype(jnp.float32).reshape(H * n_dr, n_dc), ((0, 0), (0, LANES - n_dc)))
    return pl.pallas_call(
        _rpb_kernel,
        out_shape=jax.ShapeDtypeStruct((H, n_dr - 1, GRID_W, LANES), jnp.float32),
        name="rpb_toeplitz",
    )(rpb2, win)


def _attn_kernel(x_ref, gate_ref, oa_ref, gb_ref, q_ref, k_ref, v_ref, kc_ref, vc_ref,
                 tab_ref, wout_ref, o_ref, mix_ref, s_ref, p_ref, l_ref, wob_ref):
    i = pl.program_id(1)

    @pl.when((pl.program_id(0) == 0) & (i == 0))
    def _():
        wob_ref[...] = wout_ref[...].astype(wob_ref.dtype)

    rows = pl.num_programs(1) * Q_ROWS
    shape = (GROUP_W, GROUP_W)
    row_head = lax.broadcasted_iota(jnp.int32, shape, 0) // HEAD_DIM
    lane_head = lax.broadcasted_iota(jnp.int32, shape, 1) // HEAD_DIM
    own_head = row_head == lane_head

    n_ctx = kc_ref.shape[1]
    groups = range(HEADS // HEAD_GROUP)

    def window(rho):
        r = i * Q_ROWS + rho
        r0 = jnp.clip(r - WIN_R // 2, 0, rows - WIN_R)
        d0 = r0 - r + (WIN_R - 1)
        return pl.multiple_of(r0 * GRID_W, GRID_W), d0

    def qrows(rho):
        return slice(rho * GRID_W, (rho + 1) * GRID_W)

    def scores(rho, slot):
        tok0, d0 = window(rho)
        for g in groups:
            lanes = slice(g * GROUP_W, (g + 1) * GROUP_W)
            qr = q_ref[0, qrows(rho), lanes]
            q_bd = jnp.where(own_head, jnp.concatenate([qr] * HEAD_GROUP, axis=0),
                             jnp.zeros(shape, qr.dtype))
            bias = jnp.concatenate(
                [jnp.concatenate([tab_ref[g * HEAD_GROUP + h, d0 + 2 * t] for t in range(WIN_R // 2)],
                                 axis=1) for h in range(HEAD_GROUP)], axis=0)
            s_ref[slot, g, :, 0:TKW] = lax.dot_general(
                q_bd, k_ref[0, pl.ds(tok0, TKW), lanes], _NT,
                preferred_element_type=jnp.float32) + bias
            s_ref[slot, g, :, TKW:TKW + n_ctx] = lax.dot_general(
                q_bd, kc_ref[0, :, lanes], _NT, preferred_element_type=jnp.float32)

    def softmax(slot):
        for g in groups:
            for c in range(GROUP_W // SOFTMAX_ROWS):
                rs = slice(c * SOFTMAX_ROWS, (c + 1) * SOFTMAX_ROWS)
                s = s_ref[slot, g, rs, :]
                p = jnp.exp2(s - jnp.max(s, axis=-1, keepdims=True))
                l_ref[slot, g, rs, :] = jnp.sum(p, axis=-1, keepdims=True)
                p_ref[slot, g, rs, :] = p.astype(p_ref.dtype)

    def values(rho, slot):
        tok0, _ = window(rho)
        for g in groups:
            lanes = slice(g * GROUP_W, (g + 1) * GROUP_W)
            o = (_dot(p_ref[slot, g, :, 0:TKW], v_ref[0, pl.ds(tok0, TKW), lanes])
                 + _dot(p_ref[slot, g, :, TKW:TKW + n_ctx], vc_ref[0, :, lanes]))
            o = jnp.where(own_head, o / l_ref[slot, g], 0.0)
            og = o[0:GRID_W]
            for h in range(1, HEAD_GROUP):
                og = og + o[h * GRID_W:(h + 1) * GRID_W]
            mix_ref[qrows(rho), D_A + g * GROUP_W:D_A + (g + 1) * GROUP_W] = (
                og * gb_ref[0, qrows(rho), lanes].astype(jnp.float32)).astype(mix_ref.dtype)

    mix_ref[:, 0:D_A] = oa_ref[0]
    scores(0, 0)
    for rho in range(Q_ROWS):
        if rho + 1 < Q_ROWS:
            scores(rho + 1, (rho + 1) % N_SLOTS)
        softmax(rho % N_SLOTS)
        values(rho, rho % N_SLOTS)
        if (rho + 1) % OUT_ROWS == 0:
            ts = slice((rho + 1 - OUT_ROWS) * GRID_W, (rho + 1) * GRID_W)
            o_ref[0, ts] = x_ref[0, ts] + gate_ref[0] * _dot(mix_ref[ts], wob_ref[...])


def _attn_call(x, gate, oa, gb, q, k, v, kc, vc, tab, w_out):
    B, L, _ = x.shape
    C = kc.shape[1]
    n_groups = HEADS // HEAD_GROUP
    tok_spec = lambda width: pl.BlockSpec((1, TQ, width), lambda b, i: (b, i, 0))
    batch_spec = lambda n: pl.BlockSpec((1, n, D_B), lambda b, i: (b, 0, 0))
    return pl.pallas_call(
        _attn_kernel,
        grid=(B, L // TQ),
        in_specs=[
            tok_spec(D_MODEL),
            pl.BlockSpec((1, 1, D_MODEL), lambda b, i: (b, 0, 0)),
            tok_spec(D_A),
            tok_spec(D_B),
            tok_spec(D_B),
            batch_spec(L),
            batch_spec(L),
            batch_spec(C),
            batch_spec(C),
            pl.BlockSpec(tab.shape, lambda b, i: (0, 0, 0, 0)),
            pl.BlockSpec((D_A + D_B, D_MODEL), lambda b, i: (0, 0)),
        ],
        out_specs=tok_spec(D_MODEL),
        out_shape=jax.ShapeDtypeStruct((B, L, D_MODEL), jnp.float32),
        scratch_shapes=[pltpu.VMEM((TQ, D_A + D_B), jnp.bfloat16),
                        pltpu.VMEM((N_SLOTS, n_groups, GROUP_W, TKW + C), jnp.float32),
                        pltpu.VMEM((N_SLOTS, n_groups, GROUP_W, TKW + C), jnp.bfloat16),
                        pltpu.VMEM((N_SLOTS, n_groups, GROUP_W, 1), jnp.float32),
                        pltpu.VMEM((D_A + D_B, D_MODEL), jnp.bfloat16)],
        compiler_params=pltpu.CompilerParams(
            dimension_semantics=("arbitrary", "arbitrary"), vmem_limit_bytes=VMEM_LIMIT),
        name="nbr_attn_out",
    )(x, gate, oa, gb, q, k, v, kc, vc, tab, w_out)


def kernel(x, c, ctx, c_ctx, w_ada, b_ada, norm_g, w_in, sgu_norm_g, w_spatial, b_spatial,
           q_norm_g, k_norm_g, rpb, w_out):
    B, L, D = x.shape
    depth = w_ada.shape[0]
    f32, bf16 = jnp.float32, jnp.bfloat16
    bd = jnp.asarray(np.kron(np.eye(MXU_DIM // HEAD_DIM), np.ones((HEAD_DIM, HEAD_DIM))), bf16)
    cc = jnp.concatenate([c, c_ctx[None, :], jnp.zeros((ADA_ROWS - B - 1, D), f32)], axis=0)
    assert depth == 1
    for layer in range(depth):
        mod = _ada_call(cc, w_ada[layer], b_ada[layer][None, :])
        shift, scale, gate = (mod[:B, j * D:(j + 1) * D].reshape(B, 1, D) for j in range(3))
        cshift, cscale = (mod[B:B + 1, j * D:(j + 1) * D] for j in range(2))

        ng = norm_g[layer][None, :]
        kg = jnp.tile(k_norm_g[layer], HEADS)[None, :]
        qg = jnp.tile(q_norm_g[layer], HEADS)[None, :] * (HEAD_DIM ** -0.5 * LOG2E)
        bs = jnp.repeat(b_spatial[layer].T, D_A // SGU_GROUPS, axis=1)
        oa, gb, q, k, v = _proj_call(
            x, shift, scale, ng, w_in[layer], sgu_norm_g[layer][None, :],
            w_spatial[layer].astype(bf16), bs, qg, kg, bd)
        kc, vc = _ctx_call(ctx, cshift, cscale, ng, w_in[layer], kg, bd)
        tab = _rpb_call(rpb[layer])
        x = _attn_call(x, gate, oa, gb, q, k, v, kc, vc, tab, w_out[layer])
    return x
```

```python
import numpy as np
import jax
import jax.numpy as jnp
from jax import lax
from jax.experimental import pallas as pl
from jax.experimental.pallas import tpu as pltpu

D_MODEL = 1024
GRID_W = 64
D_A = 512
D_B = 512
CHUNK = 128
SGU_GROUPS = 4
HEAD_DIM = 64
HEADS = 8
HEAD_GROUP = 4
GROUP_W = HEAD_GROUP * HEAD_DIM
WIN_R = 8
WIN_C = 16
D_IN = 3 * D_A + 4 * D_B
EPS = 1e-6
NEG_INF = -1e30
LOG2E = float(np.log2(np.e))

LANES = 128
MXU_DIM = 256
VMEM_LIMIT = 56 * 1024 * 1024

TM = 512
Q_ROWS = 8
OUT_ROWS = 4
N_SLOTS = 2
SOFTMAX_ROWS = 32
CTX_TM = 1024
TQ = Q_ROWS * GRID_W
TKW = WIN_R * GRID_W
ADA_ROWS = 16
ADA_BN = 768

_NT = (((1,), (1,)), ((), ()))


def _silu(x):
    return x / (1.0 + jnp.exp(-x))


def _gelu(x):
    return 0.5 * x * (1.0 + lax.erf(x * np.float32(np.sqrt(0.5))))


def _dot(a, b):
    return jnp.dot(a, b, preferred_element_type=jnp.float32)


def _ada_kernel(c_ref, w_ref, b_ref, o_ref):
    a, w = _silu(c_ref[...]), w_ref[...]
    a_hi, w_hi = a.astype(jnp.bfloat16), w.astype(jnp.bfloat16)
    a_lo = (a - a_hi.astype(jnp.float32)).astype(jnp.bfloat16)
    w_lo = (w - w_hi.astype(jnp.float32)).astype(jnp.bfloat16)
    o_ref[...] = _dot(a_hi, w_hi) + (_dot(a_lo, w_hi) + _dot(a_hi, w_lo)) + b_ref[...]


def _ada_call(cc, w_ada, b_ada):
    n = w_ada.shape[1]
    return pl.pallas_call(
        _ada_kernel,
        grid=(n // ADA_BN,),
        in_specs=[
            pl.BlockSpec((ADA_ROWS, D_MODEL), lambda j: (0, 0)),
            pl.BlockSpec((D_MODEL, ADA_BN), lambda j: (0, j)),
            pl.BlockSpec((1, ADA_BN), lambda j: (0, j)),
        ],
        out_specs=pl.BlockSpec((ADA_ROWS, ADA_BN), lambda j: (0, j)),
        out_shape=jax.ShapeDtypeStruct((ADA_ROWS, n), jnp.float32),
        compiler_params=pltpu.CompilerParams(dimension_semantics=("arbitrary",)),
        name="ada_params",
    )(cc, w_ada, b_ada)


def _prenorm(x, g, shift, scale):
    ms = jnp.mean(x * x, axis=-1, keepdims=True)
    h = x * lax.rsqrt(ms + EPS) * g
    return (h * (1.0 + scale) + shift).astype(jnp.bfloat16)


def _head_rms(z, bd, gain):
    sq = (z * z).astype(jnp.bfloat16)
    ss = jnp.concatenate(
        [_dot(sq[:, c * MXU_DIM:(c + 1) * MXU_DIM], bd) for c in range(D_B // MXU_DIM)], axis=-1)
    return z * lax.rsqrt(ss * (1.0 / HEAD_DIM) + EPS) * gain


def _proj_kernel(x_ref, shift_ref, scale_ref, ng_ref, w_ref, sg_ref, ws_ref, bs_ref,
                 qg_ref, kg_ref, bd_ref,
                 oa_ref, gb_ref, q_ref, k_ref, v_ref, wb_ref):
    @pl.when((pl.program_id(0) == 0) & (pl.program_id(1) == 0))
    def _():
        for lo in range(0, D_IN, D_A):
            wb_ref[:, lo:lo + D_A] = w_ref[:, lo:lo + D_A].astype(wb_ref.dtype)

    hb = _prenorm(x_ref[0], ng_ref[...], shift_ref[0], scale_ref[0])

    def zcols(lo, width):
        return _dot(hb, wb_ref[:, lo:lo + width])

    bd = bd_ref[...]
    gu = _gelu(zcols(0, D_A))
    q_ref[0] = _head_rms(zcols(3 * D_A, D_B), bd, qg_ref[...]).astype(q_ref.dtype)
    gv = _gelu(zcols(D_A, D_A))
    k_ref[0] = _head_rms(zcols(3 * D_A + D_B, D_B), bd, kg_ref[...]).astype(k_ref.dtype)
    sa = _silu(zcols(2 * D_A, D_A))
    gb_ref[0] = _silu(zcols(3 * D_A + 3 * D_B, D_B)).astype(gb_ref.dtype)
    v_ref[0] = zcols(3 * D_A + 2 * D_B, D_B).astype(v_ref.dtype)

    for g in range(SGU_GROUPS):
        cs = slice(g * LANES, (g + 1) * LANES)
        vg = gv[:, cs]
        ms = jnp.mean(vg * vg, axis=-1, keepdims=True)
        vn = (vg * lax.rsqrt(ms + EPS) * sg_ref[:, cs]).astype(jnp.bfloat16)
        chunks = [slice(c * CHUNK, (c + 1) * CHUNK) for c in range(TM // CHUNK)]
        mixed = _dot(ws_ref[g], jnp.concatenate([vn[rs] for rs in chunks], axis=1))
        for c, rs in enumerate(chunks):
            mixed_c = mixed[:, c * CHUNK:(c + 1) * CHUNK] + bs_ref[:, cs]
            oa_ref[0, rs, cs] = (gu[rs, cs] * mixed_c * sa[rs, cs]).astype(oa_ref.dtype)


def _proj_call(x, shift, scale, ng, w_in, sg, ws, bs, qg, kg, bd):
    B, L, _ = x.shape
    const2 = lambda b, i: (0, 0)
    half_spec = pl.BlockSpec((1, TM, D_A), lambda b, i: (b, i, 0))
    half_shape = jax.ShapeDtypeStruct((B, L, D_A), jnp.bfloat16)
    return pl.pallas_call(
        _proj_kernel,
        grid=(B, L // TM),
        in_specs=[
            pl.BlockSpec((1, TM, D_MODEL), lambda b, i: (b, i, 0)),
            pl.BlockSpec((1, 1, D_MODEL), lambda b, i: (b, 0, 0)),
            pl.BlockSpec((1, 1, D_MODEL), lambda b, i: (b, 0, 0)),
            pl.BlockSpec((1, D_MODEL), const2),
            pl.BlockSpec((D_MODEL, D_IN), const2),
            pl.BlockSpec((1, D_A), const2),
            pl.BlockSpec((SGU_GROUPS, CHUNK, CHUNK), lambda b, i: (0, 0, 0)),
            pl.BlockSpec((CHUNK, D_A), const2),
            pl.BlockSpec((1, D_B), const2),
            pl.BlockSpec((1, D_B), const2),
            pl.BlockSpec((MXU_DIM, MXU_DIM), const2),
        ],
        out_specs=[half_spec] * 5,
        out_shape=[half_shape] * 5,
        scratch_shapes=[pltpu.VMEM((D_MODEL, D_IN), jnp.bfloat16)],
        compiler_params=pltpu.CompilerParams(
            dimension_semantics=("arbitrary", "arbitrary"), vmem_limit_bytes=VMEM_LIMIT),
        name="latent_proj",
    )(x, shift, scale, ng, w_in, sg, ws, bs, qg, kg, bd)


def _ctx_kernel(x_ref, shift_ref, scale_ref, ng_ref, w_ref, kg_ref, bd_ref, k_ref, v_ref):
    hb = _prenorm(x_ref[...], ng_ref[...], shift_ref[...], scale_ref[...])
    wk = w_ref[:, 0:D_B].astype(jnp.bfloat16)
    wv = w_ref[:, D_B:2 * D_B].astype(jnp.bfloat16)
    k_ref[...] = _head_rms(_dot(hb, wk), bd_ref[...], kg_ref[...]).astype(k_ref.dtype)
    v_ref[...] = _dot(hb, wv).astype(v_ref.dtype)


def _ctx_call(ctx, cshift, cscale, ng, w_in, kg, bd):
    B, C, _ = ctx.shape
    const2 = lambda b: (0, 0)
    kv_block = (3 * D_A + D_B) // (2 * D_B)
    kv_spec = pl.BlockSpec((CTX_TM, D_B), lambda b: (b, 0))
    kv_shape = jax.ShapeDtypeStruct((B * C, D_B), jnp.bfloat16)
    kc, vc = pl.pallas_call(
        _ctx_kernel,
        grid=(B * C // CTX_TM,),
        in_specs=[
            pl.BlockSpec((CTX_TM, D_MODEL), lambda b: (b, 0)),
            pl.BlockSpec((1, D_MODEL), const2),
            pl.BlockSpec((1, D_MODEL), const2),
            pl.BlockSpec((1, D_MODEL), const2),
            pl.BlockSpec((D_MODEL, 2 * D_B), lambda b: (0, kv_block)),
            pl.BlockSpec((1, D_B), const2),
            pl.BlockSpec((MXU_DIM, MXU_DIM), const2),
        ],
        out_specs=[kv_spec, kv_spec],
        out_shape=[kv_shape, kv_shape],
        compiler_params=pltpu.CompilerParams(
            dimension_semantics=("arbitrary",), vmem_limit_bytes=VMEM_LIMIT),
        name="ctx_kv",
    )(ctx.reshape(B * C, D_MODEL), cshift, cscale, ng, w_in, kg, bd)
    return kc.reshape(B, C, D_B), vc.reshape(B, C, D_B)


def _rpb_kernel(rpb_ref, win_ref, o_ref):
    n_heads, n_pairs = o_ref.shape[:2]
    n_dr = n_pairs + 1
    blk = o_ref.shape[2:]
    in_win = win_ref[...] != 0
    left_half = lax.broadcasted_iota(jnp.int32, blk, 1) < GRID_W
    first = LANES - (WIN_C - 1)
    for h in range(n_heads):
        for dr in range(n_pairs):
            row = h * n_dr + dr
            lo = pltpu.roll(jnp.broadcast_to(rpb_ref[row:row + 1, :], blk), first, 1,
                            stride=1, stride_axis=0)
            hi = pltpu.roll(jnp.broadcast_to(rpb_ref[row + 1:row + 2, :], blk),
                            (first + GRID_W) % LANES, 1, stride=1, stride_axis=0)
            o_ref[h, dr] = jnp.where(in_win, jnp.where(left_half, lo, hi) * LOG2E, NEG_INF)


def _rpb_call(rpb):
    H, n_dr, n_dc = rpb.shape
    cols = np.arange(GRID_W)
    c0 = np.clip(cols - WIN_C // 2, 0, GRID_W - WIN_C)
    in_win = (cols[None, :] >= c0[:, None]) & (cols[None, :] < c0[:, None] + WIN_C)
    win = jnp.asarray(np.tile(in_win, (1, LANES // GRID_W)), jnp.int32)
    rpb2 = jnp.pad(rpb.astype(jnp.float32).reshape(H * n_dr, n_dc), ((0, 0), (0, LANES - n_dc)))
    return pl.pallas_call(
        _rpb_kernel,
        out_shape=jax.ShapeDtypeStruct((H, n_dr - 1, GRID_W, LANES), jnp.float32),
        name="rpb_toeplitz",
    )(rpb2, win)


def _attn_kernel(x_ref, gate_ref, oa_ref, gb_ref, q_ref, k_ref, v_ref, kc_ref, vc_ref,
                 tab_ref, wout_ref, o_ref, mix_ref, s_ref, p_ref, l_ref, wob_ref):
    i = pl.program_id(1)

    @pl.when((pl.program_id(0) == 0) & (i == 0))
    def _():
        wob_ref[...] = wout_ref[...].astype(wob_ref.dtype)

    rows = pl.num_programs(1) * Q_ROWS
    shape = (GROUP_W, GROUP_W)
    row_head = lax.broadcasted_iota(jnp.int32, shape, 0) // HEAD_DIM
    lane_head = lax.broadcasted_iota(jnp.int32, shape, 1) // HEAD_DIM
    own_head = row_head == lane_head

    n_ctx = kc_ref.shape[1]
    groups = range(HEADS // HEAD_GROUP)

    def window(rho):
        r = i * Q_ROWS + rho
        r0 = jnp.clip(r - WIN_R // 2, 0, rows - WIN_R)
        d0 = r0 - r + (WIN_R - 1)
        return pl.multiple_of(r0 * GRID_W, GRID_W), d0

    def qrows(rho):
        return slice(rho * GRID_W, (rho + 1) * GRID_W)

    def scores(rho, slot):
        tok0, d0 = window(rho)
        for g in groups:
            lanes = slice(g * GROUP_W, (g + 1) * GROUP_W)
            qr = q_ref[0, qrows(rho), lanes]
            q_bd = jnp.where(own_head, jnp.concatenate([qr] * HEAD_GROUP, axis=0),
                             jnp.zeros(shape, qr.dtype))
            bias = jnp.concatenate(
                [jnp.concatenate([tab_ref[g * HEAD_GROUP + h, d0 + 2 * t] for t in range(WIN_R // 2)],
                                 axis=1) for h in range(HEAD_GROUP)], axis=0)
            s_ref[slot, g, :, 0:TKW] = lax.dot_general(
                q_bd, k_ref[0, pl.ds(tok0, TKW), lanes], _NT,
                preferred_element_type=jnp.float32) + bias
            s_ref[slot, g, :, TKW:TKW + n_ctx] = lax.dot_general(
                q_bd, kc_ref[0, :, lanes], _NT, preferred_element_type=jnp.float32)

    def softmax(slot):
        for g in groups:
            for c in range(GROUP_W // SOFTMAX_ROWS):
                rs = slice(c * SOFTMAX_ROWS, (c + 1) * SOFTMAX_ROWS)
                s = s_ref[slot, g, rs, :]
                p = jnp.exp2(s - jnp.max(s, axis=-1, keepdims=True))
                l_ref[slot, g, rs, :] = jnp.sum(p, axis=-1, keepdims=True)
                p_ref[slot, g, rs, :] = p.astype(p_ref.dtype)

    def values(rho, slot):
        tok0, _ = window(rho)
        for g in groups:
            lanes = slice(g * GROUP_W, (g + 1) * GROUP_W)
            o = (_dot(p_ref[slot, g, :, 0:TKW], v_ref[0, pl.ds(tok0, TKW), lanes])
                 + _dot(p_ref[slot, g, :, TKW:TKW + n_ctx], vc_ref[0, :, lanes]))
            o = jnp.where(own_head, o / l_ref[slot, g], 0.0)
            og = o[0:GRID_W]
            for h in range(1, HEAD_GROUP):
                og = og + o[h * GRID_W:(h + 1) * GRID_W]
            mix_ref[qrows(rho), D_A + g * GROUP_W:D_A + (g + 1) * GROUP_W] = (
                og * gb_ref[0, qrows(rho), lanes].astype(jnp.float32)).astype(mix_ref.dtype)

    mix_ref[:, 0:D_A] = oa_ref[0]
    scores(0, 0)
    for rho in range(Q_ROWS):
        if rho + 1 < Q_ROWS:
            scores(rho + 1, (rho + 1) % N_SLOTS)
        softmax(rho % N_SLOTS)
        values(rho, rho % N_SLOTS)
        if (rho + 1) % OUT_ROWS == 0:
            ts = slice((rho + 1 - OUT_ROWS) * GRID_W, (rho + 1) * GRID_W)
            o_ref[0, ts] = x_ref[0, ts] + gate_ref[0] * _dot(mix_ref[ts], wob_ref[...])


def _attn_call(x, gate, oa, gb, q, k, v, kc, vc, tab, w_out):
    B, L, _ = x.shape
    C = kc.shape[1]
    n_groups = HEADS // HEAD_GROUP
    tok_spec = lambda width: pl.BlockSpec((1, TQ, width), lambda b, i: (b, i, 0))
    batch_spec = lambda n: pl.BlockSpec((1, n, D_B), lambda b, i: (b, 0, 0))
    return pl.pallas_call(
        _attn_kernel,
        grid=(B, L // TQ),
        in_specs=[
            tok_spec(D_MODEL),
            pl.BlockSpec((1, 1, D_MODEL), lambda b, i: (b, 0, 0)),
            tok_spec(D_A),
            tok_spec(D_B),
            tok_spec(D_B),
            batch_spec(L),
            batch_spec(L),
            batch_spec(C),
            batch_spec(C),
            pl.BlockSpec(tab.shape, lambda b, i: (0, 0, 0, 0)),
            pl.BlockSpec((D_A + D_B, D_MODEL), lambda b, i: (0, 0)),
        ],
        out_specs=tok_spec(D_MODEL),
        out_shape=jax.ShapeDtypeStruct((B, L, D_MODEL), jnp.float32),
        scratch_shapes=[pltpu.VMEM((TQ, D_A + D_B), jnp.bfloat16),
                        pltpu.VMEM((N_SLOTS, n_groups, GROUP_W, TKW + C), jnp.float32),
                        pltpu.VMEM((N_SLOTS, n_groups, GROUP_W, TKW + C), jnp.bfloat16),
                        pltpu.VMEM((N_SLOTS, n_groups, GROUP_W, 1), jnp.float32),
                        pltpu.VMEM((D_A + D_B, D_MODEL), jnp.bfloat16)],
        compiler_params=pltpu.CompilerParams(
            dimension_semantics=("arbitrary", "arbitrary"), vmem_limit_bytes=VMEM_LIMIT),
        name="nbr_attn_out",
    )(x, gate, oa, gb, q, k, v, kc, vc, tab, w_out)


def kernel(x, c, ctx, c_ctx, w_ada, b_ada, norm_g, w_in, sgu_norm_g, w_spatial, b_spatial,
           q_norm_g, k_norm_g, rpb, w_out):
    B, L, D = x.shape
    depth = w_ada.shape[0]
    f32, bf16 = jnp.float32, jnp.bfloat16
    bd = jnp.asarray(np.kron(np.eye(MXU_DIM // HEAD_DIM), np.ones((HEAD_DIM, HEAD_DIM))), bf16)
    cc = jnp.concatenate([c, c_ctx[None, :], jnp.zeros((ADA_ROWS - B - 1, D), f32)], axis=0)
    assert depth == 1
    for layer in range(depth):
        mod = _ada_call(cc, w_ada[layer], b_ada[layer][None, :])
        shift, scale, gate = (mod[:B, j * D:(j + 1) * D].reshape(B, 1, D) for j in range(3))
        cshift, cscale = (mod[B:B + 1, j * D:(j + 1) * D] for j in range(2))

        ng = norm_g[layer][None, :]
        kg = jnp.tile(k_norm_g[layer], HEADS)[None, :]
        qg = jnp.tile(q_norm_g[layer], HEADS)[None, :] * (HEAD_DIM ** -0.5 * LOG2E)
        bs = jnp.repeat(b_spatial[layer].T, D_A // SGU_GROUPS, axis=1)
        oa, gb, q, k, v = _proj_call(
            x, shift, scale, ng, w_in[layer], sgu_norm_g[layer][None, :],
            w_spatial[layer].astype(bf16), bs, qg, kg, bd)
        kc, vc = _ctx_call(ctx, cshift, cscale, ng, w_in[layer], kg, bd)
        tab = _rpb_call(rpb[layer])
        x = _attn_call(x, gate, oa, gb, q, k, v, kc, vc, tab, w_out[layer])
    return x
```

```python
import numpy as np
import jax
import jax.numpy as jnp
from jax import lax
from jax.experimental import pallas as pl
from jax.experimental.pallas import tpu as pltpu

D_MODEL = 1024
GRID_W = 64
D_A = 512
D_B = 512
CHUNK = 128
SGU_GROUPS = 4
HEAD_DIM = 64
HEADS = 8
HEAD_GROUP = 4
GROUP_W = HEAD_GROUP * HEAD_DIM
WIN_R = 8
WIN_C = 16
D_IN = 3 * D_A + 4 * D_B
EPS = 1e-6
NEG_INF = -1e30
LOG2E = float(np.log2(np.e))

LANES = 128
MXU_DIM = 256
VMEM_LIMIT = 56 * 1024 * 1024

TM = 512
Q_ROWS = 8
OUT_ROWS = 4
N_SLOTS = 2
SOFTMAX_ROWS = 32
CTX_TM = 1024
TQ = Q_ROWS * GRID_W
TKW = WIN_R * GRID_W
ADA_ROWS = 16
ADA_BN = 768
T_OA, T_GB, T_Q, TOK_W = 0, D_A, D_A + D_B, D_A + 2 * D_B

_NT = (((1,), (1,)), ((), ()))


def _silu(x):
    return x / (1.0 + jnp.exp(-x))


def _gelu(x):
    return 0.5 * x * (1.0 + lax.erf(x * np.float32(np.sqrt(0.5))))


def _dot(a, b):
    return jnp.dot(a, b, preferred_element_type=jnp.float32)


def _ada_kernel(c_ref, w_ref, b_ref, o_ref):
    a, w = _silu(c_ref[...]), w_ref[...]
    a_hi, w_hi = a.astype(jnp.bfloat16), w.astype(jnp.bfloat16)
    a_lo = (a - a_hi.astype(jnp.float32)).astype(jnp.bfloat16)
    w_lo = (w - w_hi.astype(jnp.float32)).astype(jnp.bfloat16)
    o_ref[...] = _dot(a_hi, w_hi) + (_dot(a_lo, w_hi) + _dot(a_hi, w_lo)) + b_ref[...]


def _ada_call(cc, w_ada, b_ada):
    n = w_ada.shape[1]
    return pl.pallas_call(
        _ada_kernel,
        grid=(n // ADA_BN,),
        in_specs=[
            pl.BlockSpec((ADA_ROWS, D_MODEL), lambda j: (0, 0)),
            pl.BlockSpec((D_MODEL, ADA_BN), lambda j: (0, j)),
            pl.BlockSpec((1, ADA_BN), lambda j: (0, j)),
        ],
        out_specs=pl.BlockSpec((ADA_ROWS, ADA_BN), lambda j: (0, j)),
        out_shape=jax.ShapeDtypeStruct((ADA_ROWS, n), jnp.float32),
        compiler_params=pltpu.CompilerParams(dimension_semantics=("arbitrary",)),
        name="ada_params",
    )(cc, w_ada, b_ada)


def _prenorm(x, g, shift, scale):
    ms = jnp.mean(x * x, axis=-1, keepdims=True)
    h = x * lax.rsqrt(ms + EPS) * g
    return (h * (1.0 + scale) + shift).astype(jnp.bfloat16)


def _head_rms(z, bd, gain):
    sq = (z * z).astype(jnp.bfloat16)
    ss = jnp.concatenate(
        [_dot(sq[:, c * MXU_DIM:(c + 1) * MXU_DIM], bd) for c in range(D_B // MXU_DIM)], axis=-1)
    return z * lax.rsqrt(ss * (1.0 / HEAD_DIM) + EPS) * gain


def _proj_kernel(x_ref, shift_ref, scale_ref, ng_ref, w_ref, sg_ref, ws_ref, bs_ref,
                 qg_ref, kg_ref, bd_ref,
                 t_ref, k_ref, v_ref, wb_ref):
    @pl.when((pl.program_id(0) == 0) & (pl.program_id(1) == 0))
    def _():
        for lo in range(0, D_IN, D_A):
            wb_ref[:, lo:lo + D_A] = w_ref[:, lo:lo + D_A].astype(wb_ref.dtype)

    hb = _prenorm(x_ref[0], ng_ref[...], shift_ref[0], scale_ref[0])

    def zcols(lo, width):
        return _dot(hb, wb_ref[:, lo:lo + width])

    bd = bd_ref[...]
    gu = _gelu(zcols(0, D_A))
    t_ref[0, :, T_Q:T_Q + D_B] = _head_rms(zcols(3 * D_A, D_B), bd, qg_ref[...]).astype(t_ref.dtype)
    gv = _gelu(zcols(D_A, D_A))
    k_ref[0] = _head_rms(zcols(3 * D_A + D_B, D_B), bd, kg_ref[...]).astype(k_ref.dtype)
    sa = _silu(zcols(2 * D_A, D_A))
    t_ref[0, :, T_GB:T_GB + D_B] = _silu(zcols(3 * D_A + 3 * D_B, D_B)).astype(t_ref.dtype)
    v_ref[0] = zcols(3 * D_A + 2 * D_B, D_B).astype(v_ref.dtype)

    for g in range(SGU_GROUPS):
        cs = slice(g * LANES, (g + 1) * LANES)
        vg = gv[:, cs]
        ms = jnp.mean(vg * vg, axis=-1, keepdims=True)
        vn = (vg * lax.rsqrt(ms + EPS) * sg_ref[:, cs]).astype(jnp.bfloat16)
        chunks = [slice(c * CHUNK, (c + 1) * CHUNK) for c in range(TM // CHUNK)]
        mixed = _dot(ws_ref[g], jnp.concatenate([vn[rs] for rs in chunks], axis=1))
        for c, rs in enumerate(chunks):
            mixed_c = mixed[:, c * CHUNK:(c + 1) * CHUNK] + bs_ref[:, cs]
            t_ref[0, rs, T_OA + g * LANES:T_OA + (g + 1) * LANES] = (
                gu[rs, cs] * mixed_c * sa[rs, cs]).astype(t_ref.dtype)


def _proj_call(x, shift, scale, ng, w_in, sg, ws, bs, qg, kg, bd):
    B, L, _ = x.shape
    const2 = lambda b, i: (0, 0)
    half_spec = pl.BlockSpec((1, TM, D_A), lambda b, i: (b, i, 0))
    half_shape = jax.ShapeDtypeStruct((B, L, D_A), jnp.bfloat16)
    return pl.pallas_call(
        _proj_kernel,
        grid=(B, L // TM),
        in_specs=[
            pl.BlockSpec((1, TM, D_MODEL), lambda b, i: (b, i, 0)),
            pl.BlockSpec((1, 1, D_MODEL), lambda b, i: (b, 0, 0)),
            pl.BlockSpec((1, 1, D_MODEL), lambda b, i: (b, 0, 0)),
            pl.BlockSpec((1, D_MODEL), const2),
            pl.BlockSpec((D_MODEL, D_IN), const2),
            pl.BlockSpec((1, D_A), const2),
            pl.BlockSpec((SGU_GROUPS, CHUNK, CHUNK), lambda b, i: (0, 0, 0)),
            pl.BlockSpec((CHUNK, D_A), const2),
            pl.BlockSpec((1, D_B), const2),
            pl.BlockSpec((1, D_B), const2),
            pl.BlockSpec((MXU_DIM, MXU_DIM), const2),
        ],
        out_specs=[pl.BlockSpec((1, TM, TOK_W), lambda b, i: (b, i, 0)), half_spec, half_spec],
        out_shape=[jax.ShapeDtypeStruct((B, L, TOK_W), jnp.bfloat16), half_shape, half_shape],
        scratch_shapes=[pltpu.VMEM((D_MODEL, D_IN), jnp.bfloat16)],
        compiler_params=pltpu.CompilerParams(
            dimension_semantics=("arbitrary", "arbitrary"), vmem_limit_bytes=VMEM_LIMIT),
        name="latent_proj",
    )(x, shift, scale, ng, w_in, sg, ws, bs, qg, kg, bd)


def _ctx_kernel(x_ref, shift_ref, scale_ref, ng_ref, w_ref, kg_ref, bd_ref, k_ref, v_ref):
    hb = _prenorm(x_ref[...], ng_ref[...], shift_ref[...], scale_ref[...])
    wk = w_ref[:, 0:D_B].astype(jnp.bfloat16)
    wv = w_ref[:, D_B:2 * D_B].astype(jnp.bfloat16)
    k_ref[...] = _head_rms(_dot(hb, wk), bd_ref[...], kg_ref[...]).astype(k_ref.dtype)
    v_ref[...] = _dot(hb, wv).astype(v_ref.dtype)


def _ctx_call(ctx, cshift, cscale, ng, w_in, kg, bd):
    B, C, _ = ctx.shape
    const2 = lambda b: (0, 0)
    kv_block = (3 * D_A + D_B) // (2 * D_B)
    kv_spec = pl.BlockSpec((CTX_TM, D_B), lambda b: (b, 0))
    kv_shape = jax.ShapeDtypeStruct((B * C, D_B), jnp.bfloat16)
    kc, vc = pl.pallas_call(
        _ctx_kernel,
        grid=(B * C // CTX_TM,),
        in_specs=[
            pl.BlockSpec((CTX_TM, D_MODEL), lambda b: (b, 0)),
            pl.BlockSpec((1, D_MODEL), const2),
            pl.BlockSpec((1, D_MODEL), const2),
            pl.BlockSpec((1, D_MODEL), const2),
            pl.BlockSpec((D_MODEL, 2 * D_B), lambda b: (0, kv_block)),
            pl.BlockSpec((1, D_B), const2),
            pl.BlockSpec((MXU_DIM, MXU_DIM), const2),
        ],
        out_specs=[kv_spec, kv_spec],
        out_shape=[kv_shape, kv_shape],
        compiler_params=pltpu.CompilerParams(
            dimension_semantics=("arbitrary",), vmem_limit_bytes=VMEM_LIMIT),
        name="ctx_kv",
    )(ctx.reshape(B * C, D_MODEL), cshift, cscale, ng, w_in, kg, bd)
    return kc.reshape(B, C, D_B), vc.reshape(B, C, D_B)


def _rpb_kernel(rpb_ref, win_ref, o_ref):
    n_heads, n_pairs = o_ref.shape[:2]
    n_dr = n_pairs + 1
    blk = o_ref.shape[2:]
    in_win = win_ref[...] != 0
    left_half = lax.broadcasted_iota(jnp.int32, blk, 1) < GRID_W
    first = LANES - (WIN_C - 1)
    for h in range(n_heads):
        for dr in range(n_pairs):
            row = h * n_dr + dr
            lo = pltpu.roll(jnp.broadcast_to(rpb_ref[row:row + 1, :], blk), first, 1,
                            stride=1, stride_axis=0)
            hi = pltpu.roll(jnp.broadcast_to(rpb_ref[row + 1:row + 2, :], blk),
                            (first + GRID_W) % LANES, 1, stride=1, stride_axis=0)
            o_ref[h, dr] = jnp.where(in_win, jnp.where(left_half, lo, hi) * LOG2E, NEG_INF)


def _rpb_call(rpb):
    H, n_dr, n_dc = rpb.shape
    cols = np.arange(GRID_W)
    c0 = np.clip(cols - WIN_C // 2, 0, GRID_W - WIN_C)
    in_win = (cols[None, :] >= c0[:, None]) & (cols[None, :] < c0[:, None] + WIN_C)
    win = jnp.asarray(np.tile(in_win, (1, LANES // GRID_W)), jnp.int32)
    rpb2 = jnp.pad(rpb.astype(jnp.float32).reshape(H * n_dr, n_dc), ((0, 0), (0, LANES - n_dc)))
    return pl.pallas_call(
        _rpb_kernel,
        out_shape=jax.ShapeDtypeStruct((H, n_dr - 1, GRID_W, LANES), jnp.float32),
        name="rpb_toeplitz",
    )(rpb2, win)


def _attn_kernel(x_ref, gate_ref, t_ref, k_ref, v_ref, kc_ref, vc_ref,
                 tab_ref, wout_ref, o_ref, mix_ref, s_ref, p_ref, l_ref, wob_ref):
    i = pl.program_id(1)

    @pl.when((pl.program_id(0) == 0) & (i == 0))
    def _():
        wob_ref[...] = wout_ref[...].astype(wob_ref.dtype)

    rows = pl.num_programs(1) * Q_ROWS
    shape = (GROUP_W, GROUP_W)
    row_head = lax.broadcasted_iota(jnp.int32, shape, 0) // HEAD_DIM
    lane_head = lax.broadcasted_iota(jnp.int32, shape, 1) // HEAD_DIM
    own_head = row_head == lane_head

    n_ctx = kc_ref.shape[1]
    groups = range(HEADS // HEAD_GROUP)

    def window(rho):
        r = i * Q_ROWS + rho
        r0 = jnp.clip(r - WIN_R // 2, 0, rows - WIN_R)
        d0 = r0 - r + (WIN_R - 1)
        return pl.multiple_of(r0 * GRID_W, GRID_W), d0

    def qrows(rho):
        return slice(rho * GRID_W, (rho + 1) * GRID_W)

    def scores(rho, slot):
        tok0, d0 = window(rho)
        for g in groups:
            lanes = slice(g * GROUP_W, (g + 1) * GROUP_W)
            qr = t_ref[0, qrows(rho), T_Q + g * GROUP_W:T_Q + (g + 1) * GROUP_W]
            q_bd = jnp.where(own_head, jnp.concatenate([qr] * HEAD_GROUP, axis=0),
                             jnp.zeros(shape, qr.dtype))
            bias = jnp.concatenate(
                [jnp.concatenate([tab_ref[g * HEAD_GROUP + h, d0 + 2 * t] for t in range(WIN_R // 2)],
                                 axis=1) for h in range(HEAD_GROUP)], axis=0)
            s_ref[slot, g, :, 0:TKW] = lax.dot_general(
                q_bd, k_ref[0, pl.ds(tok0, TKW), lanes], _NT,
                preferred_element_type=jnp.float32) + bias
            s_ref[slot, g, :, TKW:TKW + n_ctx] = lax.dot_general(
                q_bd, kc_ref[0, :, lanes], _NT, preferred_element_type=jnp.float32)

    def softmax(slot):
        for g in groups:
            for c in range(GROUP_W // SOFTMAX_ROWS):
                rs = slice(c * SOFTMAX_ROWS, (c + 1) * SOFTMAX_ROWS)
                s = s_ref[slot, g, rs, :]
                p = jnp.exp2(s - jnp.max(s, axis=-1, keepdims=True))
                l_ref[slot, g, rs, :] = jnp.sum(p, axis=-1, keepdims=True)
                p_ref[slot, g, rs, :] = p.astype(p_ref.dtype)

    def values(rho, slot):
        tok0, _ = window(rho)
        for g in groups:
            lanes = slice(g * GROUP_W, (g + 1) * GROUP_W)
            o = (_dot(p_ref[slot, g, :, 0:TKW], v_ref[0, pl.ds(tok0, TKW), lanes])
                 + _dot(p_ref[slot, g, :, TKW:TKW + n_ctx], vc_ref[0, :, lanes]))
            o = jnp.where(own_head, o / l_ref[slot, g], 0.0)
            og = o[0:GRID_W]
            for h in range(1, HEAD_GROUP):
                og = og + o[h * GRID_W:(h + 1) * GRID_W]
            mix_ref[qrows(rho), D_A + g * GROUP_W:D_A + (g + 1) * GROUP_W] = (
                og * t_ref[0, qrows(rho), T_GB + g * GROUP_W:T_GB + (g + 1) * GROUP_W].astype(jnp.float32)
            ).astype(mix_ref.dtype)

    mix_ref[:, 0:D_A] = t_ref[0, :, T_OA:T_OA + D_A]
    scores(0, 0)
    for rho in range(Q_ROWS):
        if rho + 1 < Q_ROWS:
            scores(rho + 1, (rho + 1) % N_SLOTS)
        softmax(rho % N_SLOTS)
        values(rho, rho % N_SLOTS)
        if (rho + 1) % OUT_ROWS == 0:
            ts = slice((rho + 1 - OUT_ROWS) * GRID_W, (rho + 1) * GRID_W)
            o_ref[0, ts] = x_ref[0, ts] + gate_ref[0] * _dot(mix_ref[ts], wob_ref[...])


def _attn_call(x, gate, tok, k, v, kc, vc, tab, w_out):
    B, L, _ = x.shape
    C = kc.shape[1]
    n_groups = HEADS // HEAD_GROUP
    tok_spec = lambda width: pl.BlockSpec((1, TQ, width), lambda b, i: (b, i, 0))
    batch_spec = lambda n: pl.BlockSpec((1, n, D_B), lambda b, i: (b, 0, 0))
    return pl.pallas_call(
        _attn_kernel,
        grid=(B, L // TQ),
        in_specs=[
            tok_spec(D_MODEL),
            pl.BlockSpec((1, 1, D_MODEL), lambda b, i: (b, 0, 0)),
            tok_spec(TOK_W),
            batch_spec(L),
            batch_spec(L),
            batch_spec(C),
            batch_spec(C),
            pl.BlockSpec(tab.shape, lambda b, i: (0, 0, 0, 0)),
            pl.BlockSpec((D_A + D_B, D_MODEL), lambda b, i: (0, 0)),
        ],
        out_specs=tok_spec(D_MODEL),
        out_shape=jax.ShapeDtypeStruct((B, L, D_MODEL), jnp.float32),
        scratch_shapes=[pltpu.VMEM((TQ, D_A + D_B), jnp.bfloat16),
                        pltpu.VMEM((N_SLOTS, n_groups, GROUP_W, TKW + C), jnp.float32),
                        pltpu.VMEM((N_SLOTS, n_groups, GROUP_W, TKW + C), jnp.bfloat16),
                        pltpu.VMEM((N_SLOTS, n_groups, GROUP_W, 1), jnp.float32),
                        pltpu.VMEM((D_A + D_B, D_MODEL), jnp.bfloat16)],
        compiler_params=pltpu.CompilerParams(
            dimension_semantics=("arbitrary", "arbitrary"), vmem_limit_bytes=VMEM_LIMIT),
        name="nbr_attn_out",
    )(x, gate, tok, k, v, kc, vc, tab, w_out)


def kernel(x, c, ctx, c_ctx, w_ada, b_ada, norm_g, w_in, sgu_norm_g, w_spatial, b_spatial,
           q_norm_g, k_norm_g, rpb, w_out):
    B, L, D = x.shape
    depth = w_ada.shape[0]
    f32, bf16 = jnp.float32, jnp.bfloat16
    bd = jnp.asarray(np.kron(np.eye(MXU_DIM // HEAD_DIM), np.ones((HEAD_DIM, HEAD_DIM))), bf16)
    cc = jnp.concatenate([c, c_ctx[None, :], jnp.zeros((ADA_ROWS - B - 1, D), f32)], axis=0)
    assert depth == 1
    for layer in range(depth):
        mod = _ada_call(cc, w_ada[layer], b_ada[layer][None, :])
        shift, scale, gate = (mod[:B, j * D:(j + 1) * D].reshape(B, 1, D) for j in range(3))
        cshift, cscale = (mod[B:B + 1, j * D:(j + 1) * D] for j in range(2))

        ng = norm_g[layer][None, :]
        kg = jnp.tile(k_norm_g[layer], HEADS)[None, :]
        qg = jnp.tile(q_norm_g[layer], HEADS)[None, :] * (HEAD_DIM ** -0.5 * LOG2E)
        bs = jnp.repeat(b_spatial[layer].T, D_A // SGU_GROUPS, axis=1)
        tok, k, v = _proj_call(
            x, shift, scale, ng, w_in[layer], sgu_norm_g[layer][None, :],
            w_spatial[layer].astype(bf16), bs, qg, kg, bd)
        kc, vc = _ctx_call(ctx, cshift, cscale, ng, w_in[layer], kg, bd)
        tab = _rpb_call(rpb[layer])
        x = _attn_call(x, gate, tok, k, v, kc, vc, tab, w_out[layer])
    return x
```

```python
import numpy as np
import jax
import jax.numpy as jnp
from jax import lax
from jax.experimental import pallas as pl
from jax.experimental.pallas import tpu as pltpu

D_MODEL = 1024
GRID_W = 64
D_A = 512
D_B = 512
CHUNK = 128
SGU_GROUPS = 4
HEAD_DIM = 64
HEADS = 8
HEAD_GROUP = 4
GROUP_W = HEAD_GROUP * HEAD_DIM
WIN_R = 8
WIN_C = 16
D_IN = 3 * D_A + 4 * D_B
EPS = 1e-6
NEG_INF = -1e30
LOG2E = float(np.log2(np.e))

LANES = 128
MXU_DIM = 256
VMEM_LIMIT = 56 * 1024 * 1024

TM = 1024
SUB_TM = 512
Q_ROWS = 8
OUT_ROWS = 4
N_SLOTS = 2
SOFTMAX_ROWS = 32
CTX_TM = 1024
TQ = Q_ROWS * GRID_W
TKW = WIN_R * GRID_W
ADA_ROWS = 16
ADA_BN = 768
T_OA, T_GB, T_Q, TOK_W = 0, D_A, D_A + D_B, D_A + 2 * D_B

_NT = (((1,), (1,)), ((), ()))


def _silu(x):
    return x / (1.0 + jnp.exp(-x))


def _gelu(x):
    return 0.5 * x * (1.0 + lax.erf(x * np.float32(np.sqrt(0.5))))


def _dot(a, b):
    return jnp.dot(a, b, preferred_element_type=jnp.float32)


def _ada_kernel(c_ref, w_ref, b_ref, o_ref):
    a, w = _silu(c_ref[...]), w_ref[...]
    a_hi, w_hi = a.astype(jnp.bfloat16), w.astype(jnp.bfloat16)
    a_lo = (a - a_hi.astype(jnp.float32)).astype(jnp.bfloat16)
    w_lo = (w - w_hi.astype(jnp.float32)).astype(jnp.bfloat16)
    o_ref[...] = _dot(a_hi, w_hi) + (_dot(a_lo, w_hi) + _dot(a_hi, w_lo)) + b_ref[...]


def _ada_call(cc, w_ada, b_ada):
    n = w_ada.shape[1]
    return pl.pallas_call(
        _ada_kernel,
        grid=(n // ADA_BN,),
        in_specs=[
            pl.BlockSpec((ADA_ROWS, D_MODEL), lambda j: (0, 0)),
            pl.BlockSpec((D_MODEL, ADA_BN), lambda j: (0, j)),
            pl.BlockSpec((1, ADA_BN), lambda j: (0, j)),
        ],
        out_specs=pl.BlockSpec((ADA_ROWS, ADA_BN), lambda j: (0, j)),
        out_shape=jax.ShapeDtypeStruct((ADA_ROWS, n), jnp.float32),
        compiler_params=pltpu.CompilerParams(dimension_semantics=("arbitrary",)),
        name="ada_params",
    )(cc, w_ada, b_ada)


def _prenorm(x, g, shift, scale):
    ms = jnp.mean(x * x, axis=-1, keepdims=True)
    h = x * lax.rsqrt(ms + EPS) * g
    return (h * (1.0 + scale) + shift).astype(jnp.bfloat16)


def _head_rms(z, bd, gain):
    sq = (z * z).astype(jnp.bfloat16)
    ss = jnp.concatenate(
        [_dot(sq[:, c * MXU_DIM:(c + 1) * MXU_DIM], bd) for c in range(D_B // MXU_DIM)], axis=-1)
    return z * lax.rsqrt(ss * (1.0 / HEAD_DIM) + EPS) * gain


def _proj_kernel(x_ref, shift_ref, scale_ref, ng_ref, w_ref, sg_ref, ws_ref, bs_ref,
                 qg_ref, kg_ref, bd_ref,
                 t_ref, k_ref, v_ref, wb_ref):
    @pl.when((pl.program_id(0) == 0) & (pl.program_id(1) == 0))
    def _():
        for lo in range(0, D_IN, D_A):
            wb_ref[:, lo:lo + D_A] = w_ref[:, lo:lo + D_A].astype(wb_ref.dtype)

    for sub in range(TM // SUB_TM):
        _proj_rows(slice(sub * SUB_TM, (sub + 1) * SUB_TM), x_ref, shift_ref, scale_ref, ng_ref,
                   wb_ref, sg_ref, ws_ref, bs_ref, qg_ref, kg_ref, bd_ref, t_ref, k_ref, v_ref)


def _proj_rows(rows, x_ref, shift_ref, scale_ref, ng_ref, wb_ref, sg_ref, ws_ref, bs_ref,
               qg_ref, kg_ref, bd_ref, t_ref, k_ref, v_ref):
    hb = _prenorm(x_ref[0, rows], ng_ref[...], shift_ref[0], scale_ref[0])

    def zcols(lo, width):
        return _dot(hb, wb_ref[:, lo:lo + width])

    bd = bd_ref[...]
    gu = _gelu(zcols(0, D_A))
    t_ref[0, rows, T_Q:T_Q + D_B] = _head_rms(zcols(3 * D_A, D_B), bd, qg_ref[...]).astype(t_ref.dtype)
    gv = _gelu(zcols(D_A, D_A))
    k_ref[0, rows] = _head_rms(zcols(3 * D_A + D_B, D_B), bd, kg_ref[...]).astype(k_ref.dtype)
    sa = _silu(zcols(2 * D_A, D_A))
    t_ref[0, rows, T_GB:T_GB + D_B] = _silu(zcols(3 * D_A + 3 * D_B, D_B)).astype(t_ref.dtype)
    v_ref[0, rows] = zcols(3 * D_A + 2 * D_B, D_B).astype(v_ref.dtype)

    for g in range(SGU_GROUPS):
        cs = slice(g * LANES, (g + 1) * LANES)
        vg = gv[:, cs]
        ms = jnp.mean(vg * vg, axis=-1, keepdims=True)
        vn = (vg * lax.rsqrt(ms + EPS) * sg_ref[:, cs]).astype(jnp.bfloat16)
        chunks = [slice(c * CHUNK, (c + 1) * CHUNK) for c in range(SUB_TM // CHUNK)]
        mixed = _dot(ws_ref[g], jnp.concatenate([vn[rs] for rs in chunks], axis=1))
        for c, rs in enumerate(chunks):
            mixed_c = mixed[:, c * CHUNK:(c + 1) * CHUNK] + bs_ref[:, cs]
            out_rows = slice(rows.start + rs.start, rows.start + rs.stop)
            t_ref[0, out_rows, T_OA + g * LANES:T_OA + (g + 1) * LANES] = (
                gu[rs, cs] * mixed_c * sa[rs, cs]).astype(t_ref.dtype)


def _proj_call(x, shift, scale, ng, w_in, sg, ws, bs, qg, kg, bd):
    B, L, _ = x.shape
    const2 = lambda b, i: (0, 0)
    half_spec = pl.BlockSpec((1, TM, D_A), lambda b, i: (b, i, 0))
    half_shape = jax.ShapeDtypeStruct((B, L, D_A), jnp.bfloat16)
    return pl.pallas_call(
        _proj_kernel,
        grid=(B, L // TM),
        in_specs=[
            pl.BlockSpec((1, TM, D_MODEL), lambda b, i: (b, i, 0)),
            pl.BlockSpec((1, 1, D_MODEL), lambda b, i: (b, 0, 0)),
            pl.BlockSpec((1, 1, D_MODEL), lambda b, i: (b, 0, 0)),
            pl.BlockSpec((1, D_MODEL), const2),
            pl.BlockSpec((D_MODEL, D_IN), const2),
            pl.BlockSpec((1, D_A), const2),
            pl.BlockSpec((SGU_GROUPS, CHUNK, CHUNK), lambda b, i: (0, 0, 0)),
            pl.BlockSpec((CHUNK, D_A), const2),
            pl.BlockSpec((1, D_B), const2),
            pl.BlockSpec((1, D_B), const2),
            pl.BlockSpec((MXU_DIM, MXU_DIM), const2),
        ],
        out_specs=[pl.BlockSpec((1, TM, TOK_W), lambda b, i: (b, i, 0)), half_spec, half_spec],
        out_shape=[jax.ShapeDtypeStruct((B, L, TOK_W), jnp.bfloat16), half_shape, half_shape],
        scratch_shapes=[pltpu.VMEM((D_MODEL, D_IN), jnp.bfloat16)],
        compiler_params=pltpu.CompilerParams(
            dimension_semantics=("arbitrary", "arbitrary"), vmem_limit_bytes=VMEM_LIMIT),
        name="latent_proj",
    )(x, shift, scale, ng, w_in, sg, ws, bs, qg, kg, bd)


def _ctx_kernel(x_ref, shift_ref, scale_ref, ng_ref, w_ref, kg_ref, bd_ref, k_ref, v_ref):
    hb = _prenorm(x_ref[...], ng_ref[...], shift_ref[...], scale_ref[...])
    wk = w_ref[:, 0:D_B].astype(jnp.bfloat16)
    wv = w_ref[:, D_B:2 * D_B].astype(jnp.bfloat16)
    k_ref[...] = _head_rms(_dot(hb, wk), bd_ref[...], kg_ref[...]).astype(k_ref.dtype)
    v_ref[...] = _dot(hb, wv).astype(v_ref.dtype)


def _ctx_call(ctx, cshift, cscale, ng, w_in, kg, bd):
    B, C, _ = ctx.shape
    const2 = lambda b: (0, 0)
    kv_block = (3 * D_A + D_B) // (2 * D_B)
    kv_spec = pl.BlockSpec((CTX_TM, D_B), lambda b: (b, 0))
    kv_shape = jax.ShapeDtypeStruct((B * C, D_B), jnp.bfloat16)
    kc, vc = pl.pallas_call(
        _ctx_kernel,
        grid=(B * C // CTX_TM,),
        in_specs=[
            pl.BlockSpec((CTX_TM, D_MODEL), lambda b: (b, 0)),
            pl.BlockSpec((1, D_MODEL), const2),
            pl.BlockSpec((1, D_MODEL), const2),
            pl.BlockSpec((1, D_MODEL), const2),
            pl.BlockSpec((D_MODEL, 2 * D_B), lambda b: (0, kv_block)),
            pl.BlockSpec((1, D_B), const2),
            pl.BlockSpec((MXU_DIM, MXU_DIM), const2),
        ],
        out_specs=[kv_spec, kv_spec],
        out_shape=[kv_shape, kv_shape],
        compiler_params=pltpu.CompilerParams(
            dimension_semantics=("arbitrary",), vmem_limit_bytes=VMEM_LIMIT),
        name="ctx_kv",
    )(ctx.reshape(B * C, D_MODEL), cshift, cscale, ng, w_in, kg, bd)
    return kc.reshape(B, C, D_B), vc.reshape(B, C, D_B)


def _rpb_kernel(rpb_ref, win_ref, o_ref):
    n_heads, n_pairs = o_ref.shape[:2]
    n_dr = n_pairs + 1
    blk = o_ref.shape[2:]
    in_win = win_ref[...] != 0
    left_half = lax.broadcasted_iota(jnp.int32, blk, 1) < GRID_W
    first = LANES - (WIN_C - 1)
    for h in range(n_heads):
        for dr in range(n_pairs):
            row = h * n_dr + dr
            lo = pltpu.roll(jnp.broadcast_to(rpb_ref[row:row + 1, :], blk), first, 1,
                            stride=1, stride_axis=0)
            hi = pltpu.roll(jnp.broadcast_to(rpb_ref[row + 1:row + 2, :], blk),
                            (first + GRID_W) % LANES, 1, stride=1, stride_axis=0)
            o_ref[h, dr] = jnp.where(in_win, jnp.where(left_half, lo, hi) * LOG2E, NEG_INF)


def _rpb_call(rpb):
    H, n_dr, n_dc = rpb.shape
    cols = np.arange(GRID_W)
    c0 = np.clip(cols - WIN_C // 2, 0, GRID_W - WIN_C)
    in_win = (cols[None, :] >= c0[:, None]) & (cols[None, :] < c0[:, None] + WIN_C)
    win = jnp.asarray(np.tile(in_win, (1, LANES // GRID_W)), jnp.int32)
    rpb2 = jnp.pad(rpb.astype(jnp.float32).reshape(H * n_dr, n_dc), ((0, 0), (0, LANES - n_dc)))
    return pl.pallas_call(
        _rpb_kernel,
        out_shape=jax.ShapeDtypeStruct((H, n_dr - 1, GRID_W, LANES), jnp.float32),
        name="rpb_toeplitz",
    )(rpb2, win)


def _attn_kernel(x_ref, gate_ref, t_ref, k_ref, v_ref, kc_ref, vc_ref,
                 tab_ref, wout_ref, o_ref, mix_ref, s_ref, p_ref, l_ref, wob_ref):
    i = pl.program_id(1)

    @pl.when((pl.program_id(0) == 0) & (i == 0))
    def _():
        wob_ref[...] = wout_ref[...].astype(wob_ref.dtype)

    rows = pl.num_programs(1) * Q_ROWS
    shape = (GROUP_W, GROUP_W)
    row_head = lax.broadcasted_iota(jnp.int32, shape, 0) // HEAD_DIM
    lane_head = lax.broadcasted_iota(jnp.int32, shape, 1) // HEAD_DIM
    own_head = row_head == lane_head

    n_ctx = kc_ref.shape[1]
    groups = range(HEADS // HEAD_GROUP)

    def window(rho):
        r = i * Q_ROWS + rho
        r0 = jnp.clip(r - WIN_R // 2, 0, rows - WIN_R)
        d0 = r0 - r + (WIN_R - 1)
        return pl.multiple_of(r0 * GRID_W, GRID_W), d0

    def qrows(rho):
        return slice(rho * GRID_W, (rho + 1) * GRID_W)

    def scores(rho, slot):
        tok0, d0 = window(rho)
        for g in groups:
            lanes = slice(g * GROUP_W, (g + 1) * GROUP_W)
            qr = t_ref[0, qrows(rho), T_Q + g * GROUP_W:T_Q + (g + 1) * GROUP_W]
            q_bd = jnp.where(own_head, jnp.concatenate([qr] * HEAD_GROUP, axis=0),
                             jnp.zeros(shape, qr.dtype))
            bias = jnp.concatenate(
                [jnp.concatenate([tab_ref[g * HEAD_GROUP + h, d0 + 2 * t] for t in range(WIN_R // 2)],
                                 axis=1) for h in range(HEAD_GROUP)], axis=0)
            s_ref[slot, g, :, 0:TKW] = lax.dot_general(
                q_bd, k_ref[0, pl.ds(tok0, TKW), lanes], _NT,
                preferred_element_type=jnp.float32) + bias
            s_ref[slot, g, :, TKW:TKW + n_ctx] = lax.dot_general(
                q_bd, kc_ref[0, :, lanes], _NT, preferred_element_type=jnp.float32)

    def softmax(slot):
        for g in groups:
            for c in range(GROUP_W // SOFTMAX_ROWS):
                rs = slice(c * SOFTMAX_ROWS, (c + 1) * SOFTMAX_ROWS)
                s = s_ref[slot, g, rs, :]
                p = jnp.exp2(s - jnp.max(s, axis=-1, keepdims=True))
                l_ref[slot, g, rs, :] = jnp.sum(p, axis=-1, keepdims=True)
                p_ref[slot, g, rs, :] = p.astype(p_ref.dtype)

    def values(rho, slot):
        tok0, _ = window(rho)
        for g in groups:
            lanes = slice(g * GROUP_W, (g + 1) * GROUP_W)
            o = (_dot(p_ref[slot, g, :, 0:TKW], v_ref[0, pl.ds(tok0, TKW), lanes])
                 + _dot(p_ref[slot, g, :, TKW:TKW + n_ctx], vc_ref[0, :, lanes]))
            o = jnp.where(own_head, o / l_ref[slot, g], 0.0)
            og = o[0:GRID_W]
            for h in range(1, HEAD_GROUP):
                og = og + o[h * GRID_W:(h + 1) * GRID_W]
            mix_ref[qrows(rho), D_A + g * GROUP_W:D_A + (g + 1) * GROUP_W] = (
                og * t_ref[0, qrows(rho), T_GB + g * GROUP_W:T_GB + (g + 1) * GROUP_W].astype(jnp.float32)
            ).astype(mix_ref.dtype)

    mix_ref[:, 0:D_A] = t_ref[0, :, T_OA:T_OA + D_A]
    scores(0, 0)
    for rho in range(Q_ROWS):
        if rho + 1 < Q_ROWS:
            scores(rho + 1, (rho + 1) % N_SLOTS)
        softmax(rho % N_SLOTS)
        values(rho, rho % N_SLOTS)
        if (rho + 1) % OUT_ROWS == 0:
            ts = slice((rho + 1 - OUT_ROWS) * GRID_W, (rho + 1) * GRID_W)
            o_ref[0, ts] = x_ref[0, ts] + gate_ref[0] * _dot(mix_ref[ts], wob_ref[...])


def _attn_call(x, gate, tok, k, v, kc, vc, tab, w_out):
    B, L, _ = x.shape
    C = kc.shape[1]
    n_groups = HEADS // HEAD_GROUP
    tok_spec = lambda width: pl.BlockSpec((1, TQ, width), lambda b, i: (b, i, 0))
    batch_spec = lambda n: pl.BlockSpec((1, n, D_B), lambda b, i: (b, 0, 0))
    return pl.pallas_call(
        _attn_kernel,
        grid=(B, L // TQ),
        in_specs=[
            tok_spec(D_MODEL),
            pl.BlockSpec((1, 1, D_MODEL), lambda b, i: (b, 0, 0)),
            tok_spec(TOK_W),
            batch_spec(L),
            batch_spec(L),
            batch_spec(C),
            batch_spec(C),
            pl.BlockSpec(tab.shape, lambda b, i: (0, 0, 0, 0)),
            pl.BlockSpec((D_A + D_B, D_MODEL), lambda b, i: (0, 0)),
        ],
        out_specs=tok_spec(D_MODEL),
        out_shape=jax.ShapeDtypeStruct((B, L, D_MODEL), jnp.float32),
        scratch_shapes=[pltpu.VMEM((TQ, D_A + D_B), jnp.bfloat16),
                        pltpu.VMEM((N_SLOTS, n_groups, GROUP_W, TKW + C), jnp.float32),
                        pltpu.VMEM((N_SLOTS, n_groups, GROUP_W, TKW + C), jnp.bfloat16),
                        pltpu.VMEM((N_SLOTS, n_groups, GROUP_W, 1), jnp.float32),
                        pltpu.VMEM((D_A + D_B, D_MODEL), jnp.bfloat16)],
        compiler_params=pltpu.CompilerParams(
            dimension_semantics=("arbitrary", "arbitrary"), vmem_limit_bytes=VMEM_LIMIT),
        name="nbr_attn_out",
    )(x, gate, tok, k, v, kc, vc, tab, w_out)


def kernel(x, c, ctx, c_ctx, w_ada, b_ada, norm_g, w_in, sgu_norm_g, w_spatial, b_spatial,
           q_norm_g, k_norm_g, rpb, w_out):
    B, L, D = x.shape
    depth = w_ada.shape[0]
    f32, bf16 = jnp.float32, jnp.bfloat16
    bd = jnp.asarray(np.kron(np.eye(MXU_DIM // HEAD_DIM), np.ones((HEAD_DIM, HEAD_DIM))), bf16)
    cc = jnp.concatenate([c, c_ctx[None, :], jnp.zeros((ADA_ROWS - B - 1, D), f32)], axis=0)
    assert depth == 1
    for layer in range(depth):
        mod = _ada_call(cc, w_ada[layer], b_ada[layer][None, :])
        shift, scale, gate = (mod[:B, j * D:(j + 1) * D].reshape(B, 1, D) for j in range(3))
        cshift, cscale = (mod[B:B + 1, j * D:(j + 1) * D] for j in range(2))

        ng = norm_g[layer][None, :]
        kg = jnp.tile(k_norm_g[layer], HEADS)[None, :]
        qg = jnp.tile(q_norm_g[layer], HEADS)[None, :] * (HEAD_DIM ** -0.5 * LOG2E)
        bs = jnp.repeat(b_spatial[layer].T, D_A // SGU_GROUPS, axis=1)
        tok, k, v = _proj_call(
            x, shift, scale, ng, w_in[layer], sgu_norm_g[layer][None, :],
            w_spatial[layer].astype(bf16), bs, qg, kg, bd)
        kc, vc = _ctx_call(ctx, cshift, cscale, ng, w_in[layer], kg, bd)
        tab = _rpb_call(rpb[layer])
        x = _attn_call(x, gate, tok, k, v, kc, vc, tab, w_out[layer])
    return x
```

```python
import numpy as np
import jax
import jax.numpy as jnp
from jax import lax
from jax.experimental import pallas as pl
from jax.experimental.pallas import tpu as pltpu

D_MODEL = 1024
GRID_W = 64
D_A = 512
D_B = 512
CHUNK = 128
SGU_GROUPS = 4
HEAD_DIM = 64
HEADS = 8
HEAD_GROUP = 4
GROUP_W = HEAD_GROUP * HEAD_DIM
WIN_R = 8
WIN_C = 16
D_IN = 3 * D_A + 4 * D_B
EPS = 1e-6
NEG_INF = -1e30
LOG2E = float(np.log2(np.e))

LANES = 128
MXU_DIM = 256
VMEM_LIMIT = 60 * 1024 * 1024

TM = 1024
SUB_TM = 512
Q_ROWS = 16
OUT_ROWS = 4
N_SLOTS = 2
SOFTMAX_ROWS = 32
CTX_TM = 1024
TQ = Q_ROWS * GRID_W
TKW = WIN_R * GRID_W
ADA_ROWS = 16
ADA_BN = 768
T_OA, T_GB, T_Q, TOK_W = 0, D_A, D_A + D_B, D_A + 2 * D_B

_NT = (((1,), (1,)), ((), ()))


def _silu(x):
    return x / (1.0 + jnp.exp(-x))


def _gelu(x):
    return 0.5 * x * (1.0 + lax.erf(x * np.float32(np.sqrt(0.5))))


def _dot(a, b):
    return jnp.dot(a, b, preferred_element_type=jnp.float32)


def _ada_kernel(c_ref, w_ref, b_ref, o_ref):
    a, w = _silu(c_ref[...]), w_ref[...]
    a_hi, w_hi = a.astype(jnp.bfloat16), w.astype(jnp.bfloat16)
    a_lo = (a - a_hi.astype(jnp.float32)).astype(jnp.bfloat16)
    w_lo = (w - w_hi.astype(jnp.float32)).astype(jnp.bfloat16)
    o_ref[...] = _dot(a_hi, w_hi) + (_dot(a_lo, w_hi) + _dot(a_hi, w_lo)) + b_ref[...]


def _ada_call(cc, w_ada, b_ada):
    n = w_ada.shape[1]
    return pl.pallas_call(
        _ada_kernel,
        grid=(n // ADA_BN,),
        in_specs=[
            pl.BlockSpec((ADA_ROWS, D_MODEL), lambda j: (0, 0)),
            pl.BlockSpec((D_MODEL, ADA_BN), lambda j: (0, j)),
            pl.BlockSpec((1, ADA_BN), lambda j: (0, j)),
        ],
        out_specs=pl.BlockSpec((ADA_ROWS, ADA_BN), lambda j: (0, j)),
        out_shape=jax.ShapeDtypeStruct((ADA_ROWS, n), jnp.float32),
        compiler_params=pltpu.CompilerParams(dimension_semantics=("arbitrary",)),
        name="ada_params",
    )(cc, w_ada, b_ada)


def _prenorm(x, g, shift, scale):
    ms = jnp.mean(x * x, axis=-1, keepdims=True)
    h = x * lax.rsqrt(ms + EPS) * g
    return (h * (1.0 + scale) + shift).astype(jnp.bfloat16)


def _head_rms(z, bd, gain):
    sq = (z * z).astype(jnp.bfloat16)
    ss = jnp.concatenate(
        [_dot(sq[:, c * MXU_DIM:(c + 1) * MXU_DIM], bd) for c in range(D_B // MXU_DIM)], axis=-1)
    return z * lax.rsqrt(ss * (1.0 / HEAD_DIM) + EPS) * gain


def _proj_kernel(x_ref, shift_ref, scale_ref, ng_ref, w_ref, sg_ref, ws_ref, bs_ref,
                 qg_ref, kg_ref, bd_ref,
                 t_ref, k_ref, v_ref, wb_ref):
    @pl.when((pl.program_id(0) == 0) & (pl.program_id(1) == 0))
    def _():
        for lo in range(0, D_IN, D_A):
            wb_ref[:, lo:lo + D_A] = w_ref[:, lo:lo + D_A].astype(wb_ref.dtype)

    for sub in range(TM // SUB_TM):
        _proj_rows(slice(sub * SUB_TM, (sub + 1) * SUB_TM), x_ref, shift_ref, scale_ref, ng_ref,
                   wb_ref, sg_ref, ws_ref, bs_ref, qg_ref, kg_ref, bd_ref, t_ref, k_ref, v_ref)


def _proj_rows(rows, x_ref, shift_ref, scale_ref, ng_ref, wb_ref, sg_ref, ws_ref, bs_ref,
               qg_ref, kg_ref, bd_ref, t_ref, k_ref, v_ref):
    hb = _prenorm(x_ref[0, rows], ng_ref[...], shift_ref[0], scale_ref[0])

    def zcols(lo, width):
        return _dot(hb, wb_ref[:, lo:lo + width])

    bd = bd_ref[...]
    gu = _gelu(zcols(0, D_A))
    t_ref[0, rows, T_Q:T_Q + D_B] = _head_rms(zcols(3 * D_A, D_B), bd, qg_ref[...]).astype(t_ref.dtype)
    gv = _gelu(zcols(D_A, D_A))
    k_ref[0, rows] = _head_rms(zcols(3 * D_A + D_B, D_B), bd, kg_ref[...]).astype(k_ref.dtype)
    sa = _silu(zcols(2 * D_A, D_A))
    t_ref[0, rows, T_GB:T_GB + D_B] = _silu(zcols(3 * D_A + 3 * D_B, D_B)).astype(t_ref.dtype)
    v_ref[0, rows] = zcols(3 * D_A + 2 * D_B, D_B).astype(v_ref.dtype)

    for g in range(SGU_GROUPS):
        cs = slice(g * LANES, (g + 1) * LANES)
        vg = gv[:, cs]
        ms = jnp.mean(vg * vg, axis=-1, keepdims=True)
        vn = (vg * lax.rsqrt(ms + EPS) * sg_ref[:, cs]).astype(jnp.bfloat16)
        chunks = [slice(c * CHUNK, (c + 1) * CHUNK) for c in range(SUB_TM // CHUNK)]
        mixed = _dot(ws_ref[g], jnp.concatenate([vn[rs] for rs in chunks], axis=1))
        for c, rs in enumerate(chunks):
            mixed_c = mixed[:, c * CHUNK:(c + 1) * CHUNK] + bs_ref[:, cs]
            out_rows = slice(rows.start + rs.start, rows.start + rs.stop)
            t_ref[0, out_rows, T_OA + g * LANES:T_OA + (g + 1) * LANES] = (
                gu[rs, cs] * mixed_c * sa[rs, cs]).astype(t_ref.dtype)


def _proj_call(x, shift, scale, ng, w_in, sg, ws, bs, qg, kg, bd):
    B, L, _ = x.shape
    const2 = lambda b, i: (0, 0)
    half_spec = pl.BlockSpec((1, TM, D_A), lambda b, i: (b, i, 0))
    half_shape = jax.ShapeDtypeStruct((B, L, D_A), jnp.bfloat16)
    return pl.pallas_call(
        _proj_kernel,
        grid=(B, L // TM),
        in_specs=[
            pl.BlockSpec((1, TM, D_MODEL), lambda b, i: (b, i, 0)),
            pl.BlockSpec((1, 1, D_MODEL), lambda b, i: (b, 0, 0)),
            pl.BlockSpec((1, 1, D_MODEL), lambda b, i: (b, 0, 0)),
            pl.BlockSpec((1, D_MODEL), const2),
            pl.BlockSpec((D_MODEL, D_IN), const2),
            pl.BlockSpec((1, D_A), const2),
            pl.BlockSpec((SGU_GROUPS, CHUNK, CHUNK), lambda b, i: (0, 0, 0)),
            pl.BlockSpec((CHUNK, D_A), const2),
            pl.BlockSpec((1, D_B), const2),
            pl.BlockSpec((1, D_B), const2),
            pl.BlockSpec((MXU_DIM, MXU_DIM), const2),
        ],
        out_specs=[pl.BlockSpec((1, TM, TOK_W), lambda b, i: (b, i, 0)), half_spec, half_spec],
        out_shape=[jax.ShapeDtypeStruct((B, L, TOK_W), jnp.bfloat16), half_shape, half_shape],
        scratch_shapes=[pltpu.VMEM((D_MODEL, D_IN), jnp.bfloat16)],
        compiler_params=pltpu.CompilerParams(
            dimension_semantics=("arbitrary", "arbitrary"), vmem_limit_bytes=VMEM_LIMIT),
        name="latent_proj",
    )(x, shift, scale, ng, w_in, sg, ws, bs, qg, kg, bd)


def _ctx_kernel(x_ref, shift_ref, scale_ref, ng_ref, w_ref, kg_ref, bd_ref, k_ref, v_ref):
    hb = _prenorm(x_ref[...], ng_ref[...], shift_ref[...], scale_ref[...])
    wk = w_ref[:, 0:D_B].astype(jnp.bfloat16)
    wv = w_ref[:, D_B:2 * D_B].astype(jnp.bfloat16)
    k_ref[...] = _head_rms(_dot(hb, wk), bd_ref[...], kg_ref[...]).astype(k_ref.dtype)
    v_ref[...] = _dot(hb, wv).astype(v_ref.dtype)


def _ctx_call(ctx, cshift, cscale, ng, w_in, kg, bd):
    B, C, _ = ctx.shape
    const2 = lambda b: (0, 0)
    kv_block = (3 * D_A + D_B) // (2 * D_B)
    kv_spec = pl.BlockSpec((CTX_TM, D_B), lambda b: (b, 0))
    kv_shape = jax.ShapeDtypeStruct((B * C, D_B), jnp.bfloat16)
    kc, vc = pl.pallas_call(
        _ctx_kernel,
        grid=(B * C // CTX_TM,),
        in_specs=[
            pl.BlockSpec((CTX_TM, D_MODEL), lambda b: (b, 0)),
            pl.BlockSpec((1, D_MODEL), const2),
            pl.BlockSpec((1, D_MODEL), const2),
            pl.BlockSpec((1, D_MODEL), const2),
            pl.BlockSpec((D_MODEL, 2 * D_B), lambda b: (0, kv_block)),
            pl.BlockSpec((1, D_B), const2),
            pl.BlockSpec((MXU_DIM, MXU_DIM), const2),
        ],
        out_specs=[kv_spec, kv_spec],
        out_shape=[kv_shape, kv_shape],
        compiler_params=pltpu.CompilerParams(
            dimension_semantics=("arbitrary",), vmem_limit_bytes=VMEM_LIMIT),
        name="ctx_kv",
    )(ctx.reshape(B * C, D_MODEL), cshift, cscale, ng, w_in, kg, bd)
    return kc.reshape(B, C, D_B), vc.reshape(B, C, D_B)


def _rpb_kernel(rpb_ref, win_ref, o_ref):
    n_heads, n_pairs = o_ref.shape[:2]
    n_dr = n_pairs + 1
    blk = o_ref.shape[2:]
    in_win = win_ref[...] != 0
    left_half = lax.broadcasted_iota(jnp.int32, blk, 1) < GRID_W
    first = LANES - (WIN_C - 1)
    for h in range(n_heads):
        for dr in range(n_pairs):
            row = h * n_dr + dr
            lo = pltpu.roll(jnp.broadcast_to(rpb_ref[row:row + 1, :], blk), first, 1,
                            stride=1, stride_axis=0)
            hi = pltpu.roll(jnp.broadcast_to(rpb_ref[row + 1:row + 2, :], blk),
                            (first + GRID_W) % LANES, 1, stride=1, stride_axis=0)
            o_ref[h, dr] = jnp.where(in_win, jnp.where(left_half, lo, hi) * LOG2E, NEG_INF)


def _rpb_call(rpb):
    H, n_dr, n_dc = rpb.shape
    cols = np.arange(GRID_W)
    c0 = np.clip(cols - WIN_C // 2, 0, GRID_W - WIN_C)
    in_win = (cols[None, :] >= c0[:, None]) & (cols[None, :] < c0[:, None] + WIN_C)
    win = jnp.asarray(np.tile(in_win, (1, LANES // GRID_W)), jnp.int32)
    rpb2 = jnp.pad(rpb.astype(jnp.float32).reshape(H * n_dr, n_dc), ((0, 0), (0, LANES - n_dc)))
    return pl.pallas_call(
        _rpb_kernel,
        out_shape=jax.ShapeDtypeStruct((H, n_dr - 1, GRID_W, LANES), jnp.float32),
        name="rpb_toeplitz",
    )(rpb2, win)


def _attn_kernel(x_ref, gate_ref, t_ref, k_ref, v_ref, kc_ref, vc_ref,
                 tab_ref, wout_ref, o_ref, mix_ref, s_ref, p_ref, l_ref, wob_ref):
    i = pl.program_id(1)

    @pl.when((pl.program_id(0) == 0) & (i == 0))
    def _():
        wob_ref[...] = wout_ref[...].astype(wob_ref.dtype)

    rows = pl.num_programs(1) * Q_ROWS
    shape = (GROUP_W, GROUP_W)
    row_head = lax.broadcasted_iota(jnp.int32, shape, 0) // HEAD_DIM
    lane_head = lax.broadcasted_iota(jnp.int32, shape, 1) // HEAD_DIM
    own_head = row_head == lane_head

    n_ctx = kc_ref.shape[1]
    groups = range(HEADS // HEAD_GROUP)

    def window(rho):
        r = i * Q_ROWS + rho
        r0 = jnp.clip(r - WIN_R // 2, 0, rows - WIN_R)
        d0 = r0 - r + (WIN_R - 1)
        return pl.multiple_of(r0 * GRID_W, GRID_W), d0

    def qrows(rho):
        return slice(rho * GRID_W, (rho + 1) * GRID_W)

    def scores(rho, slot):
        tok0, d0 = window(rho)
        for g in groups:
            lanes = slice(g * GROUP_W, (g + 1) * GROUP_W)
            qr = t_ref[0, qrows(rho), T_Q + g * GROUP_W:T_Q + (g + 1) * GROUP_W]
            q_bd = jnp.where(own_head, jnp.concatenate([qr] * HEAD_GROUP, axis=0),
                             jnp.zeros(shape, qr.dtype))
            bias = jnp.concatenate(
                [jnp.concatenate([tab_ref[g * HEAD_GROUP + h, d0 + 2 * t] for t in range(WIN_R // 2)],
                                 axis=1) for h in range(HEAD_GROUP)], axis=0)
            s_ref[slot, g, :, 0:TKW] = lax.dot_general(
                q_bd, k_ref[0, pl.ds(tok0, TKW), lanes], _NT,
                preferred_element_type=jnp.float32) + bias
            s_ref[slot, g, :, TKW:TKW + n_ctx] = lax.dot_general(
                q_bd, kc_ref[0, :, lanes], _NT, preferred_element_type=jnp.float32)

    def softmax(slot):
        for g in groups:
            for c in range(GROUP_W // SOFTMAX_ROWS):
                rs = slice(c * SOFTMAX_ROWS, (c + 1) * SOFTMAX_ROWS)
                s = s_ref[slot, g, rs, :]
                p = jnp.exp2(s - jnp.max(s, axis=-1, keepdims=True))
                l_ref[slot, g, rs, :] = jnp.sum(p, axis=-1, keepdims=True)
                p_ref[slot, g, rs, :] = p.astype(p_ref.dtype)

    def values(rho, slot):
        tok0, _ = window(rho)
        for g in groups:
            lanes = slice(g * GROUP_W, (g + 1) * GROUP_W)
            o = (_dot(p_ref[slot, g, :, 0:TKW], v_ref[0, pl.ds(tok0, TKW), lanes])
                 + _dot(p_ref[slot, g, :, TKW:TKW + n_ctx], vc_ref[0, :, lanes]))
            o = jnp.where(own_head, o / l_ref[slot, g], 0.0)
            og = o[0:GRID_W]
            for h in range(1, HEAD_GROUP):
                og = og + o[h * GRID_W:(h + 1) * GRID_W]
            mix_ref[qrows(rho), D_A + g * GROUP_W:D_A + (g + 1) * GROUP_W] = (
                og * t_ref[0, qrows(rho), T_GB + g * GROUP_W:T_GB + (g + 1) * GROUP_W].astype(jnp.float32)
            ).astype(mix_ref.dtype)

    mix_ref[:, 0:D_A] = t_ref[0, :, T_OA:T_OA + D_A]
    scores(0, 0)
    for rho in range(Q_ROWS):
        if rho + 1 < Q_ROWS:
            scores(rho + 1, (rho + 1) % N_SLOTS)
        softmax(rho % N_SLOTS)
        values(rho, rho % N_SLOTS)
        if (rho + 1) % OUT_ROWS == 0:
            ts = slice((rho + 1 - OUT_ROWS) * GRID_W, (rho + 1) * GRID_W)
            o_ref[0, ts] = x_ref[0, ts] + gate_ref[0] * _dot(mix_ref[ts], wob_ref[...])


def _attn_call(x, gate, tok, k, v, kc, vc, tab, w_out):
    B, L, _ = x.shape
    C = kc.shape[1]
    n_groups = HEADS // HEAD_GROUP
    tok_spec = lambda width: pl.BlockSpec((1, TQ, width), lambda b, i: (b, i, 0))
    batch_spec = lambda n: pl.BlockSpec((1, n, D_B), lambda b, i: (b, 0, 0))
    return pl.pallas_call(
        _attn_kernel,
        grid=(B, L // TQ),
        in_specs=[
            tok_spec(D_MODEL),
            pl.BlockSpec((1, 1, D_MODEL), lambda b, i: (b, 0, 0)),
            tok_spec(TOK_W),
            batch_spec(L),
            batch_spec(L),
            batch_spec(C),
            batch_spec(C),
            pl.BlockSpec(tab.shape, lambda b, i: (0, 0, 0, 0)),
            pl.BlockSpec((D_A + D_B, D_MODEL), lambda b, i: (0, 0)),
        ],
        out_specs=tok_spec(D_MODEL),
        out_shape=jax.ShapeDtypeStruct((B, L, D_MODEL), jnp.float32),
        scratch_shapes=[pltpu.VMEM((TQ, D_A + D_B), jnp.bfloat16),
                        pltpu.VMEM((N_SLOTS, n_groups, GROUP_W, TKW + C), jnp.float32),
                        pltpu.VMEM((N_SLOTS, n_groups, GROUP_W, TKW + C), jnp.bfloat16),
                        pltpu.VMEM((N_SLOTS, n_groups, GROUP_W, 1), jnp.float32),
                        pltpu.VMEM((D_A + D_B, D_MODEL), jnp.bfloat16)],
        compiler_params=pltpu.CompilerParams(
            dimension_semantics=("arbitrary", "arbitrary"), vmem_limit_bytes=VMEM_LIMIT),
        name="nbr_attn_out",
    )(x, gate, tok, k, v, kc, vc, tab, w_out)


def kernel(x, c, ctx, c_ctx, w_ada, b_ada, norm_g, w_in, sgu_norm_g, w_spatial, b_spatial,
           q_norm_g, k_norm_g, rpb, w_out):
    B, L, D = x.shape
    depth = w_ada.shape[0]
    f32, bf16 = jnp.float32, jnp.bfloat16
    bd = jnp.asarray(np.kron(np.eye(MXU_DIM // HEAD_DIM), np.ones((HEAD_DIM, HEAD_DIM))), bf16)
    cc = jnp.concatenate([c, c_ctx[None, :], jnp.zeros((ADA_ROWS - B - 1, D), f32)], axis=0)
    assert depth == 1
    for layer in range(depth):
        mod = _ada_call(cc, w_ada[layer], b_ada[layer][None, :])
        shift, scale, gate = (mod[:B, j * D:(j + 1) * D].reshape(B, 1, D) for j in range(3))
        cshift, cscale = (mod[B:B + 1, j * D:(j + 1) * D] for j in range(2))

        ng = norm_g[layer][None, :]
        kg = jnp.tile(k_norm_g[layer], HEADS)[None, :]
        qg = jnp.tile(q_norm_g[layer], HEADS)[None, :] * (HEAD_DIM ** -0.5 * LOG2E)
        bs = jnp.repeat(b_spatial[layer].T, D_A // SGU_GROUPS, axis=1)
        tok, k, v = _proj_call(
            x, shift, scale, ng, w_in[layer], sgu_norm_g[layer][None, :],
            w_spatial[layer].astype(bf16), bs, qg, kg, bd)
        kc, vc = _ctx_call(ctx, cshift, cscale, ng, w_in[layer], kg, bd)
        tab = _rpb_call(rpb[layer])
        x = _attn_call(x, gate, tok, k, v, kc, vc, tab, w_out[layer])
    return x
```

```python
import numpy as np
import jax
import jax.numpy as jnp
from jax import lax
from jax.experimental import pallas as pl
from jax.experimental.pallas import tpu as pltpu

D_MODEL = 1024
GRID_W = 64
D_A = 512
D_B = 512
CHUNK = 128
SGU_GROUPS = 4
HEAD_DIM = 64
HEADS = 8
HEAD_GROUP = 4
GROUP_W = HEAD_GROUP * HEAD_DIM
WIN_R = 8
WIN_C = 16
D_IN = 3 * D_A + 4 * D_B
EPS = 1e-6
NEG_INF = -1e30
LOG2E = float(np.log2(np.e))

LANES = 128
MXU_DIM = 256
VMEM_LIMIT = 60 * 1024 * 1024

TM = 1024
SUB_TM = 512
Q_ROWS = 16
OUT_ROWS = 4
N_SLOTS = 4
LEAD = 2
HALO = WIN_R // 2
SOFTMAX_ROWS = 32
CTX_TM = 1024
TQ = Q_ROWS * GRID_W
TKW = WIN_R * GRID_W
ADA_ROWS = 16
ADA_BN = 768
T_OA, T_GB, T_Q, TOK_W = 0, D_A, D_A + D_B, D_A + 2 * D_B

_NT = (((1,), (1,)), ((), ()))


def _silu(x):
    return x / (1.0 + jnp.exp(-x))


def _gelu(x):
    return 0.5 * x * (1.0 + lax.erf(x * np.float32(np.sqrt(0.5))))


def _dot(a, b):
    return jnp.dot(a, b, preferred_element_type=jnp.float32)


def _ada_kernel(c_ref, w_ref, b_ref, o_ref):
    a, w = _silu(c_ref[...]), w_ref[...]
    a_hi, w_hi = a.astype(jnp.bfloat16), w.astype(jnp.bfloat16)
    a_lo = (a - a_hi.astype(jnp.float32)).astype(jnp.bfloat16)
    w_lo = (w - w_hi.astype(jnp.float32)).astype(jnp.bfloat16)
    o_ref[...] = _dot(a_hi, w_hi) + (_dot(a_lo, w_hi) + _dot(a_hi, w_lo)) + b_ref[...]


def _ada_call(cc, w_ada, b_ada):
    n = w_ada.shape[1]
    return pl.pallas_call(
        _ada_kernel,
        grid=(n // ADA_BN,),
        in_specs=[
            pl.BlockSpec((ADA_ROWS, D_MODEL), lambda j: (0, 0)),
            pl.BlockSpec((D_MODEL, ADA_BN), lambda j: (0, j)),
            pl.BlockSpec((1, ADA_BN), lambda j: (0, j)),
        ],
        out_specs=pl.BlockSpec((ADA_ROWS, ADA_BN), lambda j: (0, j)),
        out_shape=jax.ShapeDtypeStruct((ADA_ROWS, n), jnp.float32),
        compiler_params=pltpu.CompilerParams(dimension_semantics=("arbitrary",)),
        name="ada_params",
    )(cc, w_ada, b_ada)


def _prenorm(x, g, shift, scale):
    ms = jnp.mean(x * x, axis=-1, keepdims=True)
    h = x * lax.rsqrt(ms + EPS) * g
    return (h * (1.0 + scale) + shift).astype(jnp.bfloat16)


def _head_rms(z, bd, gain):
    sq = (z * z).astype(jnp.bfloat16)
    ss = jnp.concatenate(
        [_dot(sq[:, c * MXU_DIM:(c + 1) * MXU_DIM], bd) for c in range(D_B // MXU_DIM)], axis=-1)
    return z * lax.rsqrt(ss * (1.0 / HEAD_DIM) + EPS) * gain


def _proj_kernel(x_ref, shift_ref, scale_ref, ng_ref, w_ref, sg_ref, ws_ref, bs_ref,
                 qg_ref, kg_ref, bd_ref,
                 t_ref, k_ref, v_ref, wb_ref):
    @pl.when((pl.program_id(0) == 0) & (pl.program_id(1) == 0))
    def _():
        for lo in range(0, D_IN, D_A):
            wb_ref[:, lo:lo + D_A] = w_ref[:, lo:lo + D_A].astype(wb_ref.dtype)

    for sub in range(TM // SUB_TM):
        _proj_rows(slice(sub * SUB_TM, (sub + 1) * SUB_TM), x_ref, shift_ref, scale_ref, ng_ref,
                   wb_ref, sg_ref, ws_ref, bs_ref, qg_ref, kg_ref, bd_ref, t_ref, k_ref, v_ref)


def _proj_rows(rows, x_ref, shift_ref, scale_ref, ng_ref, wb_ref, sg_ref, ws_ref, bs_ref,
               qg_ref, kg_ref, bd_ref, t_ref, k_ref, v_ref):
    hb = _prenorm(x_ref[0, rows], ng_ref[...], shift_ref[0], scale_ref[0])

    def zcols(lo, width):
        return _dot(hb, wb_ref[:, lo:lo + width])

    bd = bd_ref[...]
    gu = _gelu(zcols(0, D_A))
    t_ref[0, rows, T_Q:T_Q + D_B] = _head_rms(zcols(3 * D_A, D_B), bd, qg_ref[...]).astype(t_ref.dtype)
    gv = _gelu(zcols(D_A, D_A))
    k_ref[0, rows] = _head_rms(zcols(3 * D_A + D_B, D_B), bd, kg_ref[...]).astype(k_ref.dtype)
    sa = _silu(zcols(2 * D_A, D_A))
    t_ref[0, rows, T_GB:T_GB + D_B] = _silu(zcols(3 * D_A + 3 * D_B, D_B)).astype(t_ref.dtype)
    v_ref[0, rows] = zcols(3 * D_A + 2 * D_B, D_B).astype(v_ref.dtype)

    for g in range(SGU_GROUPS):
        cs = slice(g * LANES, (g + 1) * LANES)
        vg = gv[:, cs]
        ms = jnp.mean(vg * vg, axis=-1, keepdims=True)
        vn = (vg * lax.rsqrt(ms + EPS) * sg_ref[:, cs]).astype(jnp.bfloat16)
        chunks = [slice(c * CHUNK, (c + 1) * CHUNK) for c in range(SUB_TM // CHUNK)]
        mixed = _dot(ws_ref[g], jnp.concatenate([vn[rs] for rs in chunks], axis=1))
        for c, rs in enumerate(chunks):
            mixed_c = mixed[:, c * CHUNK:(c + 1) * CHUNK] + bs_ref[:, cs]
            out_rows = slice(rows.start + rs.start, rows.start + rs.stop)
            t_ref[0, out_rows, T_OA + g * LANES:T_OA + (g + 1) * LANES] = (
                gu[rs, cs] * mixed_c * sa[rs, cs]).astype(t_ref.dtype)


def _proj_call(x, shift, scale, ng, w_in, sg, ws, bs, qg, kg, bd):
    B, L, _ = x.shape
    const2 = lambda b, i: (0, 0)
    half_spec = pl.BlockSpec((1, TM, D_A), lambda b, i: (b, i, 0))
    half_shape = jax.ShapeDtypeStruct((B, L, D_A), jnp.bfloat16)
    return pl.pallas_call(
        _proj_kernel,
        grid=(B, L // TM),
        in_specs=[
            pl.BlockSpec((1, TM, D_MODEL), lambda b, i: (b, i, 0)),
            pl.BlockSpec((1, 1, D_MODEL), lambda b, i: (b, 0, 0)),
            pl.BlockSpec((1, 1, D_MODEL), lambda b, i: (b, 0, 0)),
            pl.BlockSpec((1, D_MODEL), const2),
            pl.BlockSpec((D_MODEL, D_IN), const2),
            pl.BlockSpec((1, D_A), const2),
            pl.BlockSpec((SGU_GROUPS, CHUNK, CHUNK), lambda b, i: (0, 0, 0)),
            pl.BlockSpec((CHUNK, D_A), const2),
            pl.BlockSpec((1, D_B), const2),
            pl.BlockSpec((1, D_B), const2),
            pl.BlockSpec((MXU_DIM, MXU_DIM), const2),
        ],
        out_specs=[pl.BlockSpec((1, TM, TOK_W), lambda b, i: (b, i, 0)), half_spec, half_spec],
        out_shape=[jax.ShapeDtypeStruct((B, L, TOK_W), jnp.bfloat16), half_shape, half_shape],
        scratch_shapes=[pltpu.VMEM((D_MODEL, D_IN), jnp.bfloat16)],
        compiler_params=pltpu.CompilerParams(
            dimension_semantics=("arbitrary", "arbitrary"), vmem_limit_bytes=VMEM_LIMIT),
        name="latent_proj",
    )(x, shift, scale, ng, w_in, sg, ws, bs, qg, kg, bd)


def _ctx_kernel(x_ref, shift_ref, scale_ref, ng_ref, w_ref, kg_ref, bd_ref, k_ref, v_ref):
    hb = _prenorm(x_ref[...], ng_ref[...], shift_ref[...], scale_ref[...])
    wk = w_ref[:, 0:D_B].astype(jnp.bfloat16)
    wv = w_ref[:, D_B:2 * D_B].astype(jnp.bfloat16)
    k_ref[...] = _head_rms(_dot(hb, wk), bd_ref[...], kg_ref[...]).astype(k_ref.dtype)
    v_ref[...] = _dot(hb, wv).astype(v_ref.dtype)


def _ctx_call(ctx, cshift, cscale, ng, w_in, kg, bd):
    B, C, _ = ctx.shape
    const2 = lambda b: (0, 0)
    kv_block = (3 * D_A + D_B) // (2 * D_B)
    kv_spec = pl.BlockSpec((CTX_TM, D_B), lambda b: (b, 0))
    kv_shape = jax.ShapeDtypeStruct((B * C, D_B), jnp.bfloat16)
    kc, vc = pl.pallas_call(
        _ctx_kernel,
        grid=(B * C // CTX_TM,),
        in_specs=[
            pl.BlockSpec((CTX_TM, D_MODEL), lambda b: (b, 0)),
            pl.BlockSpec((1, D_MODEL), const2),
            pl.BlockSpec((1, D_MODEL), const2),
            pl.BlockSpec((1, D_MODEL), const2),
            pl.BlockSpec((D_MODEL, 2 * D_B), lambda b: (0, kv_block)),
            pl.BlockSpec((1, D_B), const2),
            pl.BlockSpec((MXU_DIM, MXU_DIM), const2),
        ],
        out_specs=[kv_spec, kv_spec],
        out_shape=[kv_shape, kv_shape],
        compiler_params=pltpu.CompilerParams(
            dimension_semantics=("arbitrary",), vmem_limit_bytes=VMEM_LIMIT),
        name="ctx_kv",
    )(ctx.reshape(B * C, D_MODEL), cshift, cscale, ng, w_in, kg, bd)
    return kc.reshape(B, C, D_B), vc.reshape(B, C, D_B)


def _rpb_kernel(rpb_ref, win_ref, o_ref):
    n_heads, n_pairs = o_ref.shape[:2]
    n_dr = n_pairs + 1
    blk = o_ref.shape[2:]
    in_win = win_ref[...] != 0
    left_half = lax.broadcasted_iota(jnp.int32, blk, 1) < GRID_W
    first = LANES - (WIN_C - 1)
    for h in range(n_heads):
        for dr in range(n_pairs):
            row = h * n_dr + dr
            lo = pltpu.roll(jnp.broadcast_to(rpb_ref[row:row + 1, :], blk), first, 1,
                            stride=1, stride_axis=0)
            hi = pltpu.roll(jnp.broadcast_to(rpb_ref[row + 1:row + 2, :], blk),
                            (first + GRID_W) % LANES, 1, stride=1, stride_axis=0)
            o_ref[h, dr] = jnp.where(in_win, jnp.where(left_half, lo, hi) * LOG2E, NEG_INF)


def _rpb_call(rpb):
    H, n_dr, n_dc = rpb.shape
    cols = np.arange(GRID_W)
    c0 = np.clip(cols - WIN_C // 2, 0, GRID_W - WIN_C)
    in_win = (cols[None, :] >= c0[:, None]) & (cols[None, :] < c0[:, None] + WIN_C)
    win = jnp.asarray(np.tile(in_win, (1, LANES // GRID_W)), jnp.int32)
    rpb2 = jnp.pad(rpb.astype(jnp.float32).reshape(H * n_dr, n_dc), ((0, 0), (0, LANES - n_dc)))
    return pl.pallas_call(
        _rpb_kernel,
        out_shape=jax.ShapeDtypeStruct((H, n_dr - 1, GRID_W, LANES), jnp.float32),
        name="rpb_toeplitz",
    )(rpb2, win)


def _attn_kernel(x_ref, gate_ref, t_ref, kp_ref, kc0_ref, kn_ref, vp_ref, vc0_ref, vn_ref,
                 kc_ref, vc_ref, tab_ref, wout_ref, o_ref, mix_ref, kwin_ref, vwin_ref, *slot_refs):
    i = pl.program_id(1)
    s_refs, p_refs, m_refs, l_refs = (slot_refs[k * N_SLOTS:(k + 1) * N_SLOTS] for k in range(4))
    rows = pl.num_programs(1) * Q_ROWS
    shape = (GROUP_W, GROUP_W)
    row_head = lax.broadcasted_iota(jnp.int32, shape, 0) // HEAD_DIM
    lane_head = lax.broadcasted_iota(jnp.int32, shape, 1) // HEAD_DIM
    own_head = row_head == lane_head
    n_ctx = kc_ref.shape[1]
    groups = range(HEADS // HEAD_GROUP)

    halo = HALO * GRID_W
    for win_ref, prev_ref, cur_ref, next_ref in ((kwin_ref, kp_ref, kc0_ref, kn_ref),
                                                 (vwin_ref, vp_ref, vc0_ref, vn_ref)):
        win_ref[0:halo] = prev_ref[0]
        win_ref[halo:halo + TQ] = cur_ref[0]
        win_ref[halo + TQ:halo + TQ + halo] = next_ref[0]
    mix_ref[:, 0:D_A] = t_ref[0, :, T_OA:T_OA + D_A]

    def window(rho):
        r = i * Q_ROWS + rho
        r0 = jnp.clip(r - WIN_R // 2, 0, rows - WIN_R)
        d0 = r0 - r + (WIN_R - 1)
        local = r0 - (i * Q_ROWS - HALO)
        return pl.multiple_of(local * GRID_W, GRID_W), d0

    def qrows(rho):
        if isinstance(rho, int):
            return slice(rho * GRID_W, (rho + 1) * GRID_W)
        return pl.ds(pl.multiple_of(rho * GRID_W, GRID_W), GRID_W)

    def scores(rho, slot):
        tok0, d0 = window(rho)
        for g in groups:
            lanes = slice(g * GROUP_W, (g + 1) * GROUP_W)
            qr = t_ref[0, qrows(rho), T_Q + g * GROUP_W:T_Q + (g + 1) * GROUP_W]
            q_bd = jnp.where(own_head, jnp.concatenate([qr] * HEAD_GROUP, axis=0),
                             jnp.zeros(shape, qr.dtype))
            bias = jnp.concatenate(
                [jnp.concatenate([tab_ref[g * HEAD_GROUP + h, d0 + 2 * t] for t in range(WIN_R // 2)],
                                 axis=1) for h in range(HEAD_GROUP)], axis=0)
            s_refs[slot][g, :, 0:TKW] = lax.dot_general(
                q_bd, kwin_ref[pl.ds(tok0, TKW), lanes], _NT,
                preferred_element_type=jnp.float32) + bias
            s_refs[slot][g, :, TKW:TKW + n_ctx] = lax.dot_general(
                q_bd, kc_ref[0, :, lanes], _NT, preferred_element_type=jnp.float32)

    def softmax(slot):
        chunks = [slice(c * SOFTMAX_ROWS, (c + 1) * SOFTMAX_ROWS)
                  for c in range(GROUP_W // SOFTMAX_ROWS)]
        for g in groups:
            for rs in chunks:
                m = jnp.max(s_refs[slot][g, rs, :], axis=-1, keepdims=True)
                m_refs[slot][g, rs, :] = jnp.broadcast_to(m, (SOFTMAX_ROWS, LANES))
        for g in groups:
            for rs in chunks:
                m = jnp.tile(m_refs[slot][g, rs, :], (1, (TKW + n_ctx) // LANES))
                p = jnp.exp2(s_refs[slot][g, rs, :] - m)
                l_refs[slot][g, rs, :] = jnp.broadcast_to(
                    jnp.sum(p, axis=-1, keepdims=True), (SOFTMAX_ROWS, LANES))
                p_refs[slot][g, rs, :] = p.astype(p_refs[slot].dtype)

    def values(rho, slot):
        tok0, _ = window(rho)
        for g in groups:
            lanes = slice(g * GROUP_W, (g + 1) * GROUP_W)
            o = (_dot(p_refs[slot][g, :, 0:TKW], vwin_ref[pl.ds(tok0, TKW), lanes])
                 + _dot(p_refs[slot][g, :, TKW:TKW + n_ctx], vc_ref[0, :, lanes]))
            denom = jnp.tile(l_refs[slot][g], (1, GROUP_W // LANES))
            o = jnp.where(own_head, o / denom, 0.0)
            og = o[0:GRID_W]
            for h in range(1, HEAD_GROUP):
                og = og + o[h * GRID_W:(h + 1) * GRID_W]
            mix_ref[qrows(rho), D_A + g * GROUP_W:D_A + (g + 1) * GROUP_W] = (
                og * t_ref[0, qrows(rho), T_GB + g * GROUP_W:T_GB + (g + 1) * GROUP_W].astype(jnp.float32)
            ).astype(mix_ref.dtype)

    def out_proj(block):
        n = OUT_ROWS * GRID_W
        ts = (slice(block * n, (block + 1) * n) if isinstance(block, int)
              else pl.ds(pl.multiple_of(block * n, n), n))
        o_ref[0, ts] = x_ref[0, ts] + gate_ref[0] * _dot(mix_ref[ts], wout_ref[...])

    def phase(r, slot):
        static = isinstance(r, int)
        if not static or r + LEAD < Q_ROWS:
            scores(r + LEAD, (slot + LEAD) % N_SLOTS)
        if not static or 0 <= r < Q_ROWS:
            softmax(slot)
        if not static or r - LEAD >= 0:
            values(r - LEAD, (slot - LEAD) % N_SLOTS)
        if (slot - LEAD) % OUT_ROWS == 0:
            block = (r - LEAD) // OUT_ROWS - 1
            if not static or block >= 0:
                out_proj(block)

    for r in range(-LEAD, LEAD + N_SLOTS):
        phase(r, r % N_SLOTS)

    def trip(j, carry):
        for t in range(N_SLOTS):
            phase(LEAD + N_SLOTS * j + t, (LEAD + t) % N_SLOTS)
        return carry

    lax.fori_loop(1, (Q_ROWS - 2 * LEAD) // N_SLOTS, trip, 0)
    for r in range(Q_ROWS - LEAD, Q_ROWS + LEAD):
        phase(r, r % N_SLOTS)
    out_proj(Q_ROWS // OUT_ROWS - 1)


def _attn_call(x, gate, tok, k, v, kc, vc, tab, w_out):
    assert OUT_ROWS == N_SLOTS and TQ % (HALO * GRID_W) == 0
    B, L, _ = x.shape
    C = kc.shape[1]
    n_groups = HEADS // HEAD_GROUP
    halo = HALO * GRID_W
    per_step = TQ // halo
    last = L // halo - 1
    tok_spec = lambda width: pl.BlockSpec((1, TQ, width), lambda b, i: (b, i, 0))
    prev_spec = pl.BlockSpec((1, halo, D_B), lambda b, i: (b, jnp.maximum(i * per_step - 1, 0), 0))
    next_spec = pl.BlockSpec((1, halo, D_B), lambda b, i: (b, jnp.minimum((i + 1) * per_step, last), 0))
    ctx_spec = pl.BlockSpec((1, C, D_B), lambda b, i: (b, 0, 0))
    slot_shapes = (
        [pltpu.VMEM((n_groups, GROUP_W, TKW + C), jnp.float32)] * N_SLOTS
        + [pltpu.VMEM((n_groups, GROUP_W, TKW + C), jnp.bfloat16)] * N_SLOTS
        + [pltpu.VMEM((n_groups, GROUP_W, LANES), jnp.float32)] * (2 * N_SLOTS))
    return pl.pallas_call(
        _attn_kernel,
        grid=(B, L // TQ),
        in_specs=[
            tok_spec(D_MODEL),
            pl.BlockSpec((1, 1, D_MODEL), lambda b, i: (b, 0, 0)),
            tok_spec(TOK_W),
            prev_spec, tok_spec(D_B), next_spec,
            prev_spec, tok_spec(D_B), next_spec,
            ctx_spec, ctx_spec,
            pl.BlockSpec(tab.shape, lambda b, i: (0, 0, 0, 0)),
            pl.BlockSpec((D_A + D_B, D_MODEL), lambda b, i: (0, 0)),
        ],
        out_specs=tok_spec(D_MODEL),
        out_shape=jax.ShapeDtypeStruct((B, L, D_MODEL), jnp.float32),
        scratch_shapes=[pltpu.VMEM((TQ, D_A + D_B), jnp.bfloat16),
                        pltpu.VMEM((TQ + 2 * halo, D_B), jnp.bfloat16),
                        pltpu.VMEM((TQ + 2 * halo, D_B), jnp.bfloat16)] + slot_shapes,
        compiler_params=pltpu.CompilerParams(
            dimension_semantics=("arbitrary", "arbitrary"), vmem_limit_bytes=VMEM_LIMIT),
        name="nbr_attn_out",
    )(x, gate, tok, k, k, k, v, v, v, kc, vc, tab, w_out)


def kernel(x, c, ctx, c_ctx, w_ada, b_ada, norm_g, w_in, sgu_norm_g, w_spatial, b_spatial,
           q_norm_g, k_norm_g, rpb, w_out):
    B, L, D = x.shape
    depth = w_ada.shape[0]
    f32, bf16 = jnp.float32, jnp.bfloat16
    bd = jnp.asarray(np.kron(np.eye(MXU_DIM // HEAD_DIM), np.ones((HEAD_DIM, HEAD_DIM))), bf16)
    cc = jnp.concatenate([c, c_ctx[None, :], jnp.zeros((ADA_ROWS - B - 1, D), f32)], axis=0)
    assert depth == 1
    for layer in range(depth):
        mod = _ada_call(cc, w_ada[layer], b_ada[layer][None, :])
        shift, scale, gate = (mod[:B, j * D:(j + 1) * D].reshape(B, 1, D) for j in range(3))
        cshift, cscale = (mod[B:B + 1, j * D:(j + 1) * D] for j in range(2))

        ng = norm_g[layer][None, :]
        kg = jnp.tile(k_norm_g[layer], HEADS)[None, :]
        qg = jnp.tile(q_norm_g[layer], HEADS)[None, :] * (HEAD_DIM ** -0.5 * LOG2E)
        bs = jnp.repeat(b_spatial[layer].T, D_A // SGU_GROUPS, axis=1)
        tok, k, v = _proj_call(
            x, shift, scale, ng, w_in[layer], sgu_norm_g[layer][None, :],
            w_spatial[layer].astype(bf16), bs, qg, kg, bd)
        kc, vc = _ctx_call(ctx, cshift, cscale, ng, w_in[layer], kg, bd)
        tab = _rpb_call(rpb[layer])
        x = _attn_call(x, gate, tok, k, v, kc, vc, tab, w_out[layer].astype(bf16))
    return x
```

```python
import numpy as np
import jax
import jax.numpy as jnp
from jax import lax
from jax.experimental import pallas as pl
from jax.experimental.pallas import tpu as pltpu

D_MODEL = 1024
GRID_W = 64
D_A = 512
D_B = 512
CHUNK = 128
SGU_GROUPS = 4
HEAD_DIM = 64
HEADS = 8
HEAD_GROUP = 4
GROUP_W = HEAD_GROUP * HEAD_DIM
WIN_R = 8
WIN_C = 16
D_IN = 3 * D_A + 4 * D_B
EPS = 1e-6
NEG_INF = -1e30
LOG2E = float(np.log2(np.e))

LANES = 128
VMEM_LIMIT = 60 * 1024 * 1024

TM = 1024
SUB_TM = 512
Q_ROWS = 16
OUT_ROWS = 4
N_SLOTS = 2
SOFTMAX_ROWS = 32
CTX_TM = 1024
TQ = Q_ROWS * GRID_W
TKW = WIN_R * GRID_W
ADA_ROWS = 16
ADA_BN = 768
T_OA, T_GB, T_Q, TOK_W = 0, D_A, D_A + D_B, D_A + 2 * D_B

_NT = (((1,), (1,)), ((), ()))


def _silu(x):
    return x / (1.0 + jnp.exp(-x))


def _gelu(x):
    return 0.5 * x * (1.0 + lax.erf(x * np.float32(np.sqrt(0.5))))


def _dot(a, b):
    return jnp.dot(a, b, preferred_element_type=jnp.float32)


def _ada_kernel(c_ref, w_ref, b_ref, o_ref):
    a, w = _silu(c_ref[...]), w_ref[...]
    a_hi, w_hi = a.astype(jnp.bfloat16), w.astype(jnp.bfloat16)
    a_lo = (a - a_hi.astype(jnp.float32)).astype(jnp.bfloat16)
    w_lo = (w - w_hi.astype(jnp.float32)).astype(jnp.bfloat16)
    o_ref[...] = _dot(a_hi, w_hi) + (_dot(a_lo, w_hi) + _dot(a_hi, w_lo)) + b_ref[...]


def _ada_call(cc, w_ada, b_ada):
    n = w_ada.shape[1]
    return pl.pallas_call(
        _ada_kernel,
        grid=(n // ADA_BN,),
        in_specs=[
            pl.BlockSpec((ADA_ROWS, D_MODEL), lambda j: (0, 0)),
            pl.BlockSpec((D_MODEL, ADA_BN), lambda j: (0, j)),
            pl.BlockSpec((1, ADA_BN), lambda j: (0, j)),
        ],
        out_specs=pl.BlockSpec((ADA_ROWS, ADA_BN), lambda j: (0, j)),
        out_shape=jax.ShapeDtypeStruct((ADA_ROWS, n), jnp.float32),
        compiler_params=pltpu.CompilerParams(dimension_semantics=("arbitrary",)),
        name="ada_params",
    )(cc, w_ada, b_ada)


def _prenorm(x, g, shift, scale):
    ms = jnp.mean(x * x, axis=-1, keepdims=True)
    h = x * lax.rsqrt(ms + EPS) * g
    return (h * (1.0 + scale) + shift).astype(jnp.bfloat16)


def _head_rms(z, gain):
    sq = z * z
    low = lax.broadcasted_iota(jnp.int32, (z.shape[0], LANES), 1) < HEAD_DIM
    sums = []
    for c in range(z.shape[1] // LANES):
        blk = sq[:, c * LANES:(c + 1) * LANES]
        s_lo = jnp.sum(jnp.where(low, blk, 0.0), axis=-1, keepdims=True)
        s_hi = jnp.sum(jnp.where(low, 0.0, blk), axis=-1, keepdims=True)
        sums.append(jnp.where(low, s_lo, s_hi))
    ss = jnp.concatenate(sums, axis=-1)
    return z * lax.rsqrt(ss * (1.0 / HEAD_DIM) + EPS) * gain


def _proj_kernel(x_ref, shift_ref, scale_ref, ng_ref, w_ref, sg_ref, ws_ref, bs_ref,
                 qg_ref, kg_ref, t_ref, k_ref, v_ref, wb_ref):
    @pl.when((pl.program_id(0) == 0) & (pl.program_id(1) == 0))
    def _():
        for lo in range(0, D_IN, D_A):
            wb_ref[:, lo:lo + D_A] = w_ref[:, lo:lo + D_A].astype(wb_ref.dtype)

    for sub in range(TM // SUB_TM):
        _proj_rows(slice(sub * SUB_TM, (sub + 1) * SUB_TM), x_ref, shift_ref, scale_ref, ng_ref,
                   wb_ref, sg_ref, ws_ref, bs_ref, qg_ref, kg_ref, t_ref, k_ref, v_ref)


def _proj_rows(rows, x_ref, shift_ref, scale_ref, ng_ref, wb_ref, sg_ref, ws_ref, bs_ref,
               qg_ref, kg_ref, t_ref, k_ref, v_ref):
    hb = _prenorm(x_ref[0, rows], ng_ref[...], shift_ref[0], scale_ref[0])

    def zcols(lo, width):
        return _dot(hb, wb_ref[:, lo:lo + width])

    gu = _gelu(zcols(0, D_A))
    t_ref[0, rows, T_Q:T_Q + D_B] = _head_rms(zcols(3 * D_A, D_B), qg_ref[...]).astype(t_ref.dtype)
    gv = _gelu(zcols(D_A, D_A))
    k_ref[0, rows] = _head_rms(zcols(3 * D_A + D_B, D_B), kg_ref[...]).astype(k_ref.dtype)
    sa = _silu(zcols(2 * D_A, D_A))
    t_ref[0, rows, T_GB:T_GB + D_B] = _silu(zcols(3 * D_A + 3 * D_B, D_B)).astype(t_ref.dtype)
    v_ref[0, rows] = zcols(3 * D_A + 2 * D_B, D_B).astype(v_ref.dtype)

    for g in range(SGU_GROUPS):
        cs = slice(g * LANES, (g + 1) * LANES)
        vg = gv[:, cs]
        ms = jnp.mean(vg * vg, axis=-1, keepdims=True)
        vn = (vg * lax.rsqrt(ms + EPS) * sg_ref[:, cs]).astype(jnp.bfloat16)
        chunks = [slice(c * CHUNK, (c + 1) * CHUNK) for c in range(SUB_TM // CHUNK)]
        mixed = _dot(ws_ref[g], jnp.concatenate([vn[rs] for rs in chunks], axis=1))
        for c, rs in enumerate(chunks):
            mixed_c = mixed[:, c * CHUNK:(c + 1) * CHUNK] + bs_ref[:, cs]
            out_rows = slice(rows.start + rs.start, rows.start + rs.stop)
            t_ref[0, out_rows, T_OA + g * LANES:T_OA + (g + 1) * LANES] = (
                gu[rs, cs] * mixed_c * sa[rs, cs]).astype(t_ref.dtype)


def _proj_call(x, shift, scale, ng, w_in, sg, ws, bs, qg, kg):
    B, L, _ = x.shape
    const2 = lambda b, i: (0, 0)
    half_spec = pl.BlockSpec((1, TM, D_A), lambda b, i: (b, i, 0))
    half_shape = jax.ShapeDtypeStruct((B, L, D_A), jnp.bfloat16)
    return pl.pallas_call(
        _proj_kernel,
        grid=(B, L // TM),
        in_specs=[
            pl.BlockSpec((1, TM, D_MODEL), lambda b, i: (b, i, 0)),
            pl.BlockSpec((1, 1, D_MODEL), lambda b, i: (b, 0, 0)),
            pl.BlockSpec((1, 1, D_MODEL), lambda b, i: (b, 0, 0)),
            pl.BlockSpec((1, D_MODEL), const2),
            pl.BlockSpec((D_MODEL, D_IN), const2),
            pl.BlockSpec((1, D_A), const2),
            pl.BlockSpec((SGU_GROUPS, CHUNK, CHUNK), lambda b, i: (0, 0, 0)),
            pl.BlockSpec((CHUNK, D_A), const2),
            pl.BlockSpec((1, D_B), const2),
            pl.BlockSpec((1, D_B), const2),
        ],
        out_specs=[pl.BlockSpec((1, TM, TOK_W), lambda b, i: (b, i, 0)), half_spec, half_spec],
        out_shape=[jax.ShapeDtypeStruct((B, L, TOK_W), jnp.bfloat16), half_shape, half_shape],
        scratch_shapes=[pltpu.VMEM((D_MODEL, D_IN), jnp.bfloat16)],
        compiler_params=pltpu.CompilerParams(
            dimension_semantics=("arbitrary", "arbitrary"), vmem_limit_bytes=VMEM_LIMIT),
        name="latent_proj",
    )(x, shift, scale, ng, w_in, sg, ws, bs, qg, kg)


def _ctx_kernel(x_ref, shift_ref, scale_ref, ng_ref, w_ref, kg_ref, k_ref, v_ref):
    hb = _prenorm(x_ref[...], ng_ref[...], shift_ref[...], scale_ref[...])
    wk = w_ref[:, 0:D_B].astype(jnp.bfloat16)
    wv = w_ref[:, D_B:2 * D_B].astype(jnp.bfloat16)
    k_ref[...] = _head_rms(_dot(hb, wk), kg_ref[...]).astype(k_ref.dtype)
    v_ref[...] = _dot(hb, wv).astype(v_ref.dtype)


def _ctx_call(ctx, cshift, cscale, ng, w_in, kg):
    B, C, _ = ctx.shape
    const2 = lambda b: (0, 0)
    kv_block = (3 * D_A + D_B) // (2 * D_B)
    kv_spec = pl.BlockSpec((CTX_TM, D_B), lambda b: (b, 0))
    kv_shape = jax.ShapeDtypeStruct((B * C, D_B), jnp.bfloat16)
    kc, vc = pl.pallas_call(
        _ctx_kernel,
        grid=(B * C // CTX_TM,),
        in_specs=[
            pl.BlockSpec((CTX_TM, D_MODEL), lambda b: (b, 0)),
            pl.BlockSpec((1, D_MODEL), const2),
            pl.BlockSpec((1, D_MODEL), const2),
            pl.BlockSpec((1, D_MODEL), const2),
            pl.BlockSpec((D_MODEL, 2 * D_B), lambda b: (0, kv_block)),
            pl.BlockSpec((1, D_B), const2),
        ],
        out_specs=[kv_spec, kv_spec],
        out_shape=[kv_shape, kv_shape],
        compiler_params=pltpu.CompilerParams(
            dimension_semantics=("arbitrary",), vmem_limit_bytes=VMEM_LIMIT),
        name="ctx_kv",
    )(ctx.reshape(B * C, D_MODEL), cshift, cscale, ng, w_in, kg)
    return kc.reshape(B, C, D_B), vc.reshape(B, C, D_B)


def _rpb_kernel(rpb_ref, win_ref, o_ref):
    n_heads, n_pairs = o_ref.shape[:2]
    n_dr = n_pairs + 1
    blk = o_ref.shape[2:]
    in_win = win_ref[...] != 0
    left_half = lax.broadcasted_iota(jnp.int32, blk, 1) < GRID_W
    first = LANES - (WIN_C - 1)
    for h in range(n_heads):
        for dr in range(n_pairs):
            row = h * n_dr + dr
            lo = pltpu.roll(jnp.broadcast_to(rpb_ref[row:row + 1, :], blk), first, 1,
                            stride=1, stride_axis=0)
            hi = pltpu.roll(jnp.broadcast_to(rpb_ref[row + 1:row + 2, :], blk),
                            (first + GRID_W) % LANES, 1, stride=1, stride_axis=0)
            o_ref[h, dr] = jnp.where(in_win, jnp.where(left_half, lo, hi) * LOG2E, NEG_INF)


def _rpb_call(rpb):
    H, n_dr, n_dc = rpb.shape
    cols = np.arange(GRID_W)
    c0 = np.clip(cols - WIN_C // 2, 0, GRID_W - WIN_C)
    in_win = (cols[None, :] >= c0[:, None]) & (cols[None, :] < c0[:, None] + WIN_C)
    win = jnp.asarray(np.tile(in_win, (1, LANES // GRID_W)), jnp.int32)
    rpb2 = jnp.pad(rpb.astype(jnp.float32).reshape(H * n_dr, n_dc), ((0, 0), (0, LANES - n_dc)))
    return pl.pallas_call(
        _rpb_kernel,
        out_shape=jax.ShapeDtypeStruct((H, n_dr - 1, GRID_W, LANES), jnp.float32),
        name="rpb_toeplitz",
    )(rpb2, win)


def _attn_kernel(x_ref, gate_ref, t_ref, k_ref, v_ref, kc_ref, vc_ref,
                 tab_ref, wout_ref, o_ref, mix_ref, s_ref, p_ref, l_ref, wob_ref):
    i = pl.program_id(1)

    @pl.when((pl.program_id(0) == 0) & (i == 0))
    def _():
        wob_ref[...] = wout_ref[...].astype(wob_ref.dtype)

    rows = pl.num_programs(1) * Q_ROWS
    shape = (GROUP_W, GROUP_W)
    row_head = lax.broadcasted_iota(jnp.int32, shape, 0) // HEAD_DIM
    lane_head = lax.broadcasted_iota(jnp.int32, shape, 1) // HEAD_DIM
    own_head = row_head == lane_head

    n_ctx = kc_ref.shape[1]
    groups = range(HEADS // HEAD_GROUP)

    def window(rho):
        r = i * Q_ROWS + rho
        r0 = jnp.clip(r - WIN_R // 2, 0, rows - WIN_R)
        d0 = r0 - r + (WIN_R - 1)
        return pl.multiple_of(r0 * GRID_W, GRID_W), d0

    def qrows(rho):
        return slice(rho * GRID_W, (rho + 1) * GRID_W)

    def scores(rho, slot):
        tok0, d0 = window(rho)
        for g in groups:
            lanes = slice(g * GROUP_W, (g + 1) * GROUP_W)
            qr = t_ref[0, qrows(rho), T_Q + g * GROUP_W:T_Q + (g + 1) * GROUP_W]
            q_bd = jnp.where(own_head, jnp.concatenate([qr] * HEAD_GROUP, axis=0),
                             jnp.zeros(shape, qr.dtype))
            bias = jnp.concatenate(
                [jnp.concatenate([tab_ref[g * HEAD_GROUP + h, d0 + 2 * t] for t in range(WIN_R // 2)],
                                 axis=1) for h in range(HEAD_GROUP)], axis=0)
            s_ref[slot, g, :, 0:TKW] = lax.dot_general(
                q_bd, k_ref[0, pl.ds(tok0, TKW), lanes], _NT,
                preferred_element_type=jnp.float32) + bias
            s_ref[slot, g, :, TKW:TKW + n_ctx] = lax.dot_general(
                q_bd, kc_ref[0, :, lanes], _NT, preferred_element_type=jnp.float32)

    def softmax(slot):
        for g in groups:
            for c in range(GROUP_W // SOFTMAX_ROWS):
                rs = slice(c * SOFTMAX_ROWS, (c + 1) * SOFTMAX_ROWS)
                s = s_ref[slot, g, rs, :]
                p = jnp.exp2(s - jnp.max(s, axis=-1, keepdims=True))
                l_ref[slot, g, rs, :] = jnp.sum(p, axis=-1, keepdims=True)
                p_ref[slot, g, rs, :] = p.astype(p_ref.dtype)

    def values(rho, slot):
        tok0, _ = window(rho)
        for g in groups:
            lanes = slice(g * GROUP_W, (g + 1) * GROUP_W)
            o = (_dot(p_ref[slot, g, :, 0:TKW], v_ref[0, pl.ds(tok0, TKW), lanes])
                 + _dot(p_ref[slot, g, :, TKW:TKW + n_ctx], vc_ref[0, :, lanes]))
            o = jnp.where(own_head, o / l_ref[slot, g], 0.0)
            og = o[0:GRID_W]
            for h in range(1, HEAD_GROUP):
                og = og + o[h * GRID_W:(h + 1) * GRID_W]
            mix_ref[qrows(rho), D_A + g * GROUP_W:D_A + (g + 1) * GROUP_W] = (
                og * t_ref[0, qrows(rho), T_GB + g * GROUP_W:T_GB + (g + 1) * GROUP_W].astype(jnp.float32)
            ).astype(mix_ref.dtype)

    mix_ref[:, 0:D_A] = t_ref[0, :, T_OA:T_OA + D_A]
    scores(0, 0)
    for rho in range(Q_ROWS):
        if rho + 1 < Q_ROWS:
            scores(rho + 1, (rho + 1) % N_SLOTS)
        softmax(rho % N_SLOTS)
        values(rho, rho % N_SLOTS)
        if (rho + 1) % OUT_ROWS == 0:
            ts = slice((rho + 1 - OUT_ROWS) * GRID_W, (rho + 1) * GRID_W)
            o_ref[0, ts] = x_ref[0, ts] + gate_ref[0] * _dot(mix_ref[ts], wob_ref[...])


def _attn_call(x, gate, tok, k, v, kc, vc, tab, w_out):
    B, L, _ = x.shape
    C = kc.shape[1]
    n_groups = HEADS // HEAD_GROUP
    tok_spec = lambda width: pl.BlockSpec((1, TQ, width), lambda b, i: (b, i, 0))
    batch_spec = lambda n: pl.BlockSpec((1, n, D_B), lambda b, i: (b, 0, 0))
    return pl.pallas_call(
        _attn_kernel,
        grid=(B, L // TQ),
        in_specs=[
            tok_spec(D_MODEL),
            pl.BlockSpec((1, 1, D_MODEL), lambda b, i: (b, 0, 0)),
            tok_spec(TOK_W),
            batch_spec(L),
            batch_spec(L),
            batch_spec(C),
            batch_spec(C),
            pl.BlockSpec(tab.shape, lambda b, i: (0, 0, 0, 0)),
            pl.BlockSpec((D_A + D_B, D_MODEL), lambda b, i: (0, 0)),
        ],
        out_specs=tok_spec(D_MODEL),
        out_shape=jax.ShapeDtypeStruct((B, L, D_MODEL), jnp.float32),
        scratch_shapes=[pltpu.VMEM((TQ, D_A + D_B), jnp.bfloat16),
                        pltpu.VMEM((N_SLOTS, n_groups, GROUP_W, TKW + C), jnp.float32),
                        pltpu.VMEM((N_SLOTS, n_groups, GROUP_W, TKW + C), jnp.bfloat16),
                        pltpu.VMEM((N_SLOTS, n_groups, GROUP_W, 1), jnp.float32),
                        pltpu.VMEM((D_A + D_B, D_MODEL), jnp.bfloat16)],
        compiler_params=pltpu.CompilerParams(
            dimension_semantics=("arbitrary", "arbitrary"), vmem_limit_bytes=VMEM_LIMIT),
        name="nbr_attn_out",
    )(x, gate, tok, k, v, kc, vc, tab, w_out)


def kernel(x, c, ctx, c_ctx, w_ada, b_ada, norm_g, w_in, sgu_norm_g, w_spatial, b_spatial,
           q_norm_g, k_norm_g, rpb, w_out):
    B, L, D = x.shape
    depth = w_ada.shape[0]
    f32, bf16 = jnp.float32, jnp.bfloat16
    cc = jnp.concatenate([c, c_ctx[None, :], jnp.zeros((ADA_ROWS - B - 1, D), f32)], axis=0)
    assert depth == 1
    for layer in range(depth):
        mod = _ada_call(cc, w_ada[layer], b_ada[layer][None, :])
        shift, scale, gate = (mod[:B, j * D:(j + 1) * D].reshape(B, 1, D) for j in range(3))
        cshift, cscale = (mod[B:B + 1, j * D:(j + 1) * D] for j in range(2))

        ng = norm_g[layer][None, :]
        kg = jnp.tile(k_norm_g[layer], HEADS)[None, :]
        qg = jnp.tile(q_norm_g[layer], HEADS)[None, :] * (HEAD_DIM ** -0.5 * LOG2E)
        bs = jnp.repeat(b_spatial[layer].T, D_A // SGU_GROUPS, axis=1)
        tok, k, v = _proj_call(
            x, shift, scale, ng, w_in[layer], sgu_norm_g[layer][None, :],
            w_spatial[layer].astype(bf16), bs, qg, kg)
        kc, vc = _ctx_call(ctx, cshift, cscale, ng, w_in[layer], kg)
        tab = _rpb_call(rpb[layer])
        x = _attn_call(x, gate, tok, k, v, kc, vc, tab, w_out[layer])
    return x
```

```python
import numpy as np
import jax
import jax.numpy as jnp
from jax import lax
from jax.experimental import pallas as pl
from jax.experimental.pallas import tpu as pltpu

D_MODEL = 1024
GRID_W = 64
D_A = 512
D_B = 512
CHUNK = 128
SGU_GROUPS = 4
HEAD_DIM = 64
HEADS = 8
HEAD_GROUP = 4
GROUP_W = HEAD_GROUP * HEAD_DIM
WIN_R = 8
WIN_C = 16
D_IN = 3 * D_A + 4 * D_B
EPS = 1e-6
NEG_INF = -1e30
LOG2E = float(np.log2(np.e))

LANES = 128
VMEM_LIMIT = 60 * 1024 * 1024

TM = 1024
SUB_TM = 512
Q_ROWS = 16
OUT_ROWS = 4
N_SLOTS = 2
SOFTMAX_ROWS = 32
CTX_TM = 1024
TQ = Q_ROWS * GRID_W
TKW = WIN_R * GRID_W
ADA_ROWS = 16
ADA_BK = 256
MOD_SHIFT, MOD_SCALE, MOD_GATE = 0, 1, 2
T_OA, T_GB, T_Q, TOK_W = 0, D_A, D_A + D_B, D_A + 2 * D_B

_NT = (((1,), (1,)), ((), ()))


def _silu(x):
    half = 0.5 * x
    return half * (1.0 + jnp.tanh(half))


def _gelu(x):
    return 0.5 * x * (1.0 + lax.erf(x * np.float32(np.sqrt(0.5))))


def _dot(a, b):
    return jnp.dot(a, b, preferred_element_type=jnp.float32)


def _ada_kernel(c_ref, w_ref, b_ref, o_ref):
    @pl.when(pl.program_id(0) == 0)
    def _():
        o_ref[...] = jnp.broadcast_to(b_ref[...], o_ref.shape)

    a, w = _silu(c_ref[...]), w_ref[...]
    a_hi, w_hi = a.astype(jnp.bfloat16), w.astype(jnp.bfloat16)
    a_lo = (a - a_hi.astype(jnp.float32)).astype(jnp.bfloat16)
    w_lo = (w - w_hi.astype(jnp.float32)).astype(jnp.bfloat16)
    o_ref[...] += _dot(a_hi, w_hi) + (_dot(a_lo, w_hi) + _dot(a_hi, w_lo))


def _ada_call(cc, w_ada, b_ada):
    k, n = w_ada.shape
    return pl.pallas_call(
        _ada_kernel,
        grid=(k // ADA_BK,),
        in_specs=[
            pl.BlockSpec((ADA_ROWS, ADA_BK), lambda j: (0, j)),
            pl.BlockSpec((ADA_BK, n), lambda j: (j, 0)),
            pl.BlockSpec((1, n), lambda j: (0, 0)),
        ],
        out_specs=pl.BlockSpec((ADA_ROWS, n), lambda j: (0, 0)),
        out_shape=jax.ShapeDtypeStruct((ADA_ROWS, n), jnp.float32),
        compiler_params=pltpu.CompilerParams(dimension_semantics=("arbitrary",)),
        name="ada_params",
    )(cc, w_ada, b_ada)


def _prenorm(x, g, shift, scale):
    ms = jnp.mean(x * x, axis=-1, keepdims=True)
    return (x * lax.rsqrt(ms + EPS) * (g * (1.0 + scale)) + shift).astype(jnp.bfloat16)


def _head_rms(z, gain):
    sq = z * z
    low = lax.broadcasted_iota(jnp.int32, (z.shape[0], LANES), 1) < HEAD_DIM
    sums = []
    for c in range(z.shape[1] // LANES):
        blk = sq[:, c * LANES:(c + 1) * LANES]
        s_lo = jnp.sum(jnp.where(low, blk, 0.0), axis=-1, keepdims=True)
        s_hi = jnp.sum(jnp.where(low, 0.0, blk), axis=-1, keepdims=True)
        sums.append(jnp.where(low, s_lo, s_hi))
    ss = jnp.concatenate(sums, axis=-1)
    return z * lax.rsqrt(ss * (1.0 / HEAD_DIM) + EPS) * gain


def _proj_kernel(x_ref, shift_ref, scale_ref, ng_ref, w_ref, sg_ref, ws_ref, bs_ref,
                 qg_ref, kg_ref, t_ref, k_ref, v_ref, wb_ref):
    @pl.when((pl.program_id(0) == 0) & (pl.program_id(1) == 0))
    def _():
        for lo in range(0, D_IN, D_A):
            wb_ref[:, lo:lo + D_A] = w_ref[:, lo:lo + D_A].astype(wb_ref.dtype)

    for sub in range(TM // SUB_TM):
        _proj_rows(slice(sub * SUB_TM, (sub + 1) * SUB_TM), x_ref, shift_ref, scale_ref, ng_ref,
                   wb_ref, sg_ref, ws_ref, bs_ref, qg_ref, kg_ref, t_ref, k_ref, v_ref)


def _proj_rows(rows, x_ref, shift_ref, scale_ref, ng_ref, wb_ref, sg_ref, ws_ref, bs_ref,
               qg_ref, kg_ref, t_ref, k_ref, v_ref):
    hb = _prenorm(x_ref[0, rows], ng_ref[...], shift_ref[0], scale_ref[0])

    def zcols(lo, width):
        return _dot(hb, wb_ref[:, lo:lo + width])

    gu = _gelu(zcols(0, D_A))
    t_ref[0, rows, T_Q:T_Q + D_B] = _head_rms(zcols(3 * D_A, D_B), qg_ref[...]).astype(t_ref.dtype)
    gv = _gelu(zcols(D_A, D_A))
    k_ref[0, rows] = _head_rms(zcols(3 * D_A + D_B, D_B), kg_ref[...]).astype(k_ref.dtype)
    sa = _silu(zcols(2 * D_A, D_A))
    t_ref[0, rows, T_GB:T_GB + D_B] = _silu(zcols(3 * D_A + 3 * D_B, D_B)).astype(t_ref.dtype)
    v_ref[0, rows] = zcols(3 * D_A + 2 * D_B, D_B).astype(v_ref.dtype)

    for g in range(SGU_GROUPS):
        cs = slice(g * LANES, (g + 1) * LANES)
        vg = gv[:, cs]
        ms = jnp.mean(vg * vg, axis=-1, keepdims=True)
        vn = (vg * lax.rsqrt(ms + EPS) * sg_ref[:, cs]).astype(jnp.bfloat16)
        chunks = [slice(c * CHUNK, (c + 1) * CHUNK) for c in range(SUB_TM // CHUNK)]
        mixed = _dot(ws_ref[g], jnp.concatenate([vn[rs] for rs in chunks], axis=1))
        for c, rs in enumerate(chunks):
            mixed_c = mixed[:, c * CHUNK:(c + 1) * CHUNK] + bs_ref[:, cs]
            out_rows = slice(rows.start + rs.start, rows.start + rs.stop)
            t_ref[0, out_rows, T_OA + g * LANES:T_OA + (g + 1) * LANES] = (
                gu[rs, cs] * mixed_c * sa[rs, cs]).astype(t_ref.dtype)


def _proj_call(x, mod, ng, w_in, sg, ws, bs, qg, kg):
    B, L, _ = x.shape
    const2 = lambda b, i: (0, 0)
    half_spec = pl.BlockSpec((1, TM, D_A), lambda b, i: (b, i, 0))
    half_shape = jax.ShapeDtypeStruct((B, L, D_A), jnp.bfloat16)
    return pl.pallas_call(
        _proj_kernel,
        grid=(B, L // TM),
        in_specs=[
            pl.BlockSpec((1, TM, D_MODEL), lambda b, i: (b, i, 0)),
            pl.BlockSpec((1, 1, D_MODEL), lambda b, i: (b, 0, MOD_SHIFT)),
            pl.BlockSpec((1, 1, D_MODEL), lambda b, i: (b, 0, MOD_SCALE)),
            pl.BlockSpec((1, D_MODEL), const2),
            pl.BlockSpec((D_MODEL, D_IN), const2),
            pl.BlockSpec((1, D_A), const2),
            pl.BlockSpec((SGU_GROUPS, CHUNK, CHUNK), lambda b, i: (0, 0, 0)),
            pl.BlockSpec((CHUNK, D_A), const2),
            pl.BlockSpec((1, D_B), const2),
            pl.BlockSpec((1, D_B), const2),
        ],
        out_specs=[pl.BlockSpec((1, TM, TOK_W), lambda b, i: (b, i, 0)), half_spec, half_spec],
        out_shape=[jax.ShapeDtypeStruct((B, L, TOK_W), jnp.bfloat16), half_shape, half_shape],
        scratch_shapes=[pltpu.VMEM((D_MODEL, D_IN), jnp.bfloat16)],
        compiler_params=pltpu.CompilerParams(
            dimension_semantics=("arbitrary", "arbitrary"), vmem_limit_bytes=VMEM_LIMIT),
        name="latent_proj",
    )(x, mod, mod, ng, w_in, sg, ws, bs, qg, kg)


def _ctx_kernel(x_ref, shift_ref, scale_ref, ng_ref, w_ref, kg_ref, k_ref, v_ref):
    hb = _prenorm(x_ref[...], ng_ref[...], shift_ref[0], scale_ref[0])
    wk = w_ref[:, 0:D_B].astype(jnp.bfloat16)
    wv = w_ref[:, D_B:2 * D_B].astype(jnp.bfloat16)
    k_ref[...] = _head_rms(_dot(hb, wk), kg_ref[...]).astype(k_ref.dtype)
    v_ref[...] = _dot(hb, wv).astype(v_ref.dtype)


def _ctx_call(ctx, mod, ng, w_in, kg):
    B, C, _ = ctx.shape
    const2 = lambda b: (0, 0)
    kv_block = (3 * D_A + D_B) // (2 * D_B)
    kv_spec = pl.BlockSpec((CTX_TM, D_B), lambda b: (b, 0))
    kv_shape = jax.ShapeDtypeStruct((B * C, D_B), jnp.bfloat16)
    kc, vc = pl.pallas_call(
        _ctx_kernel,
        grid=(B * C // CTX_TM,),
        in_specs=[
            pl.BlockSpec((CTX_TM, D_MODEL), lambda b: (b, 0)),
            pl.BlockSpec((1, 1, D_MODEL), lambda b: (B, 0, MOD_SHIFT)),
            pl.BlockSpec((1, 1, D_MODEL), lambda b: (B, 0, MOD_SCALE)),
            pl.BlockSpec((1, D_MODEL), const2),
            pl.BlockSpec((D_MODEL, 2 * D_B), lambda b: (0, kv_block)),
            pl.BlockSpec((1, D_B), const2),
        ],
        out_specs=[kv_spec, kv_spec],
        out_shape=[kv_shape, kv_shape],
        compiler_params=pltpu.CompilerParams(
            dimension_semantics=("arbitrary",), vmem_limit_bytes=VMEM_LIMIT),
        name="ctx_kv",
    )(ctx.reshape(B * C, D_MODEL), mod, mod, ng, w_in, kg)
    return kc.reshape(B, C, D_B), vc.reshape(B, C, D_B)


def _rpb_kernel(rpb_ref, win_ref, o_ref):
    n_heads, n_pairs = o_ref.shape[:2]
    n_dr = n_pairs + 1
    blk = o_ref.shape[2:]
    in_win = win_ref[...] != 0
    left_half = lax.broadcasted_iota(jnp.int32, blk, 1) < GRID_W
    first = LANES - (WIN_C - 1)
    for h in range(n_heads):
        for dr in range(n_pairs):
            row = h * n_dr + dr
            lo = pltpu.roll(jnp.broadcast_to(rpb_ref[row:row + 1, :], blk), first, 1,
                            stride=1, stride_axis=0)
            hi = pltpu.roll(jnp.broadcast_to(rpb_ref[row + 1:row + 2, :], blk),
                            (first + GRID_W) % LANES, 1, stride=1, stride_axis=0)
            o_ref[h, dr] = jnp.where(in_win, jnp.where(left_half, lo, hi) * LOG2E, NEG_INF)


def _rpb_call(rpb):
    H, n_dr, n_dc = rpb.shape
    cols = np.arange(GRID_W)
    c0 = np.clip(cols - WIN_C // 2, 0, GRID_W - WIN_C)
    in_win = (cols[None, :] >= c0[:, None]) & (cols[None, :] < c0[:, None] + WIN_C)
    win = jnp.asarray(np.tile(in_win, (1, LANES // GRID_W)), jnp.int32)
    rpb2 = jnp.pad(rpb.astype(jnp.float32).reshape(H * n_dr, n_dc), ((0, 0), (0, LANES - n_dc)))
    return pl.pallas_call(
        _rpb_kernel,
        out_shape=jax.ShapeDtypeStruct((H, n_dr - 1, GRID_W, LANES), jnp.float32),
        name="rpb_toeplitz",
    )(rpb2, win)


def _attn_kernel(x_ref, gate_ref, t_ref, k_ref, v_ref, kc_ref, vc_ref,
                 tab_ref, wout_ref, o_ref, mix_ref, s_ref, p_ref, l_ref, wob_ref):
    i = pl.program_id(1)

    @pl.when((pl.program_id(0) == 0) & (i == 0))
    def _():
        wob_ref[...] = wout_ref[...].astype(wob_ref.dtype)

    rows = pl.num_programs(1) * Q_ROWS
    shape = (GROUP_W, GROUP_W)
    row_head = lax.broadcasted_iota(jnp.int32, shape, 0) // HEAD_DIM
    lane_head = lax.broadcasted_iota(jnp.int32, shape, 1) // HEAD_DIM
    own_head = row_head == lane_head

    n_ctx = kc_ref.shape[1]
    groups = range(HEADS // HEAD_GROUP)

    def window(rho):
        r = i * Q_ROWS + rho
        r0 = jnp.clip(r - WIN_R // 2, 0, rows - WIN_R)
        d0 = r0 - r + (WIN_R - 1)
        return pl.multiple_of(r0 * GRID_W, GRID_W), d0

    def qrows(rho):
        return slice(rho * GRID_W, (rho + 1) * GRID_W)

    def scores(rho, slot):
        tok0, d0 = window(rho)
        for g in groups:
            lanes = slice(g * GROUP_W, (g + 1) * GROUP_W)
            qr = t_ref[0, qrows(rho), T_Q + g * GROUP_W:T_Q + (g + 1) * GROUP_W]
            q_bd = jnp.where(own_head, jnp.concatenate([qr] * HEAD_GROUP, axis=0),
                             jnp.zeros(shape, qr.dtype))
            bias = jnp.concatenate(
                [jnp.concatenate([tab_ref[g * HEAD_GROUP + h, d0 + 2 * t] for t in range(WIN_R // 2)],
                                 axis=1) for h in range(HEAD_GROUP)], axis=0)
            s_ref[slot, g, :, 0:TKW] = lax.dot_general(
                q_bd, k_ref[0, pl.ds(tok0, TKW), lanes], _NT,
                preferred_element_type=jnp.float32) + bias
            s_ref[slot, g, :, TKW:TKW + n_ctx] = lax.dot_general(
                q_bd, kc_ref[0, :, lanes], _NT, preferred_element_type=jnp.float32)

    def softmax(slot):
        for g in groups:
            for c in range(GROUP_W // SOFTMAX_ROWS):
                rs = slice(c * SOFTMAX_ROWS, (c + 1) * SOFTMAX_ROWS)
                s = s_ref[slot, g, rs, :]
                p = jnp.exp2(s - jnp.max(s, axis=-1, keepdims=True))
                l_ref[slot, g, rs, :] = jnp.sum(p, axis=-1, keepdims=True)
                p_ref[slot, g, rs, :] = p.astype(p_ref.dtype)

    def values(rho, slot):
        tok0, _ = window(rho)
        for g in groups:
            lanes = slice(g * GROUP_W, (g + 1) * GROUP_W)
            o = (_dot(p_ref[slot, g, :, 0:TKW], v_ref[0, pl.ds(tok0, TKW), lanes])
                 + _dot(p_ref[slot, g, :, TKW:TKW + n_ctx], vc_ref[0, :, lanes]))
            o = jnp.where(own_head, o / l_ref[slot, g], 0.0)
            og = o[0:GRID_W]
            for h in range(1, HEAD_GROUP):
                og = og + o[h * GRID_W:(h + 1) * GRID_W]
            mix_ref[qrows(rho), D_A + g * GROUP_W:D_A + (g + 1) * GROUP_W] = (
                og * t_ref[0, qrows(rho), T_GB + g * GROUP_W:T_GB + (g + 1) * GROUP_W].astype(jnp.float32)
            ).astype(mix_ref.dtype)

    mix_ref[:, 0:D_A] = t_ref[0, :, T_OA:T_OA + D_A]
    scores(0, 0)
    for rho in range(Q_ROWS):
        if rho + 1 < Q_ROWS:
            scores(rho + 1, (rho + 1) % N_SLOTS)
        softmax(rho % N_SLOTS)
        values(rho, rho % N_SLOTS)
        if (rho + 1) % OUT_ROWS == 0:
            ts = slice((rho + 1 - OUT_ROWS) * GRID_W, (rho + 1) * GRID_W)
            o_ref[0, ts] = x_ref[0, ts] + gate_ref[0] * _dot(mix_ref[ts], wob_ref[...])


def _attn_call(x, mod, tok, k, v, kc, vc, tab, w_out):
    B, L, _ = x.shape
    C = kc.shape[1]
    n_groups = HEADS // HEAD_GROUP
    tok_spec = lambda width: pl.BlockSpec((1, TQ, width), lambda b, i: (b, i, 0))
    batch_spec = lambda n: pl.BlockSpec((1, n, D_B), lambda b, i: (b, 0, 0))
    return pl.pallas_call(
        _attn_kernel,
        grid=(B, L // TQ),
        in_specs=[
            tok_spec(D_MODEL),
            pl.BlockSpec((1, 1, D_MODEL), lambda b, i: (b, 0, MOD_GATE)),
            tok_spec(TOK_W),
            batch_spec(L),
            batch_spec(L),
            batch_spec(C),
            batch_spec(C),
            pl.BlockSpec(tab.shape, lambda b, i: (0, 0, 0, 0)),
            pl.BlockSpec((D_A + D_B, D_MODEL), lambda b, i: (0, 0)),
        ],
        out_specs=tok_spec(D_MODEL),
        out_shape=jax.ShapeDtypeStruct((B, L, D_MODEL), jnp.float32),
        scratch_shapes=[pltpu.VMEM((TQ, D_A + D_B), jnp.bfloat16),
                        pltpu.VMEM((N_SLOTS, n_groups, GROUP_W, TKW + C), jnp.float32),
                        pltpu.VMEM((N_SLOTS, n_groups, GROUP_W, TKW + C), jnp.bfloat16),
                        pltpu.VMEM((N_SLOTS, n_groups, GROUP_W, 1), jnp.float32),
                        pltpu.VMEM((D_A + D_B, D_MODEL), jnp.bfloat16)],
        compiler_params=pltpu.CompilerParams(
            dimension_semantics=("arbitrary", "arbitrary"), vmem_limit_bytes=VMEM_LIMIT),
        name="nbr_attn_out",
    )(x, mod, tok, k, v, kc, vc, tab, w_out)


def kernel(x, c, ctx, c_ctx, w_ada, b_ada, norm_g, w_in, sgu_norm_g, w_spatial, b_spatial,
           q_norm_g, k_norm_g, rpb, w_out):
    B, L, D = x.shape
    depth = w_ada.shape[0]
    f32, bf16 = jnp.float32, jnp.bfloat16
    cc = jnp.concatenate([c, c_ctx[None, :], jnp.zeros((ADA_ROWS - B - 1, D), f32)], axis=0)
    assert depth == 1
    for layer in range(depth):
        mod = _ada_call(cc, w_ada[layer], b_ada[layer][None, :])
        mod = mod.reshape(ADA_ROWS, 1, 3 * D)

        ng = norm_g[layer][None, :]
        kg = jnp.tile(k_norm_g[layer], HEADS)[None, :]
        qg = jnp.tile(q_norm_g[layer], HEADS)[None, :] * (HEAD_DIM ** -0.5 * LOG2E)
        bs = jnp.repeat(b_spatial[layer].T, D_A // SGU_GROUPS, axis=1)
        tok, k, v = _proj_call(
            x, mod, ng, w_in[layer], sgu_norm_g[layer][None, :],
            w_spatial[layer].astype(bf16), bs, qg, kg)
        kc, vc = _ctx_call(ctx, mod, ng, w_in[layer], kg)
        tab = _rpb_call(rpb[layer])
        x = _attn_call(x, mod, tok, k, v, kc, vc, tab, w_out[layer])
    return x
```

```python
import numpy as np
import jax
import jax.numpy as jnp
from jax import lax
from jax.experimental import pallas as pl
from jax.experimental.pallas import tpu as pltpu

D_MODEL = 1024
GRID_W = 64
D_A = 512
D_B = 512
CHUNK = 128
SGU_GROUPS = 4
HEAD_DIM = 64
HEADS = 8
HEAD_GROUP = 4
GROUP_W = HEAD_GROUP * HEAD_DIM
WIN_R = 8
WIN_C = 16
D_IN = 3 * D_A + 4 * D_B
EPS = 1e-6
NEG_INF = -1e30
LOG2E = float(np.log2(np.e))

LANES = 128
VMEM_LIMIT = 60 * 1024 * 1024

TM = 1024
SUB_TM = 512
Q_ROWS = 16
OUT_ROWS = 4
N_SLOTS = 2
CTX_TM = 1024
TQ = Q_ROWS * GRID_W
TKW = WIN_R * GRID_W
ADA_ROWS = 16
ADA_BK = 256
MOD_SHIFT, MOD_SCALE, MOD_GATE = 0, 1, 2
T_OA, T_GB, T_Q, TOK_W = 0, D_A, D_A + D_B, D_A + 2 * D_B

_NT = (((1,), (1,)), ((), ()))


def _silu(x):
    half = 0.5 * x
    return half * (1.0 + jnp.tanh(half))


def _gelu(x):
    return 0.5 * x * (1.0 + lax.erf(x * np.float32(np.sqrt(0.5))))


def _dot(a, b):
    return jnp.dot(a, b, preferred_element_type=jnp.float32)


def _ada_kernel(c_ref, w_ref, b_ref, o_ref):
    @pl.when(pl.program_id(0) == 0)
    def _():
        o_ref[...] = jnp.broadcast_to(b_ref[...], o_ref.shape)

    a, w = _silu(c_ref[...]), w_ref[...]
    a_hi, w_hi = a.astype(jnp.bfloat16), w.astype(jnp.bfloat16)
    a_lo = (a - a_hi.astype(jnp.float32)).astype(jnp.bfloat16)
    w_lo = (w - w_hi.astype(jnp.float32)).astype(jnp.bfloat16)
    o_ref[...] += _dot(a_hi, w_hi) + (_dot(a_lo, w_hi) + _dot(a_hi, w_lo))


def _ada_call(cc, w_ada, b_ada):
    k, n = w_ada.shape
    return pl.pallas_call(
        _ada_kernel,
        grid=(k // ADA_BK,),
        in_specs=[
            pl.BlockSpec((ADA_ROWS, ADA_BK), lambda j: (0, j)),
            pl.BlockSpec((ADA_BK, n), lambda j: (j, 0)),
            pl.BlockSpec((1, n), lambda j: (0, 0)),
        ],
        out_specs=pl.BlockSpec((ADA_ROWS, n), lambda j: (0, 0)),
        out_shape=jax.ShapeDtypeStruct((ADA_ROWS, n), jnp.float32),
        compiler_params=pltpu.CompilerParams(dimension_semantics=("arbitrary",)),
        name="ada_params",
    )(cc, w_ada, b_ada)


def _prenorm(x, g, shift, scale):
    ms = jnp.mean(x * x, axis=-1, keepdims=True)
    return (x * lax.rsqrt(ms + EPS) * (g * (1.0 + scale)) + shift).astype(jnp.bfloat16)


def _head_rms(z, gain):
    sq = z * z
    low = lax.broadcasted_iota(jnp.int32, (z.shape[0], LANES), 1) < HEAD_DIM
    sums = []
    for c in range(z.shape[1] // LANES):
        blk = sq[:, c * LANES:(c + 1) * LANES]
        s_lo = jnp.sum(jnp.where(low, blk, 0.0), axis=-1, keepdims=True)
        s_hi = jnp.sum(jnp.where(low, 0.0, blk), axis=-1, keepdims=True)
        sums.append(jnp.where(low, s_lo, s_hi))
    ss = jnp.concatenate(sums, axis=-1)
    return z * lax.rsqrt(ss * (1.0 / HEAD_DIM) + EPS) * gain


def _proj_kernel(x_ref, shift_ref, scale_ref, ng_ref, w_ref, sg_ref, ws_ref, bs_ref,
                 qg_ref, kg_ref, t_ref, k_ref, v_ref, wb_ref):
    @pl.when((pl.program_id(0) == 0) & (pl.program_id(1) == 0))
    def _():
        for lo in range(0, D_IN, D_A):
            wb_ref[:, lo:lo + D_A] = w_ref[:, lo:lo + D_A].astype(wb_ref.dtype)

    for sub in range(TM // SUB_TM):
        _proj_rows(slice(sub * SUB_TM, (sub + 1) * SUB_TM), x_ref, shift_ref, scale_ref, ng_ref,
                   wb_ref, sg_ref, ws_ref, bs_ref, qg_ref, kg_ref, t_ref, k_ref, v_ref)


def _proj_rows(rows, x_ref, shift_ref, scale_ref, ng_ref, wb_ref, sg_ref, ws_ref, bs_ref,
               qg_ref, kg_ref, t_ref, k_ref, v_ref):
    hb = _prenorm(x_ref[0, rows], ng_ref[...], shift_ref[0], scale_ref[0])

    def zcols(lo, width):
        return _dot(hb, wb_ref[:, lo:lo + width])

    gu = _gelu(zcols(0, D_A))
    t_ref[0, rows, T_Q:T_Q + D_B] = _head_rms(zcols(3 * D_A, D_B), qg_ref[...]).astype(t_ref.dtype)
    gv = _gelu(zcols(D_A, D_A))
    k_ref[0, rows] = _head_rms(zcols(3 * D_A + D_B, D_B), kg_ref[...]).astype(k_ref.dtype)
    sa = _silu(zcols(2 * D_A, D_A))
    t_ref[0, rows, T_GB:T_GB + D_B] = _silu(zcols(3 * D_A + 3 * D_B, D_B)).astype(t_ref.dtype)
    v_ref[0, rows] = zcols(3 * D_A + 2 * D_B, D_B).astype(v_ref.dtype)

    for g in range(SGU_GROUPS):
        cs = slice(g * LANES, (g + 1) * LANES)
        vg = gv[:, cs]
        ms = jnp.mean(vg * vg, axis=-1, keepdims=True)
        vn = (vg * lax.rsqrt(ms + EPS) * sg_ref[:, cs]).astype(jnp.bfloat16)
        chunks = [slice(c * CHUNK, (c + 1) * CHUNK) for c in range(SUB_TM // CHUNK)]
        mixed = _dot(ws_ref[g].astype(jnp.bfloat16), jnp.concatenate([vn[rs] for rs in chunks], axis=1))
        for c, rs in enumerate(chunks):
            mixed_c = mixed[:, c * CHUNK:(c + 1) * CHUNK] + bs_ref[:, cs]
            out_rows = slice(rows.start + rs.start, rows.start + rs.stop)
            t_ref[0, out_rows, T_OA + g * LANES:T_OA + (g + 1) * LANES] = (
                gu[rs, cs] * mixed_c * sa[rs, cs]).astype(t_ref.dtype)


def _proj_call(x, mod, ng, w_in, sg, ws, bs, qg, kg):
    B, L, _ = x.shape
    const2 = lambda b, i: (0, 0)
    half_spec = pl.BlockSpec((1, TM, D_A), lambda b, i: (b, i, 0))
    half_shape = jax.ShapeDtypeStruct((B, L, D_A), jnp.bfloat16)
    return pl.pallas_call(
        _proj_kernel,
        grid=(B, L // TM),
        in_specs=[
            pl.BlockSpec((1, TM, D_MODEL), lambda b, i: (b, i, 0)),
            pl.BlockSpec((1, 1, D_MODEL), lambda b, i: (b, 0, MOD_SHIFT)),
            pl.BlockSpec((1, 1, D_MODEL), lambda b, i: (b, 0, MOD_SCALE)),
            pl.BlockSpec((1, D_MODEL), const2),
            pl.BlockSpec((D_MODEL, D_IN), const2),
            pl.BlockSpec((1, D_A), const2),
            pl.BlockSpec((SGU_GROUPS, CHUNK, CHUNK), lambda b, i: (0, 0, 0)),
            pl.BlockSpec((CHUNK, D_A), const2),
            pl.BlockSpec((1, D_B), const2),
            pl.BlockSpec((1, D_B), const2),
        ],
        out_specs=[pl.BlockSpec((1, TM, TOK_W), lambda b, i: (b, i, 0)), half_spec, half_spec],
        out_shape=[jax.ShapeDtypeStruct((B, L, TOK_W), jnp.bfloat16), half_shape, half_shape],
        scratch_shapes=[pltpu.VMEM((D_MODEL, D_IN), jnp.bfloat16)],
        compiler_params=pltpu.CompilerParams(
            dimension_semantics=("arbitrary", "arbitrary"), vmem_limit_bytes=VMEM_LIMIT),
        name="latent_proj",
    )(x, mod, mod, ng, w_in, sg, ws, bs, qg, kg)


def _ctx_kernel(x_ref, shift_ref, scale_ref, ng_ref, w_ref, kg_ref, k_ref, v_ref):
    hb = _prenorm(x_ref[...], ng_ref[...], shift_ref[0], scale_ref[0])
    wk = w_ref[:, 0:D_B].astype(jnp.bfloat16)
    wv = w_ref[:, D_B:2 * D_B].astype(jnp.bfloat16)
    k_ref[...] = _head_rms(_dot(hb, wk), kg_ref[...]).astype(k_ref.dtype)
    v_ref[...] = _dot(hb, wv).astype(v_ref.dtype)


def _ctx_call(ctx, mod, ng, w_in, kg):
    B, C, _ = ctx.shape
    const2 = lambda b: (0, 0)
    kv_block = (3 * D_A + D_B) // (2 * D_B)
    kv_spec = pl.BlockSpec((CTX_TM, D_B), lambda b: (b, 0))
    kv_shape = jax.ShapeDtypeStruct((B * C, D_B), jnp.bfloat16)
    kc, vc = pl.pallas_call(
        _ctx_kernel,
        grid=(B * C // CTX_TM,),
        in_specs=[
            pl.BlockSpec((CTX_TM, D_MODEL), lambda b: (b, 0)),
            pl.BlockSpec((1, 1, D_MODEL), lambda b: (B, 0, MOD_SHIFT)),
            pl.BlockSpec((1, 1, D_MODEL), lambda b: (B, 0, MOD_SCALE)),
            pl.BlockSpec((1, D_MODEL), const2),
            pl.BlockSpec((D_MODEL, 2 * D_B), lambda b: (0, kv_block)),
            pl.BlockSpec((1, D_B), const2),
        ],
        out_specs=[kv_spec, kv_spec],
        out_shape=[kv_shape, kv_shape],
        compiler_params=pltpu.CompilerParams(
            dimension_semantics=("arbitrary",), vmem_limit_bytes=VMEM_LIMIT),
        name="ctx_kv",
    )(ctx.reshape(B * C, D_MODEL), mod, mod, ng, w_in, kg)
    return kc.reshape(B, C, D_B), vc.reshape(B, C, D_B)


def _rpb_kernel(rpb_ref, win_ref, o_ref):
    n_heads, n_pairs = o_ref.shape[:2]
    n_dr = n_pairs + 1
    blk = o_ref.shape[2:]
    in_win = win_ref[...] != 0
    left_half = lax.broadcasted_iota(jnp.int32, blk, 1) < GRID_W
    first = LANES - (WIN_C - 1)
    for h in range(n_heads):
        for dr in range(n_pairs):
            row = h * n_dr + dr
            lo = pltpu.roll(jnp.broadcast_to(rpb_ref[row:row + 1, :], blk), first, 1,
                            stride=1, stride_axis=0)
            hi = pltpu.roll(jnp.broadcast_to(rpb_ref[row + 1:row + 2, :], blk),
                            (first + GRID_W) % LANES, 1, stride=1, stride_axis=0)
            o_ref[h, dr] = jnp.where(in_win, jnp.where(left_half, lo, hi) * LOG2E, NEG_INF)


def _rpb_call(rpb):
    H, n_dr, n_dc = rpb.shape
    cols = np.arange(GRID_W)
    c0 = np.clip(cols - WIN_C // 2, 0, GRID_W - WIN_C)
    in_win = (cols[None, :] >= c0[:, None]) & (cols[None, :] < c0[:, None] + WIN_C)
    win = jnp.asarray(np.tile(in_win, (1, LANES // GRID_W)), jnp.int32)
    rpb2 = jnp.pad(rpb.astype(jnp.float32).reshape(H * n_dr, n_dc), ((0, 0), (0, LANES - n_dc)))
    return pl.pallas_call(
        _rpb_kernel,
        out_shape=jax.ShapeDtypeStruct((H, n_dr - 1, GRID_W, LANES), jnp.float32),
        name="rpb_toeplitz",
    )(rpb2, win)


def _attn_kernel(x_ref, gate_ref, t_ref, k_ref, v_ref, kc_ref, vc_ref,
                 tab_ref, wout_ref, o_ref, mix_ref, s_ref, p_ref, l_ref, wob_ref):
    i = pl.program_id(1)

    @pl.when((pl.program_id(0) == 0) & (i == 0))
    def _():
        wob_ref[...] = wout_ref[...].astype(wob_ref.dtype)

    rows = pl.num_programs(1) * Q_ROWS
    shape = (GROUP_W, GROUP_W)
    row_head = lax.broadcasted_iota(jnp.int32, shape, 0) // HEAD_DIM
    lane_head = lax.broadcasted_iota(jnp.int32, shape, 1) // HEAD_DIM
    own_head = row_head == lane_head

    n_ctx = kc_ref.shape[1]
    groups = range(HEADS // HEAD_GROUP)

    def window(rho):
        r = i * Q_ROWS + rho
        r0 = jnp.clip(r - WIN_R // 2, 0, rows - WIN_R)
        d0 = r0 - r + (WIN_R - 1)
        return pl.multiple_of(r0 * GRID_W, GRID_W), d0

    def qrows(rho):
        return slice(rho * GRID_W, (rho + 1) * GRID_W)

    def scores(rho, slot):
        tok0, d0 = window(rho)
        for g in groups:
            lanes = slice(g * GROUP_W, (g + 1) * GROUP_W)
            qr = t_ref[0, qrows(rho), T_Q + g * GROUP_W:T_Q + (g + 1) * GROUP_W]
            q_bd = jnp.where(own_head, jnp.concatenate([qr] * HEAD_GROUP, axis=0),
                             jnp.zeros(shape, qr.dtype))
            bias = jnp.concatenate(
                [jnp.concatenate([tab_ref[g * HEAD_GROUP + h, d0 + 2 * t] for t in range(WIN_R // 2)],
                                 axis=1) for h in range(HEAD_GROUP)], axis=0)
            s_ref[slot, g, :, 0:TKW] = lax.dot_general(
                q_bd, k_ref[0, pl.ds(tok0, TKW), lanes], _NT,
                preferred_element_type=jnp.float32) + bias
            s_ref[slot, g, :, TKW:TKW + n_ctx] = lax.dot_general(
                q_bd, kc_ref[0, :, lanes], _NT, preferred_element_type=jnp.float32)

    def softmax(slot):
        for g in groups:
            s = s_ref[slot, g]
            p = jnp.exp2(s - jnp.max(s, axis=-1, keepdims=True))
            l_ref[slot, g] = jnp.sum(p, axis=-1, keepdims=True)
            p_ref[slot, g] = p.astype(p_ref.dtype)

    def values(rho, slot):
        tok0, _ = window(rho)
        for g in groups:
            lanes = slice(g * GROUP_W, (g + 1) * GROUP_W)
            o = (_dot(p_ref[slot, g, :, 0:TKW], v_ref[0, pl.ds(tok0, TKW), lanes])
                 + _dot(p_ref[slot, g, :, TKW:TKW + n_ctx], vc_ref[0, :, lanes]))
            o = jnp.where(own_head, o / l_ref[slot, g], 0.0)
            og = o[0:GRID_W]
            for h in range(1, HEAD_GROUP):
                og = og + o[h * GRID_W:(h + 1) * GRID_W]
            mix_ref[qrows(rho), D_A + g * GROUP_W:D_A + (g + 1) * GROUP_W] = (
                og * t_ref[0, qrows(rho), T_GB + g * GROUP_W:T_GB + (g + 1) * GROUP_W].astype(jnp.float32)
            ).astype(mix_ref.dtype)

    mix_ref[:, 0:D_A] = t_ref[0, :, T_OA:T_OA + D_A]
    scores(0, 0)
    for rho in range(Q_ROWS):
        if rho + 1 < Q_ROWS:
            scores(rho + 1, (rho + 1) % N_SLOTS)
        softmax(rho % N_SLOTS)
        values(rho, rho % N_SLOTS)
        if (rho + 1) % OUT_ROWS == 0:
            ts = slice((rho + 1 - OUT_ROWS) * GRID_W, (rho + 1) * GRID_W)
            o_ref[0, ts] = x_ref[0, ts] + gate_ref[0] * _dot(mix_ref[ts], wob_ref[...])


def _attn_call(x, mod, tok, k, v, kc, vc, tab, w_out):
    B, L, _ = x.shape
    C = kc.shape[1]
    n_groups = HEADS // HEAD_GROUP
    tok_spec = lambda width: pl.BlockSpec((1, TQ, width), lambda b, i: (b, i, 0))
    batch_spec = lambda n: pl.BlockSpec((1, n, D_B), lambda b, i: (b, 0, 0))
    return pl.pallas_call(
        _attn_kernel,
        grid=(B, L // TQ),
        in_specs=[
            tok_spec(D_MODEL),
            pl.BlockSpec((1, 1, D_MODEL), lambda b, i: (b, 0, MOD_GATE)),
            tok_spec(TOK_W),
            batch_spec(L),
            batch_spec(L),
            batch_spec(C),
            batch_spec(C),
            pl.BlockSpec(tab.shape, lambda b, i: (0, 0, 0, 0)),
            pl.BlockSpec((D_A + D_B, D_MODEL), lambda b, i: (0, 0)),
        ],
        out_specs=tok_spec(D_MODEL),
        out_shape=jax.ShapeDtypeStruct((B, L, D_MODEL), jnp.float32),
        scratch_shapes=[pltpu.VMEM((TQ, D_A + D_B), jnp.bfloat16),
                        pltpu.VMEM((N_SLOTS, n_groups, GROUP_W, TKW + C), jnp.float32),
                        pltpu.VMEM((N_SLOTS, n_groups, GROUP_W, TKW + C), jnp.bfloat16),
                        pltpu.VMEM((N_SLOTS, n_groups, GROUP_W, 1), jnp.float32),
                        pltpu.VMEM((D_A + D_B, D_MODEL), jnp.bfloat16)],
        compiler_params=pltpu.CompilerParams(
            dimension_semantics=("arbitrary", "arbitrary"), vmem_limit_bytes=VMEM_LIMIT),
        name="nbr_attn_out",
    )(x, mod, tok, k, v, kc, vc, tab, w_out)


def kernel(x, c, ctx, c_ctx, w_ada, b_ada, norm_g, w_in, sgu_norm_g, w_spatial, b_spatial,
           q_norm_g, k_norm_g, rpb, w_out):
    B, L, D = x.shape
    depth = w_ada.shape[0]
    cc = jnp.concatenate([c, c_ctx[None, :], jnp.zeros((ADA_ROWS - B - 1, D), jnp.float32)], axis=0)
    assert depth == 1
    for layer in range(depth):
        mod = _ada_call(cc, w_ada[layer], b_ada[layer][None, :])
        mod = mod.reshape(ADA_ROWS, 1, 3 * D)

        ng = norm_g[layer][None, :]
        kg = jnp.tile(k_norm_g[layer], HEADS)[None, :]
        qg = jnp.tile(q_norm_g[layer], HEADS)[None, :] * (HEAD_DIM ** -0.5 * LOG2E)
        bs = jnp.repeat(b_spatial[layer].T, D_A // SGU_GROUPS, axis=1)
        tok, k, v = _proj_call(
            x, mod, ng, w_in[layer], sgu_norm_g[layer][None, :],
            w_spatial[layer], bs, qg, kg)
        kc, vc = _ctx_call(ctx, mod, ng, w_in[layer], kg)
        tab = _rpb_call(rpb[layer])
        x = _attn_call(x, mod, tok, k, v, kc, vc, tab, w_out[layer])
    return x
```

```python
import numpy as np
import jax
import jax.numpy as jnp
from jax import lax
from jax.experimental import pallas as pl
from jax.experimental.pallas import tpu as pltpu

D_MODEL = 1024
GRID_W = 64
D_A = 512
D_B = 512
CHUNK = 128
SGU_GROUPS = 4
HEAD_DIM = 64
HEADS = 8
HEAD_GROUP = 4
GROUP_W = HEAD_GROUP * HEAD_DIM
WIN_R = 8
WIN_C = 16
D_IN = 3 * D_A + 4 * D_B
EPS = 1e-6
NEG_INF = -1e30
LOG2E = float(np.log2(np.e))

LANES = 128
VMEM_LIMIT = 60 * 1024 * 1024

TM = 1024
SUB_TM = 512
Q_ROWS = 16
OUT_ROWS = 4
N_SLOTS = 2
CTX_TM = 1024
TQ = Q_ROWS * GRID_W
TKW = WIN_R * GRID_W
ADA_ROWS = 16
ADA_BK = 256
MOD_SHIFT, MOD_SCALE, MOD_GATE = 0, 1, 2
T_OA, T_GB, T_Q, TOK_W = 0, D_A, D_A + D_B, D_A + 2 * D_B

_NT = (((1,), (1,)), ((), ()))


def _silu(x):
    half = 0.5 * x
    return half * (1.0 + jnp.tanh(half))


def _gelu(x):
    return 0.5 * x * (1.0 + lax.erf(x * np.float32(np.sqrt(0.5))))


def _dot(a, b):
    return jnp.dot(a, b, preferred_element_type=jnp.float32)


def _ada_kernel(c_ref, cctx_ref, w_ref, b_ref, o_ref):
    @pl.when(pl.program_id(0) == 0)
    def _():
        o_ref[...] = jnp.broadcast_to(b_ref[...], o_ref.shape)

    n_pad = ADA_ROWS - c_ref.shape[0] - 1
    cond = jnp.concatenate([c_ref[...], cctx_ref[...], jnp.zeros((n_pad, c_ref.shape[1]), jnp.float32)],
                           axis=0)
    a, w = _silu(cond), w_ref[...]
    a_hi, w_hi = a.astype(jnp.bfloat16), w.astype(jnp.bfloat16)
    a_lo = (a - a_hi.astype(jnp.float32)).astype(jnp.bfloat16)
    w_lo = (w - w_hi.astype(jnp.float32)).astype(jnp.bfloat16)
    o_ref[...] += _dot(a_hi, w_hi) + (_dot(a_lo, w_hi) + _dot(a_hi, w_lo))


def _ada_call(c, c_ctx, w_ada, b_ada):
    k, n = w_ada.shape
    B = c.shape[0]
    return pl.pallas_call(
        _ada_kernel,
        grid=(k // ADA_BK,),
        in_specs=[
            pl.BlockSpec((B, ADA_BK), lambda j: (0, j)),
            pl.BlockSpec((1, ADA_BK), lambda j: (0, j)),
            pl.BlockSpec((ADA_BK, n), lambda j: (j, 0)),
            pl.BlockSpec((1, n), lambda j: (0, 0)),
        ],
        out_specs=pl.BlockSpec((ADA_ROWS, n), lambda j: (0, 0)),
        out_shape=jax.ShapeDtypeStruct((ADA_ROWS, n), jnp.float32),
        compiler_params=pltpu.CompilerParams(dimension_semantics=("arbitrary",)),
        name="ada_params",
    )(c, c_ctx, w_ada, b_ada)


def _prenorm(x, g, shift, scale):
    ms = jnp.mean(x * x, axis=-1, keepdims=True)
    return (x * lax.rsqrt(ms + EPS) * (g * (1.0 + scale)) + shift).astype(jnp.bfloat16)


def _head_rms(z, head_gain):
    gain = jnp.concatenate([head_gain] * (z.shape[1] // HEAD_DIM), axis=-1)
    sq = z * z
    low = lax.broadcasted_iota(jnp.int32, (z.shape[0], LANES), 1) < HEAD_DIM
    sums = []
    for c in range(z.shape[1] // LANES):
        blk = sq[:, c * LANES:(c + 1) * LANES]
        s_lo = jnp.sum(jnp.where(low, blk, 0.0), axis=-1, keepdims=True)
        s_hi = jnp.sum(jnp.where(low, 0.0, blk), axis=-1, keepdims=True)
        sums.append(jnp.where(low, s_lo, s_hi))
    ss = jnp.concatenate(sums, axis=-1)
    return z * lax.rsqrt(ss * (1.0 / HEAD_DIM) + EPS) * gain


def _proj_kernel(x_ref, shift_ref, scale_ref, ng_ref, w_ref, sg_ref, ws_ref, bs_ref,
                 qg_ref, kg_ref, t_ref, k_ref, v_ref, wb_ref):
    @pl.when((pl.program_id(0) == 0) & (pl.program_id(1) == 0))
    def _():
        for lo in range(0, D_IN, D_A):
            wb_ref[:, lo:lo + D_A] = w_ref[:, lo:lo + D_A].astype(wb_ref.dtype)

    for sub in range(TM // SUB_TM):
        _proj_rows(slice(sub * SUB_TM, (sub + 1) * SUB_TM), x_ref, shift_ref, scale_ref, ng_ref,
                   wb_ref, sg_ref, ws_ref, bs_ref, qg_ref, kg_ref, t_ref, k_ref, v_ref)


def _proj_rows(rows, x_ref, shift_ref, scale_ref, ng_ref, wb_ref, sg_ref, ws_ref, bs_ref,
               qg_ref, kg_ref, t_ref, k_ref, v_ref):
    hb = _prenorm(x_ref[0, rows], ng_ref[...], shift_ref[0], scale_ref[0])

    def zcols(lo, width):
        return _dot(hb, wb_ref[:, lo:lo + width])

    gu = _gelu(zcols(0, D_A))
    t_ref[0, rows, T_Q:T_Q + D_B] = _head_rms(zcols(3 * D_A, D_B), qg_ref[...]).astype(t_ref.dtype)
    gv = _gelu(zcols(D_A, D_A))
    k_ref[0, rows] = _head_rms(zcols(3 * D_A + D_B, D_B), kg_ref[...]).astype(k_ref.dtype)
    sa = _silu(zcols(2 * D_A, D_A))
    t_ref[0, rows, T_GB:T_GB + D_B] = _silu(zcols(3 * D_A + 3 * D_B, D_B)).astype(t_ref.dtype)
    v_ref[0, rows] = zcols(3 * D_A + 2 * D_B, D_B).astype(v_ref.dtype)

    for g in range(SGU_GROUPS):
        cs = slice(g * LANES, (g + 1) * LANES)
        vg = gv[:, cs]
        ms = jnp.mean(vg * vg, axis=-1, keepdims=True)
        vn = (vg * lax.rsqrt(ms + EPS) * sg_ref[:, cs]).astype(jnp.bfloat16)
        chunks = [slice(c * CHUNK, (c + 1) * CHUNK) for c in range(SUB_TM // CHUNK)]
        mixed = _dot(ws_ref[g].astype(jnp.bfloat16), jnp.concatenate([vn[rs] for rs in chunks], axis=1))
        for c, rs in enumerate(chunks):
            mixed_c = mixed[:, c * CHUNK:(c + 1) * CHUNK] + bs_ref[:, cs]
            out_rows = slice(rows.start + rs.start, rows.start + rs.stop)
            t_ref[0, out_rows, T_OA + g * LANES:T_OA + (g + 1) * LANES] = (
                gu[rs, cs] * mixed_c * sa[rs, cs]).astype(t_ref.dtype)


def _proj_call(x, mod, ng, w_in, sg, ws, bs, qg, kg):
    B, L, _ = x.shape
    const2 = lambda b, i: (0, 0)
    half_spec = pl.BlockSpec((1, TM, D_A), lambda b, i: (b, i, 0))
    half_shape = jax.ShapeDtypeStruct((B, L, D_A), jnp.bfloat16)
    return pl.pallas_call(
        _proj_kernel,
        grid=(B, L // TM),
        in_specs=[
            pl.BlockSpec((1, TM, D_MODEL), lambda b, i: (b, i, 0)),
            pl.BlockSpec((1, 1, D_MODEL), lambda b, i: (b, 0, MOD_SHIFT)),
            pl.BlockSpec((1, 1, D_MODEL), lambda b, i: (b, 0, MOD_SCALE)),
            pl.BlockSpec((1, D_MODEL), const2),
            pl.BlockSpec((D_MODEL, D_IN), const2),
            pl.BlockSpec((1, D_A), const2),
            pl.BlockSpec((SGU_GROUPS, CHUNK, CHUNK), lambda b, i: (0, 0, 0)),
            pl.BlockSpec((CHUNK, D_A), const2),
            pl.BlockSpec((1, HEAD_DIM), const2),
            pl.BlockSpec((1, HEAD_DIM), const2),
        ],
        out_specs=[pl.BlockSpec((1, TM, TOK_W), lambda b, i: (b, i, 0)), half_spec, half_spec],
        out_shape=[jax.ShapeDtypeStruct((B, L, TOK_W), jnp.bfloat16), half_shape, half_shape],
        scratch_shapes=[pltpu.VMEM((D_MODEL, D_IN), jnp.bfloat16)],
        compiler_params=pltpu.CompilerParams(
            dimension_semantics=("arbitrary", "arbitrary"), vmem_limit_bytes=VMEM_LIMIT),
        name="latent_proj",
    )(x, mod, mod, ng, w_in, sg, ws, bs, qg, kg)


def _ctx_kernel(x_ref, shift_ref, scale_ref, ng_ref, w_ref, kg_ref, k_ref, v_ref):
    hb = _prenorm(x_ref[...], ng_ref[...], shift_ref[0], scale_ref[0])
    wk = w_ref[:, 0:D_B].astype(jnp.bfloat16)
    wv = w_ref[:, D_B:2 * D_B].astype(jnp.bfloat16)
    k_ref[...] = _head_rms(_dot(hb, wk), kg_ref[...]).astype(k_ref.dtype)
    v_ref[...] = _dot(hb, wv).astype(v_ref.dtype)


def _ctx_call(ctx, mod, ng, w_in, kg):
    B, C, _ = ctx.shape
    const2 = lambda b: (0, 0)
    kv_block = (3 * D_A + D_B) // (2 * D_B)
    kv_spec = pl.BlockSpec((CTX_TM, D_B), lambda b: (b, 0))
    kv_shape = jax.ShapeDtypeStruct((B * C, D_B), jnp.bfloat16)
    kc, vc = pl.pallas_call(
        _ctx_kernel,
        grid=(B * C // CTX_TM,),
        in_specs=[
            pl.BlockSpec((CTX_TM, D_MODEL), lambda b: (b, 0)),
            pl.BlockSpec((1, 1, D_MODEL), lambda b: (B, 0, MOD_SHIFT)),
            pl.BlockSpec((1, 1, D_MODEL), lambda b: (B, 0, MOD_SCALE)),
            pl.BlockSpec((1, D_MODEL), const2),
            pl.BlockSpec((D_MODEL, 2 * D_B), lambda b: (0, kv_block)),
            pl.BlockSpec((1, HEAD_DIM), const2),
        ],
        out_specs=[kv_spec, kv_spec],
        out_shape=[kv_shape, kv_shape],
        compiler_params=pltpu.CompilerParams(
            dimension_semantics=("arbitrary",), vmem_limit_bytes=VMEM_LIMIT),
        name="ctx_kv",
    )(ctx.reshape(B * C, D_MODEL), mod, mod, ng, w_in, kg)
    return kc.reshape(B, C, D_B), vc.reshape(B, C, D_B)


def _rpb_kernel(rpb_ref, win_ref, o_ref):
    n_heads, n_pairs = o_ref.shape[:2]
    n_dr = n_pairs + 1
    blk = o_ref.shape[2:]
    in_win = win_ref[...] != 0
    left_half = lax.broadcasted_iota(jnp.int32, blk, 1) < GRID_W
    first = LANES - (WIN_C - 1)
    for h in range(n_heads):
        for dr in range(n_pairs):
            row = h * n_dr + dr
            lo = pltpu.roll(jnp.broadcast_to(rpb_ref[row:row + 1, :], blk), first, 1,
                            stride=1, stride_axis=0)
            hi = pltpu.roll(jnp.broadcast_to(rpb_ref[row + 1:row + 2, :], blk),
                            (first + GRID_W) % LANES, 1, stride=1, stride_axis=0)
            o_ref[h, dr] = jnp.where(in_win, jnp.where(left_half, lo, hi) * LOG2E, NEG_INF)


def _rpb_call(rpb):
    H, n_dr, n_dc = rpb.shape
    cols = np.arange(GRID_W)
    c0 = np.clip(cols - WIN_C // 2, 0, GRID_W - WIN_C)
    in_win = (cols[None, :] >= c0[:, None]) & (cols[None, :] < c0[:, None] + WIN_C)
    win = jnp.asarray(np.tile(in_win, (1, LANES // GRID_W)), jnp.int32)
    rpb2 = jnp.pad(rpb.astype(jnp.float32).reshape(H * n_dr, n_dc), ((0, 0), (0, LANES - n_dc)))
    return pl.pallas_call(
        _rpb_kernel,
        out_shape=jax.ShapeDtypeStruct((H, n_dr - 1, GRID_W, LANES), jnp.float32),
        name="rpb_toeplitz",
    )(rpb2, win)


def _attn_kernel(x_ref, gate_ref, t_ref, k_ref, v_ref, kc_ref, vc_ref,
                 tab_ref, wout_ref, o_ref, mix_ref, s_ref, p_ref, l_ref, wob_ref):
    i = pl.program_id(1)

    @pl.when((pl.program_id(0) == 0) & (i == 0))
    def _():
        wob_ref[...] = wout_ref[...].astype(wob_ref.dtype)

    rows = pl.num_programs(1) * Q_ROWS
    shape = (GROUP_W, GROUP_W)
    row_head = lax.broadcasted_iota(jnp.int32, shape, 0) // HEAD_DIM
    lane_head = lax.broadcasted_iota(jnp.int32, shape, 1) // HEAD_DIM
    own_head = row_head == lane_head

    n_ctx = kc_ref.shape[1]
    groups = range(HEADS // HEAD_GROUP)

    def window(rho):
        r = i * Q_ROWS + rho
        r0 = jnp.clip(r - WIN_R // 2, 0, rows - WIN_R)
        d0 = r0 - r + (WIN_R - 1)
        return pl.multiple_of(r0 * GRID_W, GRID_W), d0

    def qrows(rho):
        return slice(rho * GRID_W, (rho + 1) * GRID_W)

    def scores(rho, slot):
        tok0, d0 = window(rho)
        for g in groups:
            lanes = slice(g * GROUP_W, (g + 1) * GROUP_W)
            qr = t_ref[0, qrows(rho), T_Q + g * GROUP_W:T_Q + (g + 1) * GROUP_W]
            q_bd = jnp.where(own_head, jnp.concatenate([qr] * HEAD_GROUP, axis=0),
                             jnp.zeros(shape, qr.dtype))
            bias = jnp.concatenate(
                [jnp.concatenate([tab_ref[g * HEAD_GROUP + h, d0 + 2 * t] for t in range(WIN_R // 2)],
                                 axis=1) for h in range(HEAD_GROUP)], axis=0)
            s_ref[slot, g, :, 0:TKW] = lax.dot_general(
                q_bd, k_ref[0, pl.ds(tok0, TKW), lanes], _NT,
                preferred_element_type=jnp.float32) + bias
            s_ref[slot, g, :, TKW:TKW + n_ctx] = lax.dot_general(
                q_bd, kc_ref[0, :, lanes], _NT, preferred_element_type=jnp.float32)

    def softmax(slot):
        for g in groups:
            s = s_ref[slot, g]
            p = jnp.exp2(s - jnp.max(s, axis=-1, keepdims=True))
            l_ref[slot, g] = jnp.sum(p, axis=-1, keepdims=True)
            p_ref[slot, g] = p.astype(p_ref.dtype)

    def values(rho, slot):
        tok0, _ = window(rho)
        for g in groups:
            lanes = slice(g * GROUP_W, (g + 1) * GROUP_W)
            o = (_dot(p_ref[slot, g, :, 0:TKW], v_ref[0, pl.ds(tok0, TKW), lanes])
                 + _dot(p_ref[slot, g, :, TKW:TKW + n_ctx], vc_ref[0, :, lanes]))
            o = jnp.where(own_head, o / l_ref[slot, g], 0.0)
            og = o[0:GRID_W]
            for h in range(1, HEAD_GROUP):
                og = og + o[h * GRID_W:(h + 1) * GRID_W]
            mix_ref[qrows(rho), D_A + g * GROUP_W:D_A + (g + 1) * GROUP_W] = (
                og * t_ref[0, qrows(rho), T_GB + g * GROUP_W:T_GB + (g + 1) * GROUP_W].astype(jnp.float32)
            ).astype(mix_ref.dtype)

    mix_ref[:, 0:D_A] = t_ref[0, :, T_OA:T_OA + D_A]
    scores(0, 0)
    for rho in range(Q_ROWS):
        if rho + 1 < Q_ROWS:
            scores(rho + 1, (rho + 1) % N_SLOTS)
        softmax(rho % N_SLOTS)
        values(rho, rho % N_SLOTS)
        if (rho + 1) % OUT_ROWS == 0:
            ts = slice((rho + 1 - OUT_ROWS) * GRID_W, (rho + 1) * GRID_W)
            o_ref[0, ts] = x_ref[0, ts] + gate_ref[0] * _dot(mix_ref[ts], wob_ref[...])


def _attn_call(x, mod, tok, k, v, kc, vc, tab, w_out):
    B, L, _ = x.shape
    C = kc.shape[1]
    n_groups = HEADS // HEAD_GROUP
    tok_spec = lambda width: pl.BlockSpec((1, TQ, width), lambda b, i: (b, i, 0))
    batch_spec = lambda n: pl.BlockSpec((1, n, D_B), lambda b, i: (b, 0, 0))
    return pl.pallas_call(
        _attn_kernel,
        grid=(B, L // TQ),
        in_specs=[
            tok_spec(D_MODEL),
            pl.BlockSpec((1, 1, D_MODEL), lambda b, i: (b, 0, MOD_GATE)),
            tok_spec(TOK_W),
            batch_spec(L),
            batch_spec(L),
            batch_spec(C),
            batch_spec(C),
            pl.BlockSpec(tab.shape, lambda b, i: (0, 0, 0, 0)),
            pl.BlockSpec((D_A + D_B, D_MODEL), lambda b, i: (0, 0)),
        ],
        out_specs=tok_spec(D_MODEL),
        out_shape=jax.ShapeDtypeStruct((B, L, D_MODEL), jnp.float32),
        scratch_shapes=[pltpu.VMEM((TQ, D_A + D_B), jnp.bfloat16),
                        pltpu.VMEM((N_SLOTS, n_groups, GROUP_W, TKW + C), jnp.float32),
                        pltpu.VMEM((N_SLOTS, n_groups, GROUP_W, TKW + C), jnp.bfloat16),
                        pltpu.VMEM((N_SLOTS, n_groups, GROUP_W, 1), jnp.float32),
                        pltpu.VMEM((D_A + D_B, D_MODEL), jnp.bfloat16)],
        compiler_params=pltpu.CompilerParams(
            dimension_semantics=("arbitrary", "arbitrary"), vmem_limit_bytes=VMEM_LIMIT),
        name="nbr_attn_out",
    )(x, mod, tok, k, v, kc, vc, tab, w_out)


def kernel(x, c, ctx, c_ctx, w_ada, b_ada, norm_g, w_in, sgu_norm_g, w_spatial, b_spatial,
           q_norm_g, k_norm_g, rpb, w_out):
    B, L, D = x.shape
    depth = w_ada.shape[0]
    assert depth == 1
    for layer in range(depth):
        mod = _ada_call(c, c_ctx[None, :], w_ada[layer], b_ada[layer][None, :])
        mod = mod.reshape(ADA_ROWS, 1, 3 * D)

        ng = norm_g[layer][None, :]
        kg = k_norm_g[layer][None, :]
        qg = q_norm_g[layer][None, :] * (HEAD_DIM ** -0.5 * LOG2E)
        bs = jnp.repeat(b_spatial[layer].T, D_A // SGU_GROUPS, axis=1)
        tok, k, v = _proj_call(
            x, mod, ng, w_in[layer], sgu_norm_g[layer][None, :],
            w_spatial[layer], bs, qg, kg)
        kc, vc = _ctx_call(ctx, mod, ng, w_in[layer], kg)
        tab = _rpb_call(rpb[layer])
        x = _attn_call(x, mod, tok, k, v, kc, vc, tab, w_out[layer])
    return x
```

```python
import numpy as np
import jax
import jax.numpy as jnp
from jax import lax
from jax.experimental import pallas as pl
from jax.experimental.pallas import tpu as pltpu

D_MODEL = 1024
GRID_W = 64
D_A = 512
D_B = 512
CHUNK = 128
SGU_GROUPS = 4
HEAD_DIM = 64
HEADS = 8
HEAD_GROUP = 4
GROUP_W = HEAD_GROUP * HEAD_DIM
WIN_R = 8
WIN_C = 16
D_IN = 3 * D_A + 4 * D_B
EPS = 1e-6
NEG_INF = -1e30
LOG2E = float(np.log2(np.e))
Q_SCALE = HEAD_DIM ** -0.5 * LOG2E

LANES = 128
VMEM_LIMIT = 60 * 1024 * 1024

TM = 1024
SUB_TM = 512
Q_ROWS = 16
OUT_ROWS = 4
N_SLOTS = 2
CTX_TM = 1024
TQ = Q_ROWS * GRID_W
TKW = WIN_R * GRID_W
ADA_ROWS = 16
ADA_BK = 256
MOD_SHIFT, MOD_SCALE, MOD_GATE = 0, 1, 2
MOD_BLK = 8
T_OA, T_GB, T_Q, TOK_W = 0, D_A, D_A + D_B, D_A + 2 * D_B

_NT = (((1,), (1,)), ((), ()))


def _silu(x):
    half = 0.5 * x
    return half * (1.0 + jnp.tanh(half))


def _gelu(x):
    return 0.5 * x * (1.0 + lax.erf(x * np.float32(np.sqrt(0.5))))


def _dot(a, b):
    return jnp.dot(a, b, preferred_element_type=jnp.float32)


def _ada_kernel(c_ref, cctx_ref, w_ref, b_ref, o_ref):
    @pl.when(pl.program_id(0) == 0)
    def _():
        o_ref[...] = jnp.broadcast_to(b_ref[...], o_ref.shape)

    n_pad = ADA_ROWS - c_ref.shape[0] - 1
    cond = jnp.concatenate([c_ref[...], cctx_ref[...], jnp.zeros((n_pad, c_ref.shape[1]), jnp.float32)],
                           axis=0)
    a, w = _silu(cond), w_ref[...]
    a_hi, w_hi = a.astype(jnp.bfloat16), w.astype(jnp.bfloat16)
    a_lo = (a - a_hi.astype(jnp.float32)).astype(jnp.bfloat16)
    w_lo = (w - w_hi.astype(jnp.float32)).astype(jnp.bfloat16)
    o_ref[...] += _dot(a_hi, w_hi) + (_dot(a_lo, w_hi) + _dot(a_hi, w_lo))


def _ada_call(c, c_ctx, w_ada, b_ada):
    k, n = w_ada.shape
    B = c.shape[0]
    return pl.pallas_call(
        _ada_kernel,
        grid=(k // ADA_BK,),
        in_specs=[
            pl.BlockSpec((B, ADA_BK), lambda j: (0, j)),
            pl.BlockSpec((1, ADA_BK), lambda j: (0, j)),
            pl.BlockSpec((ADA_BK, n), lambda j: (j, 0)),
            pl.BlockSpec((1, n), lambda j: (0, 0)),
        ],
        out_specs=pl.BlockSpec((ADA_ROWS, n), lambda j: (0, 0)),
        out_shape=jax.ShapeDtypeStruct((ADA_ROWS, n), jnp.float32),
        compiler_params=pltpu.CompilerParams(dimension_semantics=("arbitrary",)),
        name="ada_params",
    )(c, c_ctx, w_ada, b_ada)


def _prenorm(x, g, shift, scale):
    ms = jnp.mean(x * x, axis=-1, keepdims=True)
    return (x * lax.rsqrt(ms + EPS) * (g * (1.0 + scale)) + shift).astype(jnp.bfloat16)


def _head_rms(z, head_gain, scale=1.0):
    gain = jnp.concatenate([head_gain * scale] * (z.shape[1] // HEAD_DIM), axis=-1)
    sq = z * z
    low = lax.broadcasted_iota(jnp.int32, (z.shape[0], LANES), 1) < HEAD_DIM
    sums = []
    for c in range(z.shape[1] // LANES):
        blk = sq[:, c * LANES:(c + 1) * LANES]
        s_lo = jnp.sum(jnp.where(low, blk, 0.0), axis=-1, keepdims=True)
        s_hi = jnp.sum(jnp.where(low, 0.0, blk), axis=-1, keepdims=True)
        sums.append(jnp.where(low, s_lo, s_hi))
    ss = jnp.concatenate(sums, axis=-1)
    return z * lax.rsqrt(ss * (1.0 / HEAD_DIM) + EPS) * gain


def _proj_kernel(x_ref, shift_ref, scale_ref, ng_ref, w_ref, sg_ref, ws_ref, bs_ref,
                 qg_ref, kg_ref, t_ref, k_ref, v_ref, wb_ref):
    @pl.when((pl.program_id(0) == 0) & (pl.program_id(1) == 0))
    def _():
        for lo in range(0, D_IN, D_A):
            wb_ref[:, lo:lo + D_A] = w_ref[:, lo:lo + D_A].astype(wb_ref.dtype)

    row = pl.ds(pl.program_id(0) % MOD_BLK, 1)
    shift, scale = shift_ref[row, :], scale_ref[row, :]
    for sub in range(TM // SUB_TM):
        _proj_rows(slice(sub * SUB_TM, (sub + 1) * SUB_TM), x_ref, shift, scale, ng_ref,
                   wb_ref, sg_ref, ws_ref, bs_ref, qg_ref, kg_ref, t_ref, k_ref, v_ref)


def _proj_rows(rows, x_ref, shift, scale, ng_ref, wb_ref, sg_ref, ws_ref, bs_ref,
               qg_ref, kg_ref, t_ref, k_ref, v_ref):
    hb = _prenorm(x_ref[0, rows], ng_ref[...], shift, scale)

    def zcols(lo, width):
        return _dot(hb, wb_ref[:, lo:lo + width])

    gu = _gelu(zcols(0, D_A))
    t_ref[0, rows, T_Q:T_Q + D_B] = _head_rms(
        zcols(3 * D_A, D_B), qg_ref[...], Q_SCALE).astype(t_ref.dtype)
    gv = _gelu(zcols(D_A, D_A))
    k_ref[0, rows] = _head_rms(zcols(3 * D_A + D_B, D_B), kg_ref[...]).astype(k_ref.dtype)
    sa = _silu(zcols(2 * D_A, D_A))
    t_ref[0, rows, T_GB:T_GB + D_B] = _silu(zcols(3 * D_A + 3 * D_B, D_B)).astype(t_ref.dtype)
    v_ref[0, rows] = zcols(3 * D_A + 2 * D_B, D_B).astype(v_ref.dtype)

    for g in range(SGU_GROUPS):
        cs = slice(g * LANES, (g + 1) * LANES)
        vg = gv[:, cs]
        ms = jnp.mean(vg * vg, axis=-1, keepdims=True)
        vn = (vg * lax.rsqrt(ms + EPS) * sg_ref[:, cs]).astype(jnp.bfloat16)
        chunks = [slice(c * CHUNK, (c + 1) * CHUNK) for c in range(SUB_TM // CHUNK)]
        mixed = _dot(ws_ref[g].astype(jnp.bfloat16), jnp.concatenate([vn[rs] for rs in chunks], axis=1))
        for c, rs in enumerate(chunks):
            mixed_c = mixed[:, c * CHUNK:(c + 1) * CHUNK] + bs_ref[:, cs]
            out_rows = slice(rows.start + rs.start, rows.start + rs.stop)
            t_ref[0, out_rows, T_OA + g * LANES:T_OA + (g + 1) * LANES] = (
                gu[rs, cs] * mixed_c * sa[rs, cs]).astype(t_ref.dtype)


def _proj_call(x, mod, ng, w_in, sg, ws, bs, qg, kg):
    B, L, _ = x.shape
    const2 = lambda b, i: (0, 0)
    half_spec = pl.BlockSpec((1, TM, D_A), lambda b, i: (b, i, 0))
    half_shape = jax.ShapeDtypeStruct((B, L, D_A), jnp.bfloat16)
    return pl.pallas_call(
        _proj_kernel,
        grid=(B, L // TM),
        in_specs=[
            pl.BlockSpec((1, TM, D_MODEL), lambda b, i: (b, i, 0)),
            pl.BlockSpec((MOD_BLK, D_MODEL), lambda b, i: (b // MOD_BLK, MOD_SHIFT)),
            pl.BlockSpec((MOD_BLK, D_MODEL), lambda b, i: (b // MOD_BLK, MOD_SCALE)),
            pl.BlockSpec((1, D_MODEL), const2),
            pl.BlockSpec((D_MODEL, D_IN), const2),
            pl.BlockSpec((1, D_A), const2),
            pl.BlockSpec((SGU_GROUPS, CHUNK, CHUNK), lambda b, i: (0, 0, 0)),
            pl.BlockSpec((CHUNK, D_A), const2),
            pl.BlockSpec((1, HEAD_DIM), const2),
            pl.BlockSpec((1, HEAD_DIM), const2),
        ],
        out_specs=[pl.BlockSpec((1, TM, TOK_W), lambda b, i: (b, i, 0)), half_spec, half_spec],
        out_shape=[jax.ShapeDtypeStruct((B, L, TOK_W), jnp.bfloat16), half_shape, half_shape],
        scratch_shapes=[pltpu.VMEM((D_MODEL, D_IN), jnp.bfloat16)],
        compiler_params=pltpu.CompilerParams(
            dimension_semantics=("arbitrary", "arbitrary"), vmem_limit_bytes=VMEM_LIMIT),
        name="latent_proj",
    )(x, mod, mod, ng, w_in, sg, ws, bs, qg, kg)


def _ctx_kernel(x_ref, shift_ref, scale_ref, ng_ref, w_ref, kg_ref, k_ref, v_ref):
    hb = _prenorm(x_ref[...], ng_ref[...], shift_ref[0:1, :], scale_ref[0:1, :])
    wk = w_ref[:, 0:D_B].astype(jnp.bfloat16)
    wv = w_ref[:, D_B:2 * D_B].astype(jnp.bfloat16)
    k_ref[...] = _head_rms(_dot(hb, wk), kg_ref[...]).astype(k_ref.dtype)
    v_ref[...] = _dot(hb, wv).astype(v_ref.dtype)


def _ctx_call(ctx, mod, ng, w_in, kg):
    B, C, _ = ctx.shape
    assert B % MOD_BLK == 0
    const2 = lambda b: (0, 0)
    kv_block = (3 * D_A + D_B) // (2 * D_B)
    kv_spec = pl.BlockSpec((CTX_TM, D_B), lambda b: (b, 0))
    kv_shape = jax.ShapeDtypeStruct((B * C, D_B), jnp.bfloat16)
    kc, vc = pl.pallas_call(
        _ctx_kernel,
        grid=(B * C // CTX_TM,),
        in_specs=[
            pl.BlockSpec((CTX_TM, D_MODEL), lambda b: (b, 0)),
            pl.BlockSpec((MOD_BLK, D_MODEL), lambda b: (B // MOD_BLK, MOD_SHIFT)),
            pl.BlockSpec((MOD_BLK, D_MODEL), lambda b: (B // MOD_BLK, MOD_SCALE)),
            pl.BlockSpec((1, D_MODEL), const2),
            pl.BlockSpec((D_MODEL, 2 * D_B), lambda b: (0, kv_block)),
            pl.BlockSpec((1, HEAD_DIM), const2),
        ],
        out_specs=[kv_spec, kv_spec],
        out_shape=[kv_shape, kv_shape],
        compiler_params=pltpu.CompilerParams(
            dimension_semantics=("arbitrary",), vmem_limit_bytes=VMEM_LIMIT),
        name="ctx_kv",
    )(ctx.reshape(B * C, D_MODEL), mod, mod, ng, w_in, kg)
    return kc.reshape(B, C, D_B), vc.reshape(B, C, D_B)


def _rpb_kernel(rpb_ref, win_ref, o_ref):
    n_heads, n_pairs = o_ref.shape[:2]
    n_dr = n_pairs + 1
    blk = o_ref.shape[2:]
    in_win = win_ref[...] != 0
    left_half = lax.broadcasted_iota(jnp.int32, blk, 1) < GRID_W
    first = LANES - (WIN_C - 1)
    for h in range(n_heads):
        for dr in range(n_pairs):
            row = h * n_dr + dr
            lo = pltpu.roll(jnp.broadcast_to(rpb_ref[row:row + 1, :], blk), first, 1,
                            stride=1, stride_axis=0)
            hi = pltpu.roll(jnp.broadcast_to(rpb_ref[row + 1:row + 2, :], blk),
                            (first + GRID_W) % LANES, 1, stride=1, stride_axis=0)
            o_ref[h, dr] = jnp.where(in_win, jnp.where(left_half, lo, hi) * LOG2E, NEG_INF)


def _rpb_call(rpb):
    H, n_dr, n_dc = rpb.shape
    cols = np.arange(GRID_W)
    c0 = np.clip(cols - WIN_C // 2, 0, GRID_W - WIN_C)
    in_win = (cols[None, :] >= c0[:, None]) & (cols[None, :] < c0[:, None] + WIN_C)
    win = jnp.asarray(np.tile(in_win, (1, LANES // GRID_W)), jnp.int32)
    rpb2 = jnp.pad(rpb.astype(jnp.float32).reshape(H * n_dr, n_dc), ((0, 0), (0, LANES - n_dc)))
    return pl.pallas_call(
        _rpb_kernel,
        out_shape=jax.ShapeDtypeStruct((H, n_dr - 1, GRID_W, LANES), jnp.float32),
        name="rpb_toeplitz",
    )(rpb2, win)


def _attn_kernel(x_ref, gate_ref, t_ref, k_ref, v_ref, kc_ref, vc_ref,
                 tab_ref, wout_ref, o_ref, mix_ref, s_ref, p_ref, l_ref, wob_ref):
    i = pl.program_id(1)

    @pl.when((pl.program_id(0) == 0) & (i == 0))
    def _():
        wob_ref[...] = wout_ref[...].astype(wob_ref.dtype)

    gate = gate_ref[pl.ds(pl.program_id(0) % MOD_BLK, 1), :]
    rows = pl.num_programs(1) * Q_ROWS
    shape = (GROUP_W, GROUP_W)
    row_head = lax.broadcasted_iota(jnp.int32, shape, 0) // HEAD_DIM
    lane_head = lax.broadcasted_iota(jnp.int32, shape, 1) // HEAD_DIM
    own_head = row_head == lane_head

    n_ctx = kc_ref.shape[1]
    groups = range(HEADS // HEAD_GROUP)

    def window(rho):
        r = i * Q_ROWS + rho
        r0 = jnp.clip(r - WIN_R // 2, 0, rows - WIN_R)
        d0 = r0 - r + (WIN_R - 1)
        return pl.multiple_of(r0 * GRID_W, GRID_W), d0

    def qrows(rho):
        return slice(rho * GRID_W, (rho + 1) * GRID_W)

    def scores(rho, slot):
        tok0, d0 = window(rho)
        for g in groups:
            lanes = slice(g * GROUP_W, (g + 1) * GROUP_W)
            qr = t_ref[0, qrows(rho), T_Q + g * GROUP_W:T_Q + (g + 1) * GROUP_W]
            q_bd = jnp.where(own_head, jnp.concatenate([qr] * HEAD_GROUP, axis=0),
                             jnp.zeros(shape, qr.dtype))
            bias = jnp.concatenate(
                [jnp.concatenate([tab_ref[g * HEAD_GROUP + h, d0 + 2 * t] for t in range(WIN_R // 2)],
                                 axis=1) for h in range(HEAD_GROUP)], axis=0)
            s_ref[slot, g, :, 0:TKW] = lax.dot_general(
                q_bd, k_ref[0, pl.ds(tok0, TKW), lanes], _NT,
                preferred_element_type=jnp.float32) + bias
            s_ref[slot, g, :, TKW:TKW + n_ctx] = lax.dot_general(
                q_bd, kc_ref[0, :, lanes], _NT, preferred_element_type=jnp.float32)

    def softmax(slot):
        for g in groups:
            s = s_ref[slot, g]
            p = jnp.exp2(s - jnp.max(s, axis=-1, keepdims=True))
            l_ref[slot, g] = jnp.sum(p, axis=-1, keepdims=True)
            p_ref[slot, g] = p.astype(p_ref.dtype)

    def values(rho, slot):
        tok0, _ = window(rho)
        for g in groups:
            lanes = slice(g * GROUP_W, (g + 1) * GROUP_W)
            o = (_dot(p_ref[slot, g, :, 0:TKW], v_ref[0, pl.ds(tok0, TKW), lanes])
                 + _dot(p_ref[slot, g, :, TKW:TKW + n_ctx], vc_ref[0, :, lanes]))
            o = jnp.where(own_head, o / l_ref[slot, g], 0.0)
            og = o[0:GRID_W]
            for h in range(1, HEAD_GROUP):
                og = og + o[h * GRID_W:(h + 1) * GRID_W]
            mix_ref[qrows(rho), D_A + g * GROUP_W:D_A + (g + 1) * GROUP_W] = (
                og * t_ref[0, qrows(rho), T_GB + g * GROUP_W:T_GB + (g + 1) * GROUP_W].astype(jnp.float32)
            ).astype(mix_ref.dtype)

    mix_ref[:, 0:D_A] = t_ref[0, :, T_OA:T_OA + D_A]
    scores(0, 0)
    for rho in range(Q_ROWS):
        if rho + 1 < Q_ROWS:
            scores(rho + 1, (rho + 1) % N_SLOTS)
        softmax(rho % N_SLOTS)
        values(rho, rho % N_SLOTS)
        if (rho + 1) % OUT_ROWS == 0:
            ts = slice((rho + 1 - OUT_ROWS) * GRID_W, (rho + 1) * GRID_W)
            o_ref[0, ts] = x_ref[0, ts] + gate * _dot(mix_ref[ts], wob_ref[...])


def _attn_call(x, mod, tok, k, v, kc, vc, tab, w_out):
    B, L, _ = x.shape
    C = kc.shape[1]
    n_groups = HEADS // HEAD_GROUP
    tok_spec = lambda width: pl.BlockSpec((1, TQ, width), lambda b, i: (b, i, 0))
    batch_spec = lambda n: pl.BlockSpec((1, n, D_B), lambda b, i: (b, 0, 0))
    return pl.pallas_call(
        _attn_kernel,
        grid=(B, L // TQ),
        in_specs=[
            tok_spec(D_MODEL),
            pl.BlockSpec((MOD_BLK, D_MODEL), lambda b, i: (b // MOD_BLK, MOD_GATE)),
            tok_spec(TOK_W),
            batch_spec(L),
            batch_spec(L),
            batch_spec(C),
            batch_spec(C),
            pl.BlockSpec(tab.shape, lambda b, i: (0, 0, 0, 0)),
            pl.BlockSpec((D_A + D_B, D_MODEL), lambda b, i: (0, 0)),
        ],
        out_specs=tok_spec(D_MODEL),
        out_shape=jax.ShapeDtypeStruct((B, L, D_MODEL), jnp.float32),
        scratch_shapes=[pltpu.VMEM((TQ, D_A + D_B), jnp.bfloat16),
                        pltpu.VMEM((N_SLOTS, n_groups, GROUP_W, TKW + C), jnp.float32),
                        pltpu.VMEM((N_SLOTS, n_groups, GROUP_W, TKW + C), jnp.bfloat16),
                        pltpu.VMEM((N_SLOTS, n_groups, GROUP_W, 1), jnp.float32),
                        pltpu.VMEM((D_A + D_B, D_MODEL), jnp.bfloat16)],
        compiler_params=pltpu.CompilerParams(
            dimension_semantics=("arbitrary", "arbitrary"), vmem_limit_bytes=VMEM_LIMIT),
        name="nbr_attn_out",
    )(x, mod, tok, k, v, kc, vc, tab, w_out)


def kernel(x, c, ctx, c_ctx, w_ada, b_ada, norm_g, w_in, sgu_norm_g, w_spatial, b_spatial,
           q_norm_g, k_norm_g, rpb, w_out):
    depth = w_ada.shape[0]
    assert depth == 1
    for layer in range(depth):
        mod = _ada_call(c, c_ctx[None, :], w_ada[layer], b_ada[layer][None, :])

        ng = norm_g[layer][None, :]
        kg = k_norm_g[layer][None, :]
        qg = q_norm_g[layer][None, :]
        bs = jnp.repeat(b_spatial[layer].T, D_A // SGU_GROUPS, axis=1)
        tok, k, v = _proj_call(
            x, mod, ng, w_in[layer], sgu_norm_g[layer][None, :],
            w_spatial[layer], bs, qg, kg)
        kc, vc = _ctx_call(ctx, mod, ng, w_in[layer], kg)
        tab = _rpb_call(rpb[layer])
        x = _attn_call(x, mod, tok, k, v, kc, vc, tab, w_out[layer])
    return x
```

```python
import numpy as np
import jax
import jax.numpy as jnp
from jax import lax
from jax.experimental import pallas as pl
from jax.experimental.pallas import tpu as pltpu

D_MODEL = 1024
GRID_W = 64
D_A = 512
D_B = 512
CHUNK = 128
SGU_GROUPS = 4
HEAD_DIM = 64
HEADS = 8
HEAD_GROUP = 4
GROUP_W = HEAD_GROUP * HEAD_DIM
WIN_R = 8
WIN_C = 16
D_IN = 3 * D_A + 4 * D_B
EPS = 1e-6
NEG_INF = -1e30
LOG2E = float(np.log2(np.e))
Q_SCALE = HEAD_DIM ** -0.5 * LOG2E

LANES = 128
VMEM_LIMIT = 60 * 1024 * 1024

TM = 1024
SUB_TM = 512
Q_ROWS = 16
OUT_ROWS = 4
N_SLOTS = 2
CTX_TM = 1024
TQ = Q_ROWS * GRID_W
TKW = WIN_R * GRID_W
ADA_ROWS = 16
ADA_BK = 256
MOD_SHIFT, MOD_SCALE, MOD_GATE = 0, 1, 2
MOD_BLK = 8
T_OA, T_GB, T_Q, TOK_W = 0, D_A, D_A + D_B, D_A + 2 * D_B

_NT = (((1,), (1,)), ((), ()))


def _silu(x):
    half = 0.5 * x
    return half * (1.0 + jnp.tanh(half))


def _gelu(x):
    return 0.5 * x * (1.0 + lax.erf(x * np.float32(np.sqrt(0.5))))


def _dot(a, b):
    return jnp.dot(a, b, preferred_element_type=jnp.float32)


def _ada_kernel(c_ref, cctx_ref, w_ref, b_ref, o_ref):
    @pl.when(pl.program_id(0) == 0)
    def _():
        o_ref[...] = jnp.broadcast_to(b_ref[...], o_ref.shape)

    n_pad = ADA_ROWS - c_ref.shape[0] - 1
    cond = jnp.concatenate([c_ref[...], cctx_ref[...], jnp.zeros((n_pad, c_ref.shape[1]), jnp.float32)],
                           axis=0)
    a, w = _silu(cond), w_ref[...]
    a_hi, w_hi = a.astype(jnp.bfloat16), w.astype(jnp.bfloat16)
    a_lo = (a - a_hi.astype(jnp.float32)).astype(jnp.bfloat16)
    w_lo = (w - w_hi.astype(jnp.float32)).astype(jnp.bfloat16)
    o_ref[...] += _dot(a_hi, w_hi) + (_dot(a_lo, w_hi) + _dot(a_hi, w_lo))


def _ada_call(c, c_ctx, w_ada, b_ada):
    k, n = w_ada.shape
    B = c.shape[0]
    return pl.pallas_call(
        _ada_kernel,
        grid=(k // ADA_BK,),
        in_specs=[
            pl.BlockSpec((B, ADA_BK), lambda j: (0, j)),
            pl.BlockSpec((1, ADA_BK), lambda j: (0, j)),
            pl.BlockSpec((ADA_BK, n), lambda j: (j, 0)),
            pl.BlockSpec((1, n), lambda j: (0, 0)),
        ],
        out_specs=pl.BlockSpec((ADA_ROWS, n), lambda j: (0, 0)),
        out_shape=jax.ShapeDtypeStruct((ADA_ROWS, n), jnp.float32),
        compiler_params=pltpu.CompilerParams(dimension_semantics=("arbitrary",)),
        name="ada_params",
    )(c, c_ctx, w_ada, b_ada)


def _prenorm(x, g, shift, scale):
    ms = jnp.mean(x * x, axis=-1, keepdims=True)
    return (x * lax.rsqrt(ms + EPS) * (g * (1.0 + scale)) + shift).astype(jnp.bfloat16)


def _head_rms(z, head_gain, scale=1.0):
    gain = jnp.concatenate([head_gain * scale] * (z.shape[1] // HEAD_DIM), axis=-1)
    sq = z * z
    low = lax.broadcasted_iota(jnp.int32, (z.shape[0], LANES), 1) < HEAD_DIM
    sums = []
    for c in range(z.shape[1] // LANES):
        blk = sq[:, c * LANES:(c + 1) * LANES]
        s_lo = jnp.sum(jnp.where(low, blk, 0.0), axis=-1, keepdims=True)
        s_hi = jnp.sum(jnp.where(low, 0.0, blk), axis=-1, keepdims=True)
        sums.append(jnp.where(low, s_lo, s_hi))
    ss = jnp.concatenate(sums, axis=-1)
    return z * lax.rsqrt(ss * (1.0 / HEAD_DIM) + EPS) * gain


def _proj_kernel(x_ref, shift_ref, scale_ref, ng_ref, w_ref, sg_ref, ws_ref, bs_ref,
                 qg_ref, kg_ref, t_ref, k_ref, v_ref, wb_ref):
    @pl.when((pl.program_id(0) == 0) & (pl.program_id(1) == 0))
    def _():
        for lo in range(0, D_IN, D_A):
            wb_ref[:, lo:lo + D_A] = w_ref[:, lo:lo + D_A].astype(wb_ref.dtype)

    row = pl.ds(pl.program_id(0) % MOD_BLK, 1)
    shift, scale = shift_ref[row, :], scale_ref[row, :]
    for sub in range(TM // SUB_TM):
        _proj_rows(slice(sub * SUB_TM, (sub + 1) * SUB_TM), x_ref, shift, scale, ng_ref,
                   wb_ref, sg_ref, ws_ref, bs_ref, qg_ref, kg_ref, t_ref, k_ref, v_ref)


def _proj_rows(rows, x_ref, shift, scale, ng_ref, wb_ref, sg_ref, ws_ref, bs_ref,
               qg_ref, kg_ref, t_ref, k_ref, v_ref):
    hb = _prenorm(x_ref[0, rows], ng_ref[...], shift, scale)

    def zcols(lo, width):
        return _dot(hb, wb_ref[:, lo:lo + width])

    gu = _gelu(zcols(0, D_A))
    t_ref[0, rows, T_Q:T_Q + D_B] = _head_rms(
        zcols(3 * D_A, D_B), qg_ref[...], Q_SCALE).astype(t_ref.dtype)
    gv = _gelu(zcols(D_A, D_A))
    k_ref[0, rows] = _head_rms(zcols(3 * D_A + D_B, D_B), kg_ref[...]).astype(k_ref.dtype)
    sa = _silu(zcols(2 * D_A, D_A))
    t_ref[0, rows, T_GB:T_GB + D_B] = _silu(zcols(3 * D_A + 3 * D_B, D_B)).astype(t_ref.dtype)
    v_ref[0, rows] = zcols(3 * D_A + 2 * D_B, D_B).astype(v_ref.dtype)

    for g in range(SGU_GROUPS):
        cs = slice(g * LANES, (g + 1) * LANES)
        vg = gv[:, cs]
        ms = jnp.mean(vg * vg, axis=-1, keepdims=True)
        vn = (vg * lax.rsqrt(ms + EPS) * sg_ref[:, cs]).astype(jnp.bfloat16)
        chunks = [slice(c * CHUNK, (c + 1) * CHUNK) for c in range(SUB_TM // CHUNK)]
        mixed = _dot(ws_ref[g].astype(jnp.bfloat16), jnp.concatenate([vn[rs] for rs in chunks], axis=1))
        for c, rs in enumerate(chunks):
            mixed_c = mixed[:, c * CHUNK:(c + 1) * CHUNK] + bs_ref[:, cs]
            out_rows = slice(rows.start + rs.start, rows.start + rs.stop)
            t_ref[0, out_rows, T_OA + g * LANES:T_OA + (g + 1) * LANES] = (
                gu[rs, cs] * mixed_c * sa[rs, cs]).astype(t_ref.dtype)


def _proj_call(x, mod, ng, w_in, sg, ws, bs, qg, kg):
    B, L, _ = x.shape
    const2 = lambda b, i: (0, 0)
    half_spec = pl.BlockSpec((1, TM, D_A), lambda b, i: (b, i, 0))
    half_shape = jax.ShapeDtypeStruct((B, L, D_A), jnp.bfloat16)
    return pl.pallas_call(
        _proj_kernel,
        grid=(B, L // TM),
        in_specs=[
            pl.BlockSpec((1, TM, D_MODEL), lambda b, i: (b, i, 0)),
            pl.BlockSpec((MOD_BLK, D_MODEL), lambda b, i: (b // MOD_BLK, MOD_SHIFT)),
            pl.BlockSpec((MOD_BLK, D_MODEL), lambda b, i: (b // MOD_BLK, MOD_SCALE)),
            pl.BlockSpec((1, D_MODEL), const2),
            pl.BlockSpec((D_MODEL, D_IN), const2),
            pl.BlockSpec((1, D_A), const2),
            pl.BlockSpec((SGU_GROUPS, CHUNK, CHUNK), lambda b, i: (0, 0, 0)),
            pl.BlockSpec((CHUNK, D_A), const2),
            pl.BlockSpec((1, HEAD_DIM), const2),
            pl.BlockSpec((1, HEAD_DIM), const2),
        ],
        out_specs=[pl.BlockSpec((1, TM, TOK_W), lambda b, i: (b, i, 0)), half_spec, half_spec],
        out_shape=[jax.ShapeDtypeStruct((B, L, TOK_W), jnp.bfloat16), half_shape, half_shape],
        scratch_shapes=[pltpu.VMEM((D_MODEL, D_IN), jnp.bfloat16)],
        compiler_params=pltpu.CompilerParams(
            dimension_semantics=("arbitrary", "arbitrary"), vmem_limit_bytes=VMEM_LIMIT),
        name="latent_proj",
    )(x, mod, mod, ng, w_in, sg, ws, bs, qg, kg)


def _ctx_kernel(x_ref, shift_ref, scale_ref, ng_ref, w_ref, kg_ref, k_ref, v_ref):
    hb = _prenorm(x_ref[...], ng_ref[...], shift_ref[0:1, :], scale_ref[0:1, :])
    wk = w_ref[:, 0:D_B].astype(jnp.bfloat16)
    wv = w_ref[:, D_B:2 * D_B].astype(jnp.bfloat16)
    k_ref[...] = _head_rms(_dot(hb, wk), kg_ref[...]).astype(k_ref.dtype)
    v_ref[...] = _dot(hb, wv).astype(v_ref.dtype)


def _ctx_call(ctx, mod, ng, w_in, kg):
    B, C, _ = ctx.shape
    assert B % MOD_BLK == 0
    const2 = lambda b: (0, 0)
    kv_block = (3 * D_A + D_B) // (2 * D_B)
    kv_spec = pl.BlockSpec((CTX_TM, D_B), lambda b: (b, 0))
    kv_shape = jax.ShapeDtypeStruct((B * C, D_B), jnp.bfloat16)
    kc, vc = pl.pallas_call(
        _ctx_kernel,
        grid=(B * C // CTX_TM,),
        in_specs=[
            pl.BlockSpec((CTX_TM, D_MODEL), lambda b: (b, 0)),
            pl.BlockSpec((MOD_BLK, D_MODEL), lambda b: (B // MOD_BLK, MOD_SHIFT)),
            pl.BlockSpec((MOD_BLK, D_MODEL), lambda b: (B // MOD_BLK, MOD_SCALE)),
            pl.BlockSpec((1, D_MODEL), const2),
            pl.BlockSpec((D_MODEL, 2 * D_B), lambda b: (0, kv_block)),
            pl.BlockSpec((1, HEAD_DIM), const2),
        ],
        out_specs=[kv_spec, kv_spec],
        out_shape=[kv_shape, kv_shape],
        compiler_params=pltpu.CompilerParams(
            dimension_semantics=("arbitrary",), vmem_limit_bytes=VMEM_LIMIT),
        name="ctx_kv",
    )(ctx.reshape(B * C, D_MODEL), mod, mod, ng, w_in, kg)
    return kc.reshape(B, C, D_B), vc.reshape(B, C, D_B)


def _rpb_kernel(rpb_ref, win_ref, o_ref):
    n_heads, n_pairs = o_ref.shape[:2]
    n_dr = n_pairs + 1
    blk = o_ref.shape[2:]
    in_win = win_ref[...] != 0
    left_half = lax.broadcasted_iota(jnp.int32, blk, 1) < GRID_W
    first = LANES - (WIN_C - 1)
    for h in range(n_heads):
        for dr in range(n_pairs):
            row = h * n_dr + dr
            lo = pltpu.roll(jnp.broadcast_to(rpb_ref[row:row + 1, :], blk), first, 1,
                            stride=1, stride_axis=0)
            hi = pltpu.roll(jnp.broadcast_to(rpb_ref[row + 1:row + 2, :], blk),
                            (first + GRID_W) % LANES, 1, stride=1, stride_axis=0)
            o_ref[h, dr] = jnp.where(in_win, jnp.where(left_half, lo, hi) * LOG2E, NEG_INF)


def _rpb_call(rpb):
    H, n_dr, n_dc = rpb.shape
    cols = np.arange(GRID_W)
    c0 = np.clip(cols - WIN_C // 2, 0, GRID_W - WIN_C)
    in_win = (cols[None, :] >= c0[:, None]) & (cols[None, :] < c0[:, None] + WIN_C)
    win = jnp.asarray(np.tile(in_win, (1, LANES // GRID_W)), jnp.int32)
    rpb2 = jnp.pad(rpb.astype(jnp.float32).reshape(H * n_dr, n_dc), ((0, 0), (0, LANES - n_dc)))
    return pl.pallas_call(
        _rpb_kernel,
        out_shape=jax.ShapeDtypeStruct((H, n_dr - 1, GRID_W, LANES), jnp.float32),
        name="rpb_toeplitz",
    )(rpb2, win)


def _attn_kernel(x_ref, gate_ref, t_ref, k_ref, v_ref, kc_ref, vc_ref,
                 tab_ref, wout_ref, o_ref, mix_ref, s_ref, p_ref, l_ref, wob_ref):
    i = pl.program_id(1)

    @pl.when((pl.program_id(0) == 0) & (i == 0))
    def _():
        wob_ref[...] = wout_ref[...].astype(wob_ref.dtype)

    gate = gate_ref[pl.ds(pl.program_id(0) % MOD_BLK, 1), :]
    rows = pl.num_programs(1) * Q_ROWS
    shape = (GROUP_W, GROUP_W)
    row_head = lax.broadcasted_iota(jnp.int32, shape, 0) // HEAD_DIM
    lane_head = lax.broadcasted_iota(jnp.int32, shape, 1) // HEAD_DIM
    own_head = row_head == lane_head

    n_ctx = kc_ref.shape[1]
    groups = range(HEADS // HEAD_GROUP)

    def window(rho):
        r = i * Q_ROWS + rho
        r0 = jnp.clip(r - WIN_R // 2, 0, rows - WIN_R)
        d0 = r0 - r + (WIN_R - 1)
        return pl.multiple_of(r0 * GRID_W, GRID_W), d0

    def qrows(rho):
        return slice(rho * GRID_W, (rho + 1) * GRID_W)

    def scores(rho, slot):
        tok0, d0 = window(rho)
        for g in groups:
            lanes = slice(g * GROUP_W, (g + 1) * GROUP_W)
            qr = t_ref[0, qrows(rho), T_Q + g * GROUP_W:T_Q + (g + 1) * GROUP_W]
            q_bd = jnp.where(own_head, jnp.concatenate([qr] * HEAD_GROUP, axis=0),
                             jnp.zeros(shape, qr.dtype))
            bias = jnp.concatenate(
                [jnp.concatenate([tab_ref[g * HEAD_GROUP + h, d0 + 2 * t] for t in range(WIN_R // 2)],
                                 axis=1) for h in range(HEAD_GROUP)], axis=0)
            s_ref[slot, g, :, 0:TKW] = lax.dot_general(
                q_bd, k_ref[0, pl.ds(tok0, TKW), lanes], _NT,
                preferred_element_type=jnp.float32) + bias
            s_ref[slot, g, :, TKW:TKW + n_ctx] = lax.dot_general(
                q_bd, kc_ref[0, :, lanes], _NT, preferred_element_type=jnp.float32)

    def softmax(slot):
        for g in groups:
            s = s_ref[slot, g]
            p = jnp.exp2(s - jnp.max(s, axis=-1, keepdims=True))
            l_ref[slot, g] = jnp.sum(p, axis=-1, keepdims=True)
            p_ref[slot, g] = p.astype(p_ref.dtype)

    def values(rho, slot):
        tok0, _ = window(rho)
        for g in groups:
            lanes = slice(g * GROUP_W, (g + 1) * GROUP_W)
            o = (_dot(p_ref[slot, g, :, 0:TKW], v_ref[0, pl.ds(tok0, TKW), lanes])
                 + _dot(p_ref[slot, g, :, TKW:TKW + n_ctx], vc_ref[0, :, lanes]))
            o = jnp.where(own_head, o / l_ref[slot, g], 0.0)
            og = o[0:GRID_W]
            for h in range(1, HEAD_GROUP):
                og = og + o[h * GRID_W:(h + 1) * GRID_W]
            mix_ref[qrows(rho), g * GROUP_W:(g + 1) * GROUP_W] = (
                og * t_ref[0, qrows(rho), T_GB + g * GROUP_W:T_GB + (g + 1) * GROUP_W].astype(jnp.float32)
            ).astype(mix_ref.dtype)

    scores(0, 0)
    for rho in range(Q_ROWS):
        if rho + 1 < Q_ROWS:
            scores(rho + 1, (rho + 1) % N_SLOTS)
        softmax(rho % N_SLOTS)
        values(rho, rho % N_SLOTS)
        if (rho + 1) % OUT_ROWS == 0:
            ts = slice((rho + 1 - OUT_ROWS) * GRID_W, (rho + 1) * GRID_W)
            mix = (_dot(t_ref[0, ts, T_OA:T_OA + D_A], wob_ref[0:D_A, :])
                   + _dot(mix_ref[ts], wob_ref[D_A:D_A + D_B, :]))
            o_ref[0, ts] = x_ref[0, ts] + gate * mix


def _attn_call(x, mod, tok, k, v, kc, vc, tab, w_out):
    B, L, _ = x.shape
    C = kc.shape[1]
    n_groups = HEADS // HEAD_GROUP
    tok_spec = lambda width: pl.BlockSpec((1, TQ, width), lambda b, i: (b, i, 0))
    batch_spec = lambda n: pl.BlockSpec((1, n, D_B), lambda b, i: (b, 0, 0))
    return pl.pallas_call(
        _attn_kernel,
        grid=(B, L // TQ),
        in_specs=[
            tok_spec(D_MODEL),
            pl.BlockSpec((MOD_BLK, D_MODEL), lambda b, i: (b // MOD_BLK, MOD_GATE)),
            tok_spec(TOK_W),
            batch_spec(L),
            batch_spec(L),
            batch_spec(C),
            batch_spec(C),
            pl.BlockSpec(tab.shape, lambda b, i: (0, 0, 0, 0)),
            pl.BlockSpec((D_A + D_B, D_MODEL), lambda b, i: (0, 0)),
        ],
        out_specs=tok_spec(D_MODEL),
        out_shape=jax.ShapeDtypeStruct((B, L, D_MODEL), jnp.float32),
        scratch_shapes=[pltpu.VMEM((TQ, D_B), jnp.bfloat16),
                        pltpu.VMEM((N_SLOTS, n_groups, GROUP_W, TKW + C), jnp.float32),
                        pltpu.VMEM((N_SLOTS, n_groups, GROUP_W, TKW + C), jnp.bfloat16),
                        pltpu.VMEM((N_SLOTS, n_groups, GROUP_W, 1), jnp.float32),
                        pltpu.VMEM((D_A + D_B, D_MODEL), jnp.bfloat16)],
        compiler_params=pltpu.CompilerParams(
            dimension_semantics=("arbitrary", "arbitrary"), vmem_limit_bytes=VMEM_LIMIT),
        name="nbr_attn_out",
    )(x, mod, tok, k, v, kc, vc, tab, w_out)


def kernel(x, c, ctx, c_ctx, w_ada, b_ada, norm_g, w_in, sgu_norm_g, w_spatial, b_spatial,
           q_norm_g, k_norm_g, rpb, w_out):
    depth = w_ada.shape[0]
    assert depth == 1
    for layer in range(depth):
        mod = _ada_call(c, c_ctx[None, :], w_ada[layer], b_ada[layer][None, :])

        ng = norm_g[layer][None, :]
        kg = k_norm_g[layer][None, :]
        qg = q_norm_g[layer][None, :]
        bs = jnp.repeat(b_spatial[layer].T, D_A // SGU_GROUPS, axis=1)
        tok, k, v = _proj_call(
            x, mod, ng, w_in[layer], sgu_norm_g[layer][None, :],
            w_spatial[layer], bs, qg, kg)
        kc, vc = _ctx_call(ctx, mod, ng, w_in[layer], kg)
        tab = _rpb_call(rpb[layer])
        x = _attn_call(x, mod, tok, k, v, kc, vc, tab, w_out[layer])
    return x
```

```python
import numpy as np
import jax
import jax.numpy as jnp
from jax import lax
from jax.experimental import pallas as pl
from jax.experimental.pallas import tpu as pltpu

D_MODEL = 1024
GRID_W = 64
D_A = 512
D_B = 512
CHUNK = 128
SGU_GROUPS = 4
HEAD_DIM = 64
HEADS = 8
HEAD_GROUP = 4
GROUP_W = HEAD_GROUP * HEAD_DIM
WIN_R = 8
WIN_C = 16
D_IN = 3 * D_A + 4 * D_B
EPS = 1e-6
NEG_INF = -1e30
LOG2E = float(np.log2(np.e))
Q_SCALE = HEAD_DIM ** -0.5 * LOG2E

LANES = 128
VMEM_LIMIT = 60 * 1024 * 1024

TM = 1024
SUB_TM = 512
Q_ROWS = 16
OUT_ROWS = 4
N_SLOTS = 4
CTX_TM = 1024
TQ = Q_ROWS * GRID_W
TKW = WIN_R * GRID_W
ADA_ROWS = 16
ADA_BK = 256
MOD_SHIFT, MOD_SCALE, MOD_GATE = 0, 1, 2
MOD_BLK = 8
T_OA, T_GB, T_Q, TOK_W = 0, D_A, D_A + D_B, D_A + 2 * D_B

_NT = (((1,), (1,)), ((), ()))


def _silu(x):
    half = 0.5 * x
    return half * (1.0 + jnp.tanh(half))


def _gelu(x):
    return 0.5 * x * (1.0 + lax.erf(x * np.float32(np.sqrt(0.5))))


def _dot(a, b):
    return jnp.dot(a, b, preferred_element_type=jnp.float32)


def _ada_kernel(c_ref, cctx_ref, w_ref, b_ref, o_ref):
    @pl.when(pl.program_id(0) == 0)
    def _():
        o_ref[...] = jnp.broadcast_to(b_ref[...], o_ref.shape)

    n_pad = ADA_ROWS - c_ref.shape[0] - 1
    cond = jnp.concatenate([c_ref[...], cctx_ref[...], jnp.zeros((n_pad, c_ref.shape[1]), jnp.float32)],
                           axis=0)
    a, w = _silu(cond), w_ref[...]
    a_hi, w_hi = a.astype(jnp.bfloat16), w.astype(jnp.bfloat16)
    a_lo = (a - a_hi.astype(jnp.float32)).astype(jnp.bfloat16)
    w_lo = (w - w_hi.astype(jnp.float32)).astype(jnp.bfloat16)
    o_ref[...] += _dot(a_hi, w_hi) + (_dot(a_lo, w_hi) + _dot(a_hi, w_lo))


def _ada_call(c, c_ctx, w_ada, b_ada):
    k, n = w_ada.shape
    B = c.shape[0]
    return pl.pallas_call(
        _ada_kernel,
        grid=(k // ADA_BK,),
        in_specs=[
            pl.BlockSpec((B, ADA_BK), lambda j: (0, j)),
            pl.BlockSpec((1, ADA_BK), lambda j: (0, j)),
            pl.BlockSpec((ADA_BK, n), lambda j: (j, 0)),
            pl.BlockSpec((1, n), lambda j: (0, 0)),
        ],
        out_specs=pl.BlockSpec((ADA_ROWS, n), lambda j: (0, 0)),
        out_shape=jax.ShapeDtypeStruct((ADA_ROWS, n), jnp.float32),
        compiler_params=pltpu.CompilerParams(dimension_semantics=("arbitrary",)),
        name="ada_params",
    )(c, c_ctx, w_ada, b_ada)


def _prenorm(x, g, shift, scale):
    ms = jnp.mean(x * x, axis=-1, keepdims=True)
    return (x * lax.rsqrt(ms + EPS) * (g * (1.0 + scale)) + shift).astype(jnp.bfloat16)


def _head_rms(z, head_gain, scale=1.0):
    gain = jnp.concatenate([head_gain * scale] * (z.shape[1] // HEAD_DIM), axis=-1)
    sq = z * z
    low = lax.broadcasted_iota(jnp.int32, (z.shape[0], LANES), 1) < HEAD_DIM
    sums = []
    for c in range(z.shape[1] // LANES):
        blk = sq[:, c * LANES:(c + 1) * LANES]
        s_lo = jnp.sum(jnp.where(low, blk, 0.0), axis=-1, keepdims=True)
        s_hi = jnp.sum(jnp.where(low, 0.0, blk), axis=-1, keepdims=True)
        sums.append(jnp.where(low, s_lo, s_hi))
    ss = jnp.concatenate(sums, axis=-1)
    return z * lax.rsqrt(ss * (1.0 / HEAD_DIM) + EPS) * gain


def _proj_kernel(x_ref, shift_ref, scale_ref, ng_ref, w_ref, sg_ref, ws_ref, bs_ref,
                 qg_ref, kg_ref, t_ref, k_ref, v_ref, wb_ref):
    @pl.when((pl.program_id(0) == 0) & (pl.program_id(1) == 0))
    def _():
        for lo in range(0, D_IN, D_A):
            wb_ref[:, lo:lo + D_A] = w_ref[:, lo:lo + D_A].astype(wb_ref.dtype)

    row = pl.ds(pl.program_id(0) % MOD_BLK, 1)
    shift, scale = shift_ref[row, :], scale_ref[row, :]
    for sub in range(TM // SUB_TM):
        _proj_rows(slice(sub * SUB_TM, (sub + 1) * SUB_TM), x_ref, shift, scale, ng_ref,
                   wb_ref, sg_ref, ws_ref, bs_ref, qg_ref, kg_ref, t_ref, k_ref, v_ref)


def _proj_rows(rows, x_ref, shift, scale, ng_ref, wb_ref, sg_ref, ws_ref, bs_ref,
               qg_ref, kg_ref, t_ref, k_ref, v_ref):
    hb = _prenorm(x_ref[0, rows], ng_ref[...], shift, scale)

    def zcols(lo, width):
        return _dot(hb, wb_ref[:, lo:lo + width])

    gu = _gelu(zcols(0, D_A))
    t_ref[0, rows, T_Q:T_Q + D_B] = _head_rms(
        zcols(3 * D_A, D_B), qg_ref[...], Q_SCALE).astype(t_ref.dtype)
    gv = _gelu(zcols(D_A, D_A))
    k_ref[0, rows] = _head_rms(zcols(3 * D_A + D_B, D_B), kg_ref[...]).astype(k_ref.dtype)
    sa = _silu(zcols(2 * D_A, D_A))
    t_ref[0, rows, T_GB:T_GB + D_B] = _silu(zcols(3 * D_A + 3 * D_B, D_B)).astype(t_ref.dtype)
    v_ref[0, rows] = zcols(3 * D_A + 2 * D_B, D_B).astype(v_ref.dtype)

    for g in range(SGU_GROUPS):
        cs = slice(g * LANES, (g + 1) * LANES)
        vg = gv[:, cs]
        ms = jnp.mean(vg * vg, axis=-1, keepdims=True)
        vn = (vg * lax.rsqrt(ms + EPS) * sg_ref[:, cs]).astype(jnp.bfloat16)
        chunks = [slice(c * CHUNK, (c + 1) * CHUNK) for c in range(SUB_TM // CHUNK)]
        mixed = _dot(ws_ref[g].astype(jnp.bfloat16), jnp.concatenate([vn[rs] for rs in chunks], axis=1))
        for c, rs in enumerate(chunks):
            mixed_c = mixed[:, c * CHUNK:(c + 1) * CHUNK] + bs_ref[:, cs]
            out_rows = slice(rows.start + rs.start, rows.start + rs.stop)
            t_ref[0, out_rows, T_OA + g * LANES:T_OA + (g + 1) * LANES] = (
                gu[rs, cs] * mixed_c * sa[rs, cs]).astype(t_ref.dtype)


def _proj_call(x, mod, ng, w_in, sg, ws, bs, qg, kg):
    B, L, _ = x.shape
    const2 = lambda b, i: (0, 0)
    half_spec = pl.BlockSpec((1, TM, D_A), lambda b, i: (b, i, 0))
    half_shape = jax.ShapeDtypeStruct((B, L, D_A), jnp.bfloat16)
    return pl.pallas_call(
        _proj_kernel,
        grid=(B, L // TM),
        in_specs=[
            pl.BlockSpec((1, TM, D_MODEL), lambda b, i: (b, i, 0)),
            pl.BlockSpec((MOD_BLK, D_MODEL), lambda b, i: (b // MOD_BLK, MOD_SHIFT)),
            pl.BlockSpec((MOD_BLK, D_MODEL), lambda b, i: (b // MOD_BLK, MOD_SCALE)),
            pl.BlockSpec((1, D_MODEL), const2),
            pl.BlockSpec((D_MODEL, D_IN), const2),
            pl.BlockSpec((1, D_A), const2),
            pl.BlockSpec((SGU_GROUPS, CHUNK, CHUNK), lambda b, i: (0, 0, 0)),
            pl.BlockSpec((CHUNK, D_A), const2),
            pl.BlockSpec((1, HEAD_DIM), const2),
            pl.BlockSpec((1, HEAD_DIM), const2),
        ],
        out_specs=[pl.BlockSpec((1, TM, TOK_W), lambda b, i: (b, i, 0)), half_spec, half_spec],
        out_shape=[jax.ShapeDtypeStruct((B, L, TOK_W), jnp.bfloat16), half_shape, half_shape],
        scratch_shapes=[pltpu.VMEM((D_MODEL, D_IN), jnp.bfloat16)],
        compiler_params=pltpu.CompilerParams(
            dimension_semantics=("arbitrary", "arbitrary"), vmem_limit_bytes=VMEM_LIMIT),
        name="latent_proj",
    )(x, mod, mod, ng, w_in, sg, ws, bs, qg, kg)


def _ctx_kernel(x_ref, shift_ref, scale_ref, ng_ref, w_ref, kg_ref, k_ref, v_ref):
    hb = _prenorm(x_ref[...], ng_ref[...], shift_ref[0:1, :], scale_ref[0:1, :])
    wk = w_ref[:, 0:D_B].astype(jnp.bfloat16)
    wv = w_ref[:, D_B:2 * D_B].astype(jnp.bfloat16)
    k_ref[...] = _head_rms(_dot(hb, wk), kg_ref[...]).astype(k_ref.dtype)
    v_ref[...] = _dot(hb, wv).astype(v_ref.dtype)


def _ctx_call(ctx, mod, ng, w_in, kg):
    B, C, _ = ctx.shape
    assert B % MOD_BLK == 0
    const2 = lambda b: (0, 0)
    kv_block = (3 * D_A + D_B) // (2 * D_B)
    kv_spec = pl.BlockSpec((CTX_TM, D_B), lambda b: (b, 0))
    kv_shape = jax.ShapeDtypeStruct((B * C, D_B), jnp.bfloat16)
    kc, vc = pl.pallas_call(
        _ctx_kernel,
        grid=(B * C // CTX_TM,),
        in_specs=[
            pl.BlockSpec((CTX_TM, D_MODEL), lambda b: (b, 0)),
            pl.BlockSpec((MOD_BLK, D_MODEL), lambda b: (B // MOD_BLK, MOD_SHIFT)),
            pl.BlockSpec((MOD_BLK, D_MODEL), lambda b: (B // MOD_BLK, MOD_SCALE)),
            pl.BlockSpec((1, D_MODEL), const2),
            pl.BlockSpec((D_MODEL, 2 * D_B), lambda b: (0, kv_block)),
            pl.BlockSpec((1, HEAD_DIM), const2),
        ],
        out_specs=[kv_spec, kv_spec],
        out_shape=[kv_shape, kv_shape],
        compiler_params=pltpu.CompilerParams(
            dimension_semantics=("arbitrary",), vmem_limit_bytes=VMEM_LIMIT),
        name="ctx_kv",
    )(ctx.reshape(B * C, D_MODEL), mod, mod, ng, w_in, kg)
    return kc.reshape(B, C, D_B), vc.reshape(B, C, D_B)


def _rpb_kernel(rpb_ref, win_ref, o_ref):
    n_heads, n_pairs = o_ref.shape[:2]
    n_dr = n_pairs + 1
    blk = o_ref.shape[2:]
    in_win = win_ref[...] != 0
    left_half = lax.broadcasted_iota(jnp.int32, blk, 1) < GRID_W
    first = LANES - (WIN_C - 1)
    for h in range(n_heads):
        for dr in range(n_pairs):
            row = h * n_dr + dr
            lo = pltpu.roll(jnp.broadcast_to(rpb_ref[row:row + 1, :], blk), first, 1,
                            stride=1, stride_axis=0)
            hi = pltpu.roll(jnp.broadcast_to(rpb_ref[row + 1:row + 2, :], blk),
                            (first + GRID_W) % LANES, 1, stride=1, stride_axis=0)
            o_ref[h, dr] = jnp.where(in_win, jnp.where(left_half, lo, hi) * LOG2E, NEG_INF)


def _rpb_call(rpb):
    H, n_dr, n_dc = rpb.shape
    cols = np.arange(GRID_W)
    c0 = np.clip(cols - WIN_C // 2, 0, GRID_W - WIN_C)
    in_win = (cols[None, :] >= c0[:, None]) & (cols[None, :] < c0[:, None] + WIN_C)
    win = jnp.asarray(np.tile(in_win, (1, LANES // GRID_W)), jnp.int32)
    rpb2 = jnp.pad(rpb.astype(jnp.float32).reshape(H * n_dr, n_dc), ((0, 0), (0, LANES - n_dc)))
    return pl.pallas_call(
        _rpb_kernel,
        out_shape=jax.ShapeDtypeStruct((H, n_dr - 1, GRID_W, LANES), jnp.float32),
        name="rpb_toeplitz",
    )(rpb2, win)


def _attn_kernel(x_ref, gate_ref, t_ref, k_ref, v_ref, kc_ref, vc_ref,
                 tab_ref, wout_ref, o_ref, mix_ref, s_ref, p_ref, l_ref, wob_ref):
    i = pl.program_id(1)

    @pl.when((pl.program_id(0) == 0) & (i == 0))
    def _():
        wob_ref[...] = wout_ref[...].astype(wob_ref.dtype)

    gate = gate_ref[pl.ds(pl.program_id(0) % MOD_BLK, 1), :]
    rows = pl.num_programs(1) * Q_ROWS
    shape = (GROUP_W, GROUP_W)
    row_head = lax.broadcasted_iota(jnp.int32, shape, 0) // HEAD_DIM
    lane_head = lax.broadcasted_iota(jnp.int32, shape, 1) // HEAD_DIM
    own_head = row_head == lane_head

    n_ctx = kc_ref.shape[1]
    groups = range(HEADS // HEAD_GROUP)

    def window(rho):
        r = i * Q_ROWS + rho
        r0 = jnp.clip(r - WIN_R // 2, 0, rows - WIN_R)
        d0 = r0 - r + (WIN_R - 1)
        return pl.multiple_of(r0 * GRID_W, GRID_W), d0

    def qrows(rho):
        return slice(rho * GRID_W, (rho + 1) * GRID_W)

    def scores(rho, slot):
        tok0, d0 = window(rho)
        for g in groups:
            lanes = slice(g * GROUP_W, (g + 1) * GROUP_W)
            qr = t_ref[0, qrows(rho), T_Q + g * GROUP_W:T_Q + (g + 1) * GROUP_W]
            q_bd = jnp.where(own_head, jnp.concatenate([qr] * HEAD_GROUP, axis=0),
                             jnp.zeros(shape, qr.dtype))
            bias = jnp.concatenate(
                [jnp.concatenate([tab_ref[g * HEAD_GROUP + h, d0 + 2 * t] for t in range(WIN_R // 2)],
                                 axis=1) for h in range(HEAD_GROUP)], axis=0)
            s_ref[slot, g, :, 0:TKW] = lax.dot_general(
                q_bd, k_ref[0, pl.ds(tok0, TKW), lanes], _NT,
                preferred_element_type=jnp.float32) + bias
            s_ref[slot, g, :, TKW:TKW + n_ctx] = lax.dot_general(
                q_bd, kc_ref[0, :, lanes], _NT, preferred_element_type=jnp.float32)

    def softmax(slot):
        for g in groups:
            s = s_ref[slot, g]
            p = jnp.exp2(s - jnp.max(s, axis=-1, keepdims=True))
            l_ref[slot, g] = jnp.sum(p, axis=-1, keepdims=True)
            p_ref[slot, g] = p.astype(p_ref.dtype)

    def values(rho, slot):
        tok0, _ = window(rho)
        for g in groups:
            lanes = slice(g * GROUP_W, (g + 1) * GROUP_W)
            o = (_dot(p_ref[slot, g, :, 0:TKW], v_ref[0, pl.ds(tok0, TKW), lanes])
                 + _dot(p_ref[slot, g, :, TKW:TKW + n_ctx], vc_ref[0, :, lanes]))
            o = jnp.where(own_head, o / l_ref[slot, g], 0.0)
            og = o[0:GRID_W]
            for h in range(1, HEAD_GROUP):
                og = og + o[h * GRID_W:(h + 1) * GRID_W]
            mix_ref[qrows(rho), g * GROUP_W:(g + 1) * GROUP_W] = (
                og * t_ref[0, qrows(rho), T_GB + g * GROUP_W:T_GB + (g + 1) * GROUP_W].astype(jnp.float32)
            ).astype(mix_ref.dtype)

    scores(0, 0)
    for rho in range(Q_ROWS):
        if rho + 1 < Q_ROWS:
            scores(rho + 1, (rho + 1) % N_SLOTS)
        softmax(rho % N_SLOTS)
        values(rho, rho % N_SLOTS)
        if (rho + 1) % OUT_ROWS == 0:
            ts = slice((rho + 1 - OUT_ROWS) * GRID_W, (rho + 1) * GRID_W)
            mix = (_dot(t_ref[0, ts, T_OA:T_OA + D_A], wob_ref[0:D_A, :])
                   + _dot(mix_ref[ts], wob_ref[D_A:D_A + D_B, :]))
            o_ref[0, ts] = x_ref[0, ts] + gate * mix


def _attn_call(x, mod, tok, k, v, kc, vc, tab, w_out):
    B, L, _ = x.shape
    C = kc.shape[1]
    n_groups = HEADS // HEAD_GROUP
    tok_spec = lambda width: pl.BlockSpec((1, TQ, width), lambda b, i: (b, i, 0))
    batch_spec = lambda n: pl.BlockSpec((1, n, D_B), lambda b, i: (b, 0, 0))
    return pl.pallas_call(
        _attn_kernel,
        grid=(B, L // TQ),
        in_specs=[
            tok_spec(D_MODEL),
            pl.BlockSpec((MOD_BLK, D_MODEL), lambda b, i: (b // MOD_BLK, MOD_GATE)),
            tok_spec(TOK_W),
            batch_spec(L),
            batch_spec(L),
            batch_spec(C),
            batch_spec(C),
            pl.BlockSpec(tab.shape, lambda b, i: (0, 0, 0, 0)),
            pl.BlockSpec((D_A + D_B, D_MODEL), lambda b, i: (0, 0)),
        ],
        out_specs=tok_spec(D_MODEL),
        out_shape=jax.ShapeDtypeStruct((B, L, D_MODEL), jnp.float32),
        scratch_shapes=[pltpu.VMEM((TQ, D_B), jnp.bfloat16),
                        pltpu.VMEM((N_SLOTS, n_groups, GROUP_W, TKW + C), jnp.float32),
                        pltpu.VMEM((N_SLOTS, n_groups, GROUP_W, TKW + C), jnp.bfloat16),
                        pltpu.VMEM((N_SLOTS, n_groups, GROUP_W, 1), jnp.float32),
                        pltpu.VMEM((D_A + D_B, D_MODEL), jnp.bfloat16)],
        compiler_params=pltpu.CompilerParams(
            dimension_semantics=("arbitrary", "arbitrary"), vmem_limit_bytes=VMEM_LIMIT),
        name="nbr_attn_out",
    )(x, mod, tok, k, v, kc, vc, tab, w_out)


def kernel(x, c, ctx, c_ctx, w_ada, b_ada, norm_g, w_in, sgu_norm_g, w_spatial, b_spatial,
           q_norm_g, k_norm_g, rpb, w_out):
    depth = w_ada.shape[0]
    assert depth == 1
    for layer in range(depth):
        mod = _ada_call(c, c_ctx[None, :], w_ada[layer], b_ada[layer][None, :])

        ng = norm_g[layer][None, :]
        kg = k_norm_g[layer][None, :]
        qg = q_norm_g[layer][None, :]
        bs = jnp.repeat(b_spatial[layer].T, D_A // SGU_GROUPS, axis=1)
        tok, k, v = _proj_call(
            x, mod, ng, w_in[layer], sgu_norm_g[layer][None, :],
            w_spatial[layer], bs, qg, kg)
        kc, vc = _ctx_call(ctx, mod, ng, w_in[layer], kg)
        tab = _rpb_call(rpb[layer])
        x = _attn_call(x, mod, tok, k, v, kc, vc, tab, w_out[layer])
    return x
```

```python
import numpy as np
import jax
import jax.numpy as jnp
from jax import lax
from jax.experimental import pallas as pl
from jax.experimental.pallas import tpu as pltpu

D_MODEL = 1024
GRID_W = 64
D_A = 512
D_B = 512
CHUNK = 128
SGU_GROUPS = 4
HEAD_DIM = 64
HEADS = 8
HEAD_GROUP = 4
GROUP_W = HEAD_GROUP * HEAD_DIM
WIN_R = 8
WIN_C = 16
D_IN = 3 * D_A + 4 * D_B
EPS = 1e-6
NEG_INF = -1e30
LOG2E = float(np.log2(np.e))
Q_SCALE = HEAD_DIM ** -0.5 * LOG2E

LANES = 128
VMEM_LIMIT = 60 * 1024 * 1024

TM = 1024
SUB_TM = 512
Q_ROWS = 16
OUT_ROWS = 4
N_SLOTS = 4
CTX_TM = 1024
TQ = Q_ROWS * GRID_W
TKW = WIN_R * GRID_W
ADA_ROWS = 16
ADA_BK = 256
MOD_SHIFT, MOD_SCALE, MOD_GATE = 0, 1, 2
MOD_BLK = 8
T_OA, T_GB, T_Q, TOK_W = 0, D_A, D_A + D_B, D_A + 2 * D_B

_NT = (((1,), (1,)), ((), ()))


def _silu(x):
    half = 0.5 * x
    return half * (1.0 + jnp.tanh(half))


def _gelu(x):
    return 0.5 * x * (1.0 + lax.erf(x * np.float32(np.sqrt(0.5))))


def _dot(a, b):
    return jnp.dot(a, b, preferred_element_type=jnp.float32)


def _ada_kernel(c_ref, cctx_ref, w_ref, b_ref, o_ref):
    @pl.when(pl.program_id(0) == 0)
    def _():
        o_ref[...] = jnp.broadcast_to(b_ref[...], o_ref.shape)

    n_pad = ADA_ROWS - c_ref.shape[0] - 1
    cond = jnp.concatenate([c_ref[...], cctx_ref[...], jnp.zeros((n_pad, c_ref.shape[1]), jnp.float32)],
                           axis=0)
    a, w = _silu(cond), w_ref[...]
    a_hi, w_hi = a.astype(jnp.bfloat16), w.astype(jnp.bfloat16)
    a_lo = (a - a_hi.astype(jnp.float32)).astype(jnp.bfloat16)
    w_lo = (w - w_hi.astype(jnp.float32)).astype(jnp.bfloat16)
    o_ref[...] += _dot(a_hi, w_hi) + (_dot(a_lo, w_hi) + _dot(a_hi, w_lo))


def _ada_call(c, c_ctx, w_ada, b_ada):
    k, n = w_ada.shape
    B = c.shape[0]
    return pl.pallas_call(
        _ada_kernel,
        grid=(k // ADA_BK,),
        in_specs=[
            pl.BlockSpec((B, ADA_BK), lambda j: (0, j)),
            pl.BlockSpec((1, ADA_BK), lambda j: (0, j)),
            pl.BlockSpec((ADA_BK, n), lambda j: (j, 0)),
            pl.BlockSpec((1, n), lambda j: (0, 0)),
        ],
        out_specs=pl.BlockSpec((ADA_ROWS, n), lambda j: (0, 0)),
        out_shape=jax.ShapeDtypeStruct((ADA_ROWS, n), jnp.float32),
        compiler_params=pltpu.CompilerParams(dimension_semantics=("arbitrary",)),
        name="ada_params",
    )(c, c_ctx, w_ada, b_ada)


def _prenorm(x, g, shift, scale):
    ms = jnp.mean(x * x, axis=-1, keepdims=True)
    return (x * lax.rsqrt(ms + EPS) * (g * (1.0 + scale)) + shift).astype(jnp.bfloat16)


def _head_rms(z, head_gain, scale=1.0):
    gain = jnp.concatenate([head_gain * scale] * (z.shape[1] // HEAD_DIM), axis=-1)
    sq = z * z
    low = lax.broadcasted_iota(jnp.int32, (z.shape[0], LANES), 1) < HEAD_DIM
    sums = []
    for c in range(z.shape[1] // LANES):
        blk = sq[:, c * LANES:(c + 1) * LANES]
        s_lo = jnp.sum(jnp.where(low, blk, 0.0), axis=-1, keepdims=True)
        s_hi = jnp.sum(jnp.where(low, 0.0, blk), axis=-1, keepdims=True)
        sums.append(jnp.where(low, s_lo, s_hi))
    ss = jnp.concatenate(sums, axis=-1)
    return z * lax.rsqrt(ss * (1.0 / HEAD_DIM) + EPS) * gain


def _proj_kernel(x_ref, shift_ref, scale_ref, ng_ref, w_ref, sg_ref, ws_ref, bs_ref,
                 qg_ref, kg_ref, t_ref, k_ref, v_ref, wb_ref):
    @pl.when((pl.program_id(0) == 0) & (pl.program_id(1) == 0))
    def _():
        for lo in range(0, D_IN, D_A):
            wb_ref[:, lo:lo + D_A] = w_ref[:, lo:lo + D_A].astype(wb_ref.dtype)

    row = pl.ds(pl.program_id(0) % MOD_BLK, 1)
    shift, scale = shift_ref[row, :], scale_ref[row, :]
    for sub in range(TM // SUB_TM):
        _proj_rows(slice(sub * SUB_TM, (sub + 1) * SUB_TM), x_ref, shift, scale, ng_ref,
                   wb_ref, sg_ref, ws_ref, bs_ref, qg_ref, kg_ref, t_ref, k_ref, v_ref)


def _proj_rows(rows, x_ref, shift, scale, ng_ref, wb_ref, sg_ref, ws_ref, bs_ref,
               qg_ref, kg_ref, t_ref, k_ref, v_ref):
    hb = _prenorm(x_ref[0, rows], ng_ref[...], shift, scale)

    def zcols(lo, width):
        return _dot(hb, wb_ref[:, lo:lo + width])

    gu = _gelu(zcols(0, D_A))
    t_ref[0, rows, T_Q:T_Q + D_B] = _head_rms(
        zcols(3 * D_A, D_B), qg_ref[...], Q_SCALE).astype(t_ref.dtype)
    gv = _gelu(zcols(D_A, D_A))
    k_ref[0, rows] = _head_rms(zcols(3 * D_A + D_B, D_B), kg_ref[...]).astype(k_ref.dtype)
    sa = _silu(zcols(2 * D_A, D_A))
    t_ref[0, rows, T_GB:T_GB + D_B] = _silu(zcols(3 * D_A + 3 * D_B, D_B)).astype(t_ref.dtype)
    v_ref[0, rows] = zcols(3 * D_A + 2 * D_B, D_B).astype(v_ref.dtype)

    for g in range(SGU_GROUPS):
        cs = slice(g * LANES, (g + 1) * LANES)
        vg = gv[:, cs]
        ms = jnp.mean(vg * vg, axis=-1, keepdims=True)
        vn = (vg * lax.rsqrt(ms + EPS) * sg_ref[:, cs]).astype(jnp.bfloat16)
        chunks = [slice(c * CHUNK, (c + 1) * CHUNK) for c in range(SUB_TM // CHUNK)]
        mixed = _dot(ws_ref[g].astype(jnp.bfloat16), jnp.concatenate([vn[rs] for rs in chunks], axis=1))
        for c, rs in enumerate(chunks):
            mixed_c = mixed[:, c * CHUNK:(c + 1) * CHUNK] + bs_ref[:, cs]
            out_rows = slice(rows.start + rs.start, rows.start + rs.stop)
            t_ref[0, out_rows, T_OA + g * LANES:T_OA + (g + 1) * LANES] = (
                gu[rs, cs] * mixed_c * sa[rs, cs]).astype(t_ref.dtype)


def _proj_call(x, mod, ng, w_in, sg, ws, bs, qg, kg):
    B, L, _ = x.shape
    const2 = lambda b, i: (0, 0)
    half_spec = pl.BlockSpec((1, TM, D_A), lambda b, i: (b, i, 0))
    half_shape = jax.ShapeDtypeStruct((B, L, D_A), jnp.bfloat16)
    return pl.pallas_call(
        _proj_kernel,
        grid=(B, L // TM),
        in_specs=[
            pl.BlockSpec((1, TM, D_MODEL), lambda b, i: (b, i, 0)),
            pl.BlockSpec((MOD_BLK, D_MODEL), lambda b, i: (b // MOD_BLK, MOD_SHIFT)),
            pl.BlockSpec((MOD_BLK, D_MODEL), lambda b, i: (b // MOD_BLK, MOD_SCALE)),
            pl.BlockSpec((1, D_MODEL), const2),
            pl.BlockSpec((D_MODEL, D_IN), const2),
            pl.BlockSpec((1, D_A), const2),
            pl.BlockSpec((SGU_GROUPS, CHUNK, CHUNK), lambda b, i: (0, 0, 0)),
            pl.BlockSpec((CHUNK, D_A), const2),
            pl.BlockSpec((1, HEAD_DIM), const2),
            pl.BlockSpec((1, HEAD_DIM), const2),
        ],
        out_specs=[pl.BlockSpec((1, TM, TOK_W), lambda b, i: (b, i, 0)), half_spec, half_spec],
        out_shape=[jax.ShapeDtypeStruct((B, L, TOK_W), jnp.bfloat16), half_shape, half_shape],
        scratch_shapes=[pltpu.VMEM((D_MODEL, D_IN), jnp.bfloat16)],
        compiler_params=pltpu.CompilerParams(
            dimension_semantics=("arbitrary", "arbitrary"), vmem_limit_bytes=VMEM_LIMIT),
        name="latent_proj",
    )(x, mod, mod, ng, w_in, sg, ws, bs, qg, kg)


def _ctx_kernel(x_ref, shift_ref, scale_ref, ng_ref, w_ref, kg_ref, k_ref, v_ref):
    hb = _prenorm(x_ref[...], ng_ref[...], shift_ref[0:1, :], scale_ref[0:1, :])
    wk = w_ref[:, 0:D_B].astype(jnp.bfloat16)
    wv = w_ref[:, D_B:2 * D_B].astype(jnp.bfloat16)
    k_ref[...] = _head_rms(_dot(hb, wk), kg_ref[...]).astype(k_ref.dtype)
    v_ref[...] = _dot(hb, wv).astype(v_ref.dtype)


def _ctx_call(ctx, mod, ng, w_in, kg):
    B, C, _ = ctx.shape
    assert B % MOD_BLK == 0
    const2 = lambda b: (0, 0)
    kv_block = (3 * D_A + D_B) // (2 * D_B)
    kv_spec = pl.BlockSpec((CTX_TM, D_B), lambda b: (b, 0))
    kv_shape = jax.ShapeDtypeStruct((B * C, D_B), jnp.bfloat16)
    kc, vc = pl.pallas_call(
        _ctx_kernel,
        grid=(B * C // CTX_TM,),
        in_specs=[
            pl.BlockSpec((CTX_TM, D_MODEL), lambda b: (b, 0)),
            pl.BlockSpec((MOD_BLK, D_MODEL), lambda b: (B // MOD_BLK, MOD_SHIFT)),
            pl.BlockSpec((MOD_BLK, D_MODEL), lambda b: (B // MOD_BLK, MOD_SCALE)),
            pl.BlockSpec((1, D_MODEL), const2),
            pl.BlockSpec((D_MODEL, 2 * D_B), lambda b: (0, kv_block)),
            pl.BlockSpec((1, HEAD_DIM), const2),
        ],
        out_specs=[kv_spec, kv_spec],
        out_shape=[kv_shape, kv_shape],
        compiler_params=pltpu.CompilerParams(
            dimension_semantics=("arbitrary",), vmem_limit_bytes=VMEM_LIMIT),
        name="ctx_kv",
    )(ctx.reshape(B * C, D_MODEL), mod, mod, ng, w_in, kg)
    return kc.reshape(B, C, D_B), vc.reshape(B, C, D_B)


def _rpb_kernel(rpb_ref, win_ref, o_ref):
    n_heads, n_pairs = o_ref.shape[:2]
    n_dr = n_pairs + 1
    blk = o_ref.shape[2:]
    in_win = win_ref[...] != 0
    left_half = lax.broadcasted_iota(jnp.int32, blk, 1) < GRID_W
    first = LANES - (WIN_C - 1)
    for h in range(n_heads):
        for dr in range(n_pairs):
            row = h * n_dr + dr
            lo = pltpu.roll(jnp.broadcast_to(rpb_ref[row:row + 1, :], blk), first, 1,
                            stride=1, stride_axis=0)
            hi = pltpu.roll(jnp.broadcast_to(rpb_ref[row + 1:row + 2, :], blk),
                            (first + GRID_W) % LANES, 1, stride=1, stride_axis=0)
            o_ref[h, dr] = jnp.where(in_win, jnp.where(left_half, lo, hi) * LOG2E, NEG_INF)


def _rpb_call(rpb):
    H, n_dr, n_dc = rpb.shape
    cols = np.arange(GRID_W)
    c0 = np.clip(cols - WIN_C // 2, 0, GRID_W - WIN_C)
    in_win = (cols[None, :] >= c0[:, None]) & (cols[None, :] < c0[:, None] + WIN_C)
    win = jnp.asarray(np.tile(in_win, (1, LANES // GRID_W)), jnp.int32)
    rpb2 = jnp.pad(rpb.astype(jnp.float32).reshape(H * n_dr, n_dc), ((0, 0), (0, LANES - n_dc)))
    return pl.pallas_call(
        _rpb_kernel,
        out_shape=jax.ShapeDtypeStruct((H, n_dr - 1, GRID_W, LANES), jnp.float32),
        name="rpb_toeplitz",
    )(rpb2, win)


def _attn_kernel(x_ref, gate_ref, t_ref, k_ref, v_ref, kc_ref, vc_ref,
                 tab_ref, wout_ref, o_ref, mix_ref, s_ref, p_ref, l_ref, wob_ref):
    i = pl.program_id(1)

    @pl.when((pl.program_id(0) == 0) & (i == 0))
    def _():
        wob_ref[...] = wout_ref[...].astype(wob_ref.dtype)

    gate = gate_ref[pl.ds(pl.program_id(0) % MOD_BLK, 1), :]
    rows = pl.num_programs(1) * Q_ROWS
    shape = (GROUP_W, GROUP_W)
    row_head = lax.broadcasted_iota(jnp.int32, shape, 0) // HEAD_DIM
    lane_head = lax.broadcasted_iota(jnp.int32, shape, 1) // HEAD_DIM
    own_head = row_head == lane_head

    n_ctx = kc_ref.shape[1]
    groups = range(HEADS // HEAD_GROUP)

    def window(rho):
        r = i * Q_ROWS + rho
        r0 = jnp.clip(r - WIN_R // 2, 0, rows - WIN_R)
        d0 = r0 - r + (WIN_R - 1)
        return pl.multiple_of(r0 * GRID_W, GRID_W), d0

    def qrows(rho):
        return slice(rho * GRID_W, (rho + 1) * GRID_W)

    def scores(rho, slot):
        tok0, d0 = window(rho)
        for g in groups:
            lanes = slice(g * GROUP_W, (g + 1) * GROUP_W)
            qr = t_ref[0, qrows(rho), T_Q + g * GROUP_W:T_Q + (g + 1) * GROUP_W]
            q_bd = jnp.where(own_head, jnp.concatenate([qr] * HEAD_GROUP, axis=0),
                             jnp.zeros(shape, qr.dtype))
            bias = jnp.concatenate(
                [jnp.concatenate([tab_ref[g * HEAD_GROUP + h, d0 + 2 * t] for t in range(WIN_R // 2)],
                                 axis=1) for h in range(HEAD_GROUP)], axis=0)
            for j in range(HEAD_GROUP // 2):
                pr = slice(j * LANES, (j + 1) * LANES)
                pl_ = slice(g * GROUP_W + j * LANES, g * GROUP_W + (j + 1) * LANES)
                q_pair = q_bd[pr, pr]
                s_ref[slot, g, pr, 0:TKW] = lax.dot_general(
                    q_pair, k_ref[0, pl.ds(tok0, TKW), pl_], _NT,
                    preferred_element_type=jnp.float32) + bias[pr]
                s_ref[slot, g, pr, TKW:TKW + n_ctx] = lax.dot_general(
                    q_pair, kc_ref[0, :, pl_], _NT, preferred_element_type=jnp.float32)

    def softmax(slot):
        for g in groups:
            s = s_ref[slot, g]
            p = jnp.exp2(s - jnp.max(s, axis=-1, keepdims=True))
            l_ref[slot, g] = jnp.sum(p, axis=-1, keepdims=True)
            p_ref[slot, g] = p.astype(p_ref.dtype)

    def values(rho, slot):
        tok0, _ = window(rho)
        for g in groups:
            lanes = slice(g * GROUP_W, (g + 1) * GROUP_W)
            o = (_dot(p_ref[slot, g, :, 0:TKW], v_ref[0, pl.ds(tok0, TKW), lanes])
                 + _dot(p_ref[slot, g, :, TKW:TKW + n_ctx], vc_ref[0, :, lanes]))
            o = jnp.where(own_head, o / l_ref[slot, g], 0.0)
            og = o[0:GRID_W]
            for h in range(1, HEAD_GROUP):
                og = og + o[h * GRID_W:(h + 1) * GRID_W]
            mix_ref[qrows(rho), g * GROUP_W:(g + 1) * GROUP_W] = (
                og * t_ref[0, qrows(rho), T_GB + g * GROUP_W:T_GB + (g + 1) * GROUP_W].astype(jnp.float32)
            ).astype(mix_ref.dtype)

    scores(0, 0)
    for rho in range(Q_ROWS):
        if rho + 1 < Q_ROWS:
            scores(rho + 1, (rho + 1) % N_SLOTS)
        softmax(rho % N_SLOTS)
        values(rho, rho % N_SLOTS)
        if (rho + 1) % OUT_ROWS == 0:
            ts = slice((rho + 1 - OUT_ROWS) * GRID_W, (rho + 1) * GRID_W)
            mix = (_dot(t_ref[0, ts, T_OA:T_OA + D_A], wob_ref[0:D_A, :])
                   + _dot(mix_ref[ts], wob_ref[D_A:D_A + D_B, :]))
            o_ref[0, ts] = x_ref[0, ts] + gate * mix


def _attn_call(x, mod, tok, k, v, kc, vc, tab, w_out):
    B, L, _ = x.shape
    C = kc.shape[1]
    n_groups = HEADS // HEAD_GROUP
    tok_spec = lambda width: pl.BlockSpec((1, TQ, width), lambda b, i: (b, i, 0))
    batch_spec = lambda n: pl.BlockSpec((1, n, D_B), lambda b, i: (b, 0, 0))
    return pl.pallas_call(
        _attn_kernel,
        grid=(B, L // TQ),
        in_specs=[
            tok_spec(D_MODEL),
            pl.BlockSpec((MOD_BLK, D_MODEL), lambda b, i: (b // MOD_BLK, MOD_GATE)),
            tok_spec(TOK_W),
            batch_spec(L),
            batch_spec(L),
            batch_spec(C),
            batch_spec(C),
            pl.BlockSpec(tab.shape, lambda b, i: (0, 0, 0, 0)),
            pl.BlockSpec((D_A + D_B, D_MODEL), lambda b, i: (0, 0)),
        ],
        out_specs=tok_spec(D_MODEL),
        out_shape=jax.ShapeDtypeStruct((B, L, D_MODEL), jnp.float32),
        scratch_shapes=[pltpu.VMEM((TQ, D_B), jnp.bfloat16),
                        pltpu.VMEM((N_SLOTS, n_groups, GROUP_W, TKW + C), jnp.float32),
                        pltpu.VMEM((N_SLOTS, n_groups, GROUP_W, TKW + C), jnp.bfloat16),
                        pltpu.VMEM((N_SLOTS, n_groups, GROUP_W, 1), jnp.float32),
                        pltpu.VMEM((D_A + D_B, D_MODEL), jnp.bfloat16)],
        compiler_params=pltpu.CompilerParams(
            dimension_semantics=("arbitrary", "arbitrary"), vmem_limit_bytes=VMEM_LIMIT),
        name="nbr_attn_out",
    )(x, mod, tok, k, v, kc, vc, tab, w_out)


def kernel(x, c, ctx, c_ctx, w_ada, b_ada, norm_g, w_in, sgu_norm_g, w_spatial, b_spatial,
           q_norm_g, k_norm_g, rpb, w_out):
    depth = w_ada.shape[0]
    assert depth == 1
    for layer in range(depth):
        mod = _ada_call(c, c_ctx[None, :], w_ada[layer], b_ada[layer][None, :])

        ng = norm_g[layer][None, :]
        kg = k_norm_g[layer][None, :]
        qg = q_norm_g[layer][None, :]
        bs = jnp.repeat(b_spatial[layer].T, D_A // SGU_GROUPS, axis=1)
        tok, k, v = _proj_call(
            x, mod, ng, w_in[layer], sgu_norm_g[layer][None, :],
            w_spatial[layer], bs, qg, kg)
        kc, vc = _ctx_call(ctx, mod, ng, w_in[layer], kg)
        tab = _rpb_call(rpb[layer])
        x = _attn_call(x, mod, tok, k, v, kc, vc, tab, w_out[layer])
    return x
```
